```python
import math
import jax
import jax.numpy as jnp
from jax import lax
import numpy as np

D_MODEL = 4096
BATCH = 4
SEQ = 2048
DEPTH = 2
DEC_BATCH = 8
DEC_SEQ = 8
PAST_LEN = 16384
PAGE_SIZE = 128

H_A = 8
HD_A = 128
A_BRANCHES = ((128, 1), (512, 4), (2048, 16))
A_WINDOW = 2048
A_BLOCK = 128
H_B = 8
HD_BK = 128
HD_BV = 128
H_C_QK = 4
H_C_V = 8
HD_C = 128
C_CONV = 4
H_D = 4
HD_D = 256
CHUNK = 64
D_FF = 11008
FFN_CONV = 3
D_PLE = 256
EPS = 1e-6
NEG = -1e30

W_A = H_A * HD_A
W_B = H_B * HD_BV
W_CQK = H_C_QK * HD_C
W_CV = H_C_V * HD_C
W_D = H_D * HD_D
MIX = W_A + W_B + W_CV + W_D
C_CONV_CH = 2 * W_CQK + W_CV
IN_SIZES = (W_A, W_A, W_A,
            H_B * HD_BK, H_B * HD_BK, W_B, W_B,
            W_CQK, W_CQK, W_CV, W_CV, H_C_V, H_C_V,
            W_D, W_D, W_D, W_D)
IN_TOTAL = sum(IN_SIZES)

kernel_name = 'hybrid_dilated_hgrn2_gdn_retention_step'

F32 = jnp.float32


def rms_norm(x, g):
    xf = x.astype(F32)
    y = xf * lax.rsqrt(jnp.mean(xf * xf, axis=-1, keepdims=True) + EPS)
    return (y * g.astype(F32)).astype(x.dtype)


def group_norm(x, g):
    xf = x.astype(F32)
    mu = jnp.mean(xf, axis=-1, keepdims=True)
    var = jnp.mean(jnp.square(xf - mu), axis=-1, keepdims=True)
    return ((xf - mu) * lax.rsqrt(var + EPS) * g.astype(F32)).astype(x.dtype)


def l2_norm(x):
    return x * lax.rsqrt(jnp.sum(x * x, axis=-1, keepdims=True) + EPS)


def causal_dwconv(x, buf, w):
    width = w.shape[0]
    t = x.shape[1]
    xx = jnp.concatenate([buf.astype(x.dtype), x], axis=1)
    wx = w.astype(x.dtype)
    y = xx[:, 0:t] * wx[0]
    for j in range(1, width):
        y = y + xx[:, j:j + t] * wx[j]
    return y, xx[:, xx.shape[1] - (width - 1):]


def alibi_slopes():
    return jnp.exp2(-8.0 * jnp.arange(1, H_A + 1, dtype=F32) / H_A)


def to_chunks(u, c):
    b, t = u.shape[:2]
    nc = -(-t // c)
    u = jnp.pad(u, [(0, 0), (0, nc * c - t)] + [(0, 0)] * (u.ndim - 2))
    u = u.reshape((b, nc, c) + u.shape[2:])
    return jnp.moveaxis(jnp.swapaxes(u, 2, 3), 1, 0)


def from_chunks(o, t):
    o = jnp.swapaxes(jnp.moveaxis(o, 0, 1), 2, 3)
    return o.reshape((o.shape[0], -1) + o.shape[3:])[:, :t]


def masked_softmax_parts(s, valid, v, eq):
    s = jnp.where(valid, s, NEG)
    m = jnp.max(s, axis=-1, keepdims=True)
    p = jnp.exp(s - m)
    l = jnp.sum(p, axis=-1, keepdims=True)
    return jnp.einsum(eq, p / l, v), (m + jnp.log(l))[..., 0]


def combine_branches(outs, lses):
    wts = jax.nn.softmax(jnp.stack(lses), axis=0)
    return jnp.einsum('gbth,gbthd->bthd', wts, jnp.stack(outs))


def band_branch(q, k, v, slopes, band, dil):
    n_, n, h, hd = q.shape
    nb = n // A_BLOCK

    def frame(u):
        u = u.reshape(n_, nb, A_BLOCK, h, hd)
        prev = jnp.pad(u, ((0, 0), (1, 0), (0, 0), (0, 0), (0, 0)))[:, :-1]
        return jnp.concatenate([prev, u], axis=2)

    qb = q.reshape(n_, nb, A_BLOCK, h, hd)
    kb, vb = frame(k), frame(v)
    dist = (jnp.arange(A_BLOCK)[:, None] + A_BLOCK) - jnp.arange(2 * A_BLOCK)[None, :]
    kidx = jnp.arange(nb)[:, None, None] * A_BLOCK + jnp.arange(2 * A_BLOCK)[None, None, :] - A_BLOCK
    valid = (dist >= 0) & (dist <= band) & (kidx >= 0)
    s = jnp.einsum('nbqhd,nbkhd->nbhqk', qb, kb) - slopes[:, None, None] * (dil * dist)
    o, lse = masked_softmax_parts(s, valid[:, None], vb, 'nbhqk,nbkhd->nbqhd')
    return o.reshape(n_, n, h, hd), jnp.transpose(lse, (0, 1, 3, 2)).reshape(n_, n, h)


def dilated_attention_prompt(q, k, v, slopes):
    b, s, h, hd = q.shape
    outs, lses = [], []
    for window, dil in A_BRANCHES:
        n = -(-s // (dil * A_BLOCK)) * A_BLOCK
        pad = n * dil - s

        def split(u):
            u = jnp.pad(u, ((0, 0), (0, pad), (0, 0), (0, 0)))
            return u.reshape(b, n, dil, h, hd).transpose(0, 2, 1, 3, 4).reshape(b * dil, n, h, hd)

        o, lse = band_branch(split(q), split(k), split(v), slopes, window // dil, dil)
        outs.append(o.reshape(b, dil, n, h, hd).transpose(0, 2, 1, 3, 4).reshape(b, n * dil, h, hd)[:, :s])
        lses.append(lse.reshape(b, dil, n, h).transpose(0, 2, 1, 3).reshape(b, n * dil, h)[:, :s])
    return combine_branches(outs, lses)


def dilated_attention_sample(q, k_all, v_all, slopes, n_past):
    t = q.shape[1]
    outs, lses = [], []
    for window, dil in A_BRANCHES:
        off = dil * jnp.arange(window // dil + 1)
        idx = n_past + jnp.arange(t)[:, None] - off[None, :]
        valid = idx >= 0
        idxc = jnp.clip(idx, 0, None)
        kg = jnp.take(k_all, idxc, axis=1)
        vg = jnp.take(v_all, idxc, axis=1)
        s = jnp.einsum('bthd,btkhd->bhtk', q, kg) - slopes[:, None, None] * off
        o, lse = masked_softmax_parts(s, valid, vg, 'bhtk,btkhd->bthd')
        outs.append(o)
        lses.append(jnp.transpose(lse, (0, 2, 1)))
    return combine_branches(outs, lses)


def gla_chunked(q, k, v, log_g, s0):
    t = q.shape[1]
    c = min(CHUNK, t)
    causal = jnp.tril(jnp.ones((c, c), bool))

    def step(s, inp):
        qi, ki, vi, gi = inp
        b = jnp.cumsum(gi, axis=2)
        rel = jnp.where(causal[:, :, None], b[:, :, :, None] - b[:, :, None], NEG)
        att = jnp.einsum('bhtd,bhtjd,bhjd->bhtj', qi, jnp.exp(rel), ki)
        o = jnp.einsum('bhtd,bhdv->bhtv', qi * jnp.exp(b), s) + jnp.einsum('bhtj,bhjv->bhtv', att, vi)
        b_end = b[:, :, -1]
        s = jnp.exp(b_end)[..., None] * s + jnp.einsum('bhjd,bhjv->bhdv', ki * jnp.exp(b_end[:, :, None] - b), vi)
        return s, o

    s_fin, o = lax.scan(step, s0, (to_chunks(q, c), to_chunks(k, c), to_chunks(v, c), to_chunks(log_g, c)))
    return from_chunks(o, t), s_fin


def retention_chunked(q, k, v, log_g, s0):
    t = q.shape[1]
    c = min(CHUNK, t)
    causal = jnp.tril(jnp.ones((c, c), bool))

    def step(s, inp):
        qi, ki, vi, gi = inp
        b = jnp.cumsum(gi, axis=-1)
        dec = jnp.exp(jnp.where(causal, b[..., :, None] - b[..., None, :], NEG))
        att = jnp.einsum('bhtd,bhjd->bhtj', qi, ki) * dec
        o = jnp.exp(b)[..., None] * jnp.einsum('bhtd,bhdv->bhtv', qi, s) + jnp.einsum('bhtj,bhjv->bhtv', att, vi)
        s = jnp.exp(b[..., -1])[..., None, None] * s + jnp.einsum('bhjd,bhjv->bhdv', ki * jnp.exp(b[..., -1:] - b)[..., None], vi)
        return s, o

    s_fin, o = lax.scan(step, s0, (to_chunks(q, c), to_chunks(k, c), to_chunks(v, c), to_chunks(log_g, c)))
    return from_chunks(o, t), s_fin


def gated_delta_chunked(q, k, v, beta, log_a, s0):
    t = q.shape[1]
    c = min(CHUNK, t)
    causal = jnp.tril(jnp.ones((c, c), bool))
    strict = jnp.tril(jnp.ones((c, c), bool), -1)
    eye = jnp.eye(c, dtype=F32)

    def step(s, inp):
        qi, ki, vi, bi, gi = inp
        b = jnp.cumsum(gi, axis=-1)
        dec = jnp.exp(jnp.where(causal, b[..., :, None] - b[..., None, :], NEG))
        kk = jnp.einsum('bhtd,bhjd->bhtj', ki, ki) * dec
        a_mat = eye + jnp.where(strict, bi[..., :, None] * kk, 0.0)
        rhs = bi[..., None] * (vi - jnp.exp(b)[..., None] * jnp.einsum('bhtd,bhdv->bhtv', ki, s))
        u = lax.linalg.triangular_solve(a_mat, rhs, left_side=True, lower=True, unit_diagonal=True)
        att = jnp.einsum('bhtd,bhjd->bhtj', qi, ki) * dec
        o = jnp.exp(b)[..., None] * jnp.einsum('bhtd,bhdv->bhtv', qi, s) + jnp.einsum('bhtj,bhjv->bhtv', att, u)
        s = jnp.exp(b[..., -1])[..., None, None] * s + jnp.einsum('bhjd,bhjv->bhdv', ki * jnp.exp(b[..., -1:] - b)[..., None], u)
        return s, o

    xs = (to_chunks(q, c), to_chunks(k, c), to_chunks(v, c), to_chunks(beta, c), to_chunks(log_a, c))
    s_fin, o = lax.scan(step, s0, xs)
    return from_chunks(o, t), s_fin


def block(x, pe, kv_buf, s_hgrn, s_delta, buf_delta, s_ret, buf_ffn, lb, lw):
    dt = x.dtype
    bsz, t, _ = x.shape
    hn = rms_norm(x, lw['attn_norm'])
    z = jnp.matmul(hn, lw['w_in']).astype(F32)
    splits = [int(c) for c in np.cumsum(IN_SIZES)[:-1]]
    (a_q, a_k, a_v, b_q, b_f, b_i, b_g, c_q, c_k, c_v, c_z, c_b, c_a,
     r_q, r_k, r_v, r_g) = jnp.split(z, splits, axis=-1)

    def heads(u, h):
        return u.reshape(bsz, t, h, -1)

    qa = heads(a_q, H_A) * HD_A ** -0.5
    kv_new = jnp.stack([heads(a_k, H_A), heads(a_v, H_A)], axis=2)
    slopes = alibi_slopes()
    if kv_buf is None:
        o_a = dilated_attention_prompt(qa, kv_new[:, :, 0], kv_new[:, :, 1], slopes)
        kv_state = kv_new[:, t - min(A_WINDOW, t):]
    else:
        n_past = kv_buf.shape[1]
        kv_all = jnp.concatenate([kv_buf.astype(F32), kv_new], axis=1)
        o_a = dilated_attention_sample(qa, kv_all[:, :, 0], kv_all[:, :, 1], slopes, n_past)
        kv_state = kv_all[:, kv_all.shape[1] - n_past:]

    zf = heads(b_f, H_B)
    lbf = lb.astype(F32)
    k_b = (1.0 - lbf) * jax.nn.sigmoid(-zf)
    log_f = jnp.log1p(-k_b)
    o_b, s_hgrn_new = gla_chunked(jax.nn.silu(heads(b_q, H_B)), k_b, heads(b_i, H_B), log_f, s_hgrn.astype(F32))
    o_b = rms_norm(o_b, lw['hgrn_norm']) * jax.nn.silu(heads(b_g, H_B))

    conv_out, buf_delta_new = causal_dwconv(jnp.concatenate([c_q, c_k, c_v], axis=-1), buf_delta, lw['delta_conv'])
    conv_out = jax.nn.silu(conv_out)
    cq = l2_norm(heads(conv_out[..., :W_CQK], H_C_QK)) * HD_C ** -0.5
    ck = l2_norm(heads(conv_out[..., W_CQK:2 * W_CQK], H_C_QK))
    cv = heads(conv_out[..., 2 * W_CQK:], H_C_V)
    rep = H_C_V // H_C_QK
    beta = jax.nn.sigmoid(c_b)
    log_a = -jnp.exp(lw['delta_A_log'].astype(F32)) * jax.nn.softplus(c_a + lw['delta_dt_bias'].astype(F32))
    o_c, s_delta_new = gated_delta_chunked(jnp.repeat(cq, rep, axis=2), jnp.repeat(ck, rep, axis=2), cv,
                                           beta, log_a, s_delta.astype(F32))
    o_c = rms_norm(o_c, lw['delta_norm']) * jax.nn.silu(heads(c_z, H_C_V))

    log_gamma = jnp.log1p(-jnp.exp2(-5.0 - jnp.arange(H_D, dtype=F32)))
    o_d, s_ret_new = retention_chunked(heads(r_q, H_D), heads(r_k, H_D) * HD_D ** -0.5, heads(r_v, H_D),
                                       jnp.broadcast_to(log_gamma, (bsz, t, H_D)), s_ret.astype(F32))
    o_d = group_norm(o_d, lw['ret_norm']) * jax.nn.silu(heads(r_g, H_D))

    mix = jnp.concatenate([o.reshape(bsz, t, -1) for o in (o_a, o_b, o_c, o_d)], axis=-1).astype(dt)
    x = x + jnp.matmul(mix, lw['w_out'])

    hf = rms_norm(x, lw['ffn_norm'])
    g_conv, buf_ffn_new = causal_dwconv(jnp.matmul(hf, lw['w_gate']), buf_ffn, lw['ffn_conv'])
    x = x + jnp.matmul(jax.nn.silu(g_conv) * jnp.matmul(hf, lw['w_up']), lw['w_down'])

    gate = jax.nn.sigmoid(jnp.matmul(rms_norm(x, lw['ple_norm']), lw['ple_gate']))
    x = x + gate * jnp.matmul(pe.astype(dt), lw['ple_proj'])

    states = (kv_state.astype(dt), s_hgrn_new.astype(dt), s_delta_new.astype(dt),
              buf_delta_new.astype(dt), s_ret_new.astype(dt), buf_ffn_new.astype(dt))
    return x, states


def setup_inputs(seed: int = 0) -> dict:
    ks = jax.random.split(jax.random.key(seed), 32)
    cnt = [0]

    def nxt():
        cnt[0] += 1
        return ks[cnt[0] - 1]

    def nrm(shape, scale):
        return scale * jax.random.normal(nxt(), shape, F32)

    def gain(shape):
        return 1.0 + nrm(shape, 0.05)

    l_a = min(A_WINDOW, PAST_LEN)
    inputs = {}
    inputs['x_prompt'] = nrm((BATCH, SEQ, D_MODEL), 1.0)
    inputs['x_sample'] = nrm((DEC_BATCH, DEC_SEQ, D_MODEL), 1.0)
    inputs['cache_attn_kv'] = nrm((DEPTH, DEC_BATCH, l_a, 2, H_A, HD_A), 1.0)
    inputs['state_hgrn'] = nrm((DEPTH, DEC_BATCH, H_B, HD_BK, HD_BV), 0.3)
    inputs['state_delta'] = nrm((DEPTH, DEC_BATCH, H_C_V, HD_C, HD_C), 0.1)
    inputs['state_delta_conv'] = nrm((DEPTH, DEC_BATCH, C_CONV - 1, C_CONV_CH), 1.0)
    inputs['state_ret'] = nrm((DEPTH, DEC_BATCH, H_D, HD_D, HD_D), 0.1)
    inputs['state_ffn_conv'] = nrm((DEPTH, DEC_BATCH, FFN_CONV - 1, D_FF), 1.0)
    inputs['p_prompt'] = nrm((DEPTH, BATCH, SEQ, D_PLE), 1.0)
    inputs['p_sample'] = nrm((DEPTH, DEC_BATCH, DEC_SEQ, D_PLE), 1.0)
    inputs['attn_norm'] = gain((DEPTH, D_MODEL))
    inputs['w_in'] = nrm((DEPTH, D_MODEL, IN_TOTAL), D_MODEL ** -0.5)
    inputs['hgrn_lb'] = nrm((DEPTH, H_B, HD_BK), 1.0)
    inputs['hgrn_norm'] = gain((DEPTH, HD_BV))
    inputs['delta_conv'] = nrm((DEPTH, C_CONV, C_CONV_CH), C_CONV ** -0.5)
    inputs['delta_A_log'] = jnp.log(jax.random.uniform(nxt(), (DEPTH, H_C_V), F32, 1.0, 16.0))
    dt0 = jnp.exp(jax.random.uniform(nxt(), (DEPTH, H_C_V), F32, math.log(1e-3), math.log(1e-1)))
    inputs['delta_dt_bias'] = dt0 + jnp.log(-jnp.expm1(-dt0))
    inputs['delta_norm'] = gain((DEPTH, HD_C))
    inputs['ret_norm'] = gain((DEPTH, HD_D))
    inputs['w_out'] = nrm((DEPTH, MIX, D_MODEL), MIX ** -0.5)
    inputs['ffn_norm'] = gain((DEPTH, D_MODEL))
    inputs['w_gate'] = nrm((DEPTH, D_MODEL, D_FF), D_MODEL ** -0.5)
    inputs['w_up'] = nrm((DEPTH, D_MODEL, D_FF), D_MODEL ** -0.5)
    inputs['ffn_conv'] = nrm((DEPTH, FFN_CONV, D_FF), FFN_CONV ** -0.5)
    inputs['w_down'] = nrm((DEPTH, D_FF, D_MODEL), D_FF ** -0.5)
    inputs['ple_norm'] = gain((DEPTH, D_MODEL))
    inputs['ple_gate'] = nrm((DEPTH, D_MODEL, D_MODEL), D_MODEL ** -0.5)
    inputs['ple_proj'] = nrm((DEPTH, D_PLE, D_MODEL), D_PLE ** -0.5)
    inputs['final_norm'] = gain((D_MODEL,))
    return inputs


def reference(x_prompt, x_sample, cache_attn_kv, state_hgrn, state_delta, state_delta_conv, state_ret,
              state_ffn_conv, p_prompt, p_sample, attn_norm, w_in, hgrn_lb, hgrn_norm, delta_conv,
              delta_A_log, delta_dt_bias, delta_norm, ret_norm, w_out, ffn_norm, w_gate, w_up, ffn_conv,
              w_down, ple_norm, ple_gate, ple_proj, final_norm):
    sm = jax.nn.softmax(hgrn_lb.astype(F32), axis=0)
    lower_bounds = jnp.cumsum(sm, axis=0) - sm[0]
    xp, xs = x_prompt, x_sample
    bp = xp.shape[0]
    dt = xp.dtype
    st_p, st_s = [], []
    for l in range(DEPTH):
        lw = {'attn_norm': attn_norm[l], 'w_in': w_in[l], 'hgrn_norm': hgrn_norm[l],
              'delta_conv': delta_conv[l], 'delta_A_log': delta_A_log[l], 'delta_dt_bias': delta_dt_bias[l],
              'delta_norm': delta_norm[l], 'ret_norm': ret_norm[l], 'w_out': w_out[l],
              'ffn_norm': ffn_norm[l], 'w_gate': w_gate[l], 'w_up': w_up[l], 'ffn_conv': ffn_conv[l],
              'w_down': w_down[l], 'ple_norm': ple_norm[l], 'ple_gate': ple_gate[l], 'ple_proj': ple_proj[l]}
        xp, sp = block(xp, p_prompt[l], None,
                       jnp.zeros((bp, H_B, HD_BK, HD_BV), dt), jnp.zeros((bp, H_C_V, HD_C, HD_C), dt),
                       jnp.zeros((bp, C_CONV - 1, C_CONV_CH), dt), jnp.zeros((bp, H_D, HD_D, HD_D), dt),
                       jnp.zeros((bp, FFN_CONV - 1, D_FF), dt), lower_bounds[l], lw)
        xs, ss = block(xs, p_sample[l], cache_attn_kv[l], state_hgrn[l], state_delta[l], state_delta_conv[l],
                       state_ret[l], state_ffn_conv[l], lower_bounds[l], lw)
        st_p.append(sp)
        st_s.append(ss)

    def stack(sts, i):
        return jnp.stack([s[i] for s in sts])

    y_prompt = rms_norm(xp, final_norm)
    y_sample = rms_norm(xs, final_norm)
    return (y_prompt, y_sample,
            stack(st_p, 0), stack(st_s, 0),
            stack(st_p, 1), stack(st_s, 1),
            stack(st_p, 2), stack(st_s, 2),
            stack(st_p, 3), stack(st_s, 3),
            stack(st_p, 4), stack(st_s, 4),
            stack(st_p, 5), stack(st_s, 5))
```

```python
import functools
import math

import jax
import jax.numpy as jnp
from jax import lax
from jax.experimental import pallas as pl
from jax.experimental.pallas import tpu as pltpu

F32 = jnp.float32
BF16 = jnp.bfloat16
EPS = 1e-6
NEG = -1e30

H_A = 8
HD_A = 128
A_BRANCHES = ((128, 1), (512, 4), (2048, 16))
A_BLOCK = 128
H_B = 8
HD_B = 128
H_C_QK = 4
H_C_V = 8
HD_C = 128
C_CONV = 4
H_D = 4
HD_D = 256
FFN_CONV = 3

W_A = H_A * HD_A
W_B = H_B * HD_B
W_CQK = H_C_QK * HD_C
W_CV = H_C_V * HD_C
W_D = H_D * HD_D
W_CG = 128

VMEM_LIMIT_BYTES = 52 * 1024 * 1024
SUBLANES = 8
LANES = 128

_NN = (((1,), (0,)), ((), ()))
_NT = (((1,), (1,)), ((), ()))
_TN = (((0,), (0,)), ((), ()))


def _cparams(*sem):
    return pltpu.CompilerParams(dimension_semantics=sem, vmem_limit_bytes=VMEM_LIMIT_BYTES)


def _dg(a, b, dn=_NN):
    return lax.dot_general(a, b, dn, preferred_element_type=F32)


def _hi_lo(x):
    hi = x.astype(BF16)
    lo = (x - hi.astype(F32)).astype(BF16)
    return hi, lo


def _dot3(a, b, dn=_NN):
    ah, al = _hi_lo(a)
    bh, bl = _hi_lo(b)
    return _dg(ah, bh, dn) + (_dg(ah, bl, dn) + _dg(al, bh, dn))


def _split3(x):
    x1 = x.astype(BF16)
    r1 = x - x1.astype(F32)
    x2 = r1.astype(BF16)
    x3 = (r1 - x2.astype(F32)).astype(BF16)
    return x1, x2, x3


def _sel_dot(m, x):
    x1, x2, x3 = _split3(x)
    return _dg(m, x1) + (_dg(m, x2) + _dg(m, x3))


def _dot_sel(x, m):
    x1, x2, x3 = _split3(x)
    return _dg(x1, m) + (_dg(x2, m) + _dg(x3, m))


def _sigmoid(x):
    return 1.0 / (1.0 + jnp.exp(-x))


def _silu(x):
    return x * _sigmoid(x)


def _softplus(x):
    return jnp.maximum(x, 0.0) + jnp.log1p(jnp.exp(-jnp.abs(x)))


def _pad_rows(x, rows):
    if x.shape[0] == rows:
        return x
    return jnp.concatenate([x, jnp.zeros((rows - x.shape[0],) + x.shape[1:], x.dtype)], axis=0)


def _onehot(cond):
    return jnp.where(cond, 1.0, 0.0).astype(BF16)


def _pick(n, prefs):
    for p in prefs:
        if n % p == 0:
            return p
    return n


def _rmsnorm_kernel(x_ref, g_ref, o_ref):
    x = x_ref[...]
    y = x * lax.rsqrt(jnp.mean(x * x, axis=-1, keepdims=True) + EPS)
    o_ref[...] = (y * g_ref[...]).astype(o_ref.dtype)


def _rmsnorm(x, g, out_dtype):
    m, d = x.shape
    tm = _pick(m, (256, 64, 8))
    return pl.pallas_call(
        _rmsnorm_kernel,
        grid=(m // tm,),
        in_specs=[pl.BlockSpec((tm, d), lambda i: (i, 0)), pl.BlockSpec((1, d), lambda i: (0, 0))],
        out_specs=pl.BlockSpec((tm, d), lambda i: (i, 0)),
        out_shape=jax.ShapeDtypeStruct((m, d), out_dtype),
        compiler_params=_cparams("parallel"),
        name="rmsnorm",
    )(x, g.reshape(1, d))


def _mm_kernel(a_ref, w_ref, o_ref):
    o_ref[...] = _dg(a_ref[...], w_ref[...]).astype(o_ref.dtype)


def _matmul(a, w, out_dtype=F32):
    m, k = a.shape
    n = w.shape[1]
    tm = _pick(m, (1024, 512, 256, 64))
    tn = _pick(n, (512, 256, 128))
    return pl.pallas_call(
        _mm_kernel,
        grid=(m // tm, n // tn),
        in_specs=[pl.BlockSpec((tm, k), lambda i, j: (i, 0)), pl.BlockSpec((k, tn), lambda i, j: (0, j))],
        out_specs=pl.BlockSpec((tm, tn), lambda i, j: (i, j)),
        out_shape=jax.ShapeDtypeStruct((m, n), out_dtype),
        compiler_params=_cparams("parallel", "arbitrary"),
        name="matmul",
    )(a, w)


def _mm_res_kernel(*refs, nparts):
    a_refs = refs[:nparts]
    w_refs = refs[nparts:2 * nparts]
    x_ref = refs[2 * nparts]
    o_ref = refs[2 * nparts + 1]
    acc = x_ref[...]
    for a_ref, w_ref in zip(a_refs, w_refs):
        acc = acc + _dg(a_ref[...], w_ref[...])
    o_ref[...] = acc


def _matmul_residual(parts, w, x, tm_prefs, tn_prefs):
    m, n = x.shape
    kp = parts[0].shape[1]
    nparts = len(parts)
    tm = _pick(m, tm_prefs)
    tn = _pick(n, tn_prefs)
    in_specs = [pl.BlockSpec((tm, kp), lambda i, j: (i, 0)) for _ in parts]
    in_specs += [pl.BlockSpec((kp, tn), functools.partial(lambda i, j, p: (p, j), p=p)) for p in range(nparts)]
    in_specs += [pl.BlockSpec((tm, tn), lambda i, j: (i, j))]
    return pl.pallas_call(
        functools.partial(_mm_res_kernel, nparts=nparts),
        grid=(m // tm, n // tn),
        in_specs=in_specs,
        out_specs=pl.BlockSpec((tm, tn), lambda i, j: (i, j)),
        out_shape=jax.ShapeDtypeStruct((m, n), F32),
        compiler_params=_cparams("parallel", "arbitrary"),
        name="matmul_residual",
    )(*parts, *([w] * nparts), x)


def _ple_kernel(a_ref, wg_ref, pe_ref, wp_ref, x_ref, o_ref):
    gate = _sigmoid(_dg(a_ref[...], wg_ref[...]))
    o_ref[...] = x_ref[...] + gate * _dg(pe_ref[...], wp_ref[...])


def _ple(hp, wg, pe, wp, x):
    m, d = x.shape
    k = hp.shape[1]
    kp = pe.shape[1]
    tm = _pick(m, (1024, 512, 256, 64))
    tn = _pick(d, (512, 256, 128))
    return pl.pallas_call(
        _ple_kernel,
        grid=(m // tm, d // tn),
        in_specs=[pl.BlockSpec((tm, k), lambda i, j: (i, 0)),
                  pl.BlockSpec((k, tn), lambda i, j: (0, j)),
                  pl.BlockSpec((tm, kp), lambda i, j: (i, 0)),
                  pl.BlockSpec((kp, tn), lambda i, j: (0, j)),
                  pl.BlockSpec((tm, tn), lambda i, j: (i, j))],
        out_specs=pl.BlockSpec((tm, tn), lambda i, j: (i, j)),
        out_shape=jax.ShapeDtypeStruct((m, d), F32),
        compiler_params=_cparams("parallel", "arbitrary"),
        name="ple",
    )(hp, wg, pe, wp, x)


def _ffn_act(g, p1, p2, cw, u):
    gc = cw[0:1] * p2 + cw[1:2] * p1 + cw[2:3] * g
    return (_silu(gc) * u).astype(BF16)


def _ffn_gu_prompt_kernel(h_ref, halo_ref, wg_ref, wu_ref, cw_ref, o_ref, tail_ref, *, seq):
    tm = h_ref.shape[0]
    a = h_ref[...]
    g = _dg(a, wg_ref[...])
    u = _dg(a, wu_ref[...])
    gh = _dg(halo_ref[...], wg_ref[...])
    seq_start = lax.rem(pl.program_id(0) * tm, seq) == 0
    gh = jnp.where(seq_start, 0.0, gh)
    row = lax.broadcasted_iota(jnp.int32, g.shape, 0)
    p1 = jnp.where(row == 0, gh[7:8], pltpu.roll(g, 1, 0))
    p2 = jnp.where(row == 0, gh[6:7], jnp.where(row == 1, gh[7:8], pltpu.roll(g, 2, 0)))
    o_ref[...] = _ffn_act(g, p1, p2, cw_ref[...], u)
    tail_ref[0] = g[tm - SUBLANES:tm]


def _ffn_gate_up_prompt(hf, wg, wu, cw, seq):
    m, d = hf.shape
    f = wg.shape[1]
    tm = _pick(seq, (1024, 512, 256, 128, 64, 8))
    tf = _pick(f, (256, 128))
    hb = tm // SUBLANES
    out, tail = pl.pallas_call(
        functools.partial(_ffn_gu_prompt_kernel, seq=seq),
        grid=(m // tm, f // tf),
        in_specs=[pl.BlockSpec((tm, d), lambda i, j: (i, 0)),
                  pl.BlockSpec((SUBLANES, d), lambda i, j: (jnp.maximum(i * hb - 1, 0), 0)),
                  pl.BlockSpec((d, tf), lambda i, j: (0, j)),
                  pl.BlockSpec((d, tf), lambda i, j: (0, j)),
                  pl.BlockSpec((FFN_CONV, tf), lambda i, j: (0, j))],
        out_specs=[pl.BlockSpec((tm, tf), lambda i, j: (i, j)),
                   pl.BlockSpec((1, SUBLANES, tf), lambda i, j: (i, 0, j))],
        out_shape=[jax.ShapeDtypeStruct((m, f), BF16),
                   jax.ShapeDtypeStruct((m // tm, SUBLANES, f), F32)],
        compiler_params=_cparams("parallel", "arbitrary"),
        name="ffn_gate_up_prompt",
    )(hf, hf, wg, wu, cw)
    per_seq = seq // tm
    tail = tail.reshape(m // seq, per_seq, SUBLANES, f)[:, per_seq - 1, SUBLANES - (FFN_CONV - 1):]
    return out, tail


def _ffn_gu_sample_kernel(h_ref, wg_ref, wu_ref, cw_ref, b1_ref, b2_ref, o_ref, g_ref, *, t):
    a = h_ref[...]
    g = _dg(a, wg_ref[...])
    u = _dg(a, wu_ref[...])
    pos = lax.rem(lax.broadcasted_iota(jnp.int32, g.shape, 0), t)
    p1 = jnp.where(pos == 0, b1_ref[...], pltpu.roll(g, 1, 0))
    p2 = jnp.where(pos < 2, b2_ref[...], pltpu.roll(g, 2, 0))
    o_ref[...] = _ffn_act(g, p1, p2, cw_ref[...], u)
    g_ref[...] = g


def _ffn_gate_up_sample(hf, wg, wu, cw, buf, t):
    m, d = hf.shape
    f = wg.shape[1]
    nb = m // t
    tf = _pick(f, (256, 128))
    zeros = jnp.zeros((nb, t - 1, f), F32)
    b1 = jnp.concatenate([buf[:, 1:2], zeros], axis=1).reshape(m, f)
    b2 = jnp.concatenate([buf[:, 0:2], zeros[:, 1:]], axis=1).reshape(m, f)
    out, g = pl.pallas_call(
        functools.partial(_ffn_gu_sample_kernel, t=t),
        grid=(f // tf,),
        in_specs=[pl.BlockSpec((m, d), lambda j: (0, 0)),
                  pl.BlockSpec((d, tf), lambda j: (0, j)),
                  pl.BlockSpec((d, tf), lambda j: (0, j)),
                  pl.BlockSpec((FFN_CONV, tf), lambda j: (0, j)),
                  pl.BlockSpec((m, tf), lambda j: (0, j)),
                  pl.BlockSpec((m, tf), lambda j: (0, j))],
        out_specs=[pl.BlockSpec((m, tf), lambda j: (0, j)), pl.BlockSpec((m, tf), lambda j: (0, j))],
        out_shape=[jax.ShapeDtypeStruct((m, f), BF16), jax.ShapeDtypeStruct((m, f), F32)],
        compiler_params=_cparams("arbitrary"),
        name="ffn_gate_up_sample",
    )(hf, wg, wu, cw, b1, b2)
    tail = g.reshape(nb, t, f)[:, t - (FFN_CONV - 1):]
    return out, tail


def _band_kernel(*refs, dil, band, combine):
    if combine:
        q_ref, kc_ref, kp_ref, vc_ref, vp_ref, o2_ref, l2_ref, o3_ref, l3_ref, o_ref = refs
    else:
        q_ref, kc_ref, kp_ref, vc_ref, vp_ref, o_ref, lse_ref = refs
    blk = pl.program_id(2)
    nq = q_ref.shape[1]
    qi = lax.broadcasted_iota(jnp.int32, (nq, nq), 0)
    kj = lax.broadcasted_iota(jnp.int32, (nq, nq), 1)
    dist_c = qi - kj
    dist_p = dist_c + nq
    valid_c = dist_c >= 0
    valid_p = (dist_p <= band) & (blk > 0)
    dist_cf = dist_c.astype(F32)
    dist_pf = dist_p.astype(F32)
    lane = lax.broadcasted_iota(jnp.int32, (nq, LANES), 1)
    lse_all = jnp.zeros((nq, LANES), F32)
    for h in range(H_A):
        slope = float(dil) * 2.0 ** (-8.0 * (h + 1) / H_A)
        sl = slice(h * HD_A, (h + 1) * HD_A)
        q = q_ref[0, :, sl] * HD_A ** -0.5
        sc = jnp.where(valid_c, _dot3(q, kc_ref[0, :, sl], _NT) - slope * dist_cf, NEG)
        sp = jnp.where(valid_p, _dot3(q, kp_ref[0, :, sl], _NT) - slope * dist_pf, NEG)
        m = jnp.maximum(jnp.max(sc, axis=-1, keepdims=True), jnp.max(sp, axis=-1, keepdims=True))
        pc = jnp.exp(sc - m)
        pp = jnp.exp(sp - m)
        l = jnp.sum(pc, axis=-1, keepdims=True) + jnp.sum(pp, axis=-1, keepdims=True)
        o = (_dot3(pc, vc_ref[0, :, sl]) + _dot3(pp, vp_ref[0, :, sl])) / l
        lse = m + jnp.log(l)
        if combine:
            l2 = l2_ref[0, :, h:h + 1]
            l3 = l3_ref[0, :, h:h + 1]
            mx = jnp.maximum(lse, jnp.maximum(l2, l3))
            e1 = jnp.exp(lse - mx)
            e2 = jnp.exp(l2 - mx)
            e3 = jnp.exp(l3 - mx)
            tot = e1 + e2 + e3
            o = (e1 * o + e2 * o2_ref[0, :, sl] + e3 * o3_ref[0, :, sl]) / tot
            o_ref[0, :, sl] = o.astype(o_ref.dtype)
        else:
            o_ref[0, :, sl] = o
            lse_all = jnp.where(lane == h, lse, lse_all)
    if not combine:
        lse_ref[0] = lse_all


def _band_attention(za, dil, band, others=None):
    b, s, _ = za.shape
    assert s % (dil * A_BLOCK) == 0 and band <= A_BLOCK
    n = s // dil
    nb = n // A_BLOCK
    zr = za.reshape(b, n, dil * 3 * W_A)
    blk = (1, A_BLOCK, W_A)
    in_specs = [pl.BlockSpec(blk, lambda bi, r, i: (bi, i, 3 * r)),
                pl.BlockSpec(blk, lambda bi, r, i: (bi, i, 3 * r + 1)),
                pl.BlockSpec(blk, lambda bi, r, i: (bi, jnp.maximum(i - 1, 0), 3 * r + 1)),
                pl.BlockSpec(blk, lambda bi, r, i: (bi, i, 3 * r + 2)),
                pl.BlockSpec(blk, lambda bi, r, i: (bi, jnp.maximum(i - 1, 0), 3 * r + 2))]
    o_spec = pl.BlockSpec(blk, lambda bi, r, i: (bi, i, r))
    l_spec = pl.BlockSpec((1, A_BLOCK, LANES), lambda bi, r, i: (bi, i, r))
    args = [zr, zr, zr, zr, zr]
    if others is None:
        out_specs = [o_spec, l_spec]
        out_shape = [jax.ShapeDtypeStruct((b, n, dil * W_A), F32), jax.ShapeDtypeStruct((b, n, dil * LANES), F32)]
    else:
        assert dil == 1
        for o_g, l_g in others:
            in_specs += [o_spec, l_spec]
            args += [o_g, l_g]
        out_specs = o_spec
        out_shape = jax.ShapeDtypeStruct((b, n, W_A), BF16)
    res = pl.pallas_call(
        functools.partial(_band_kernel, dil=dil, band=band, combine=others is not None),
        grid=(b, dil, nb),
        in_specs=in_specs,
        out_specs=out_specs,
        out_shape=out_shape,
        compiler_params=_cparams("parallel", "parallel", "arbitrary"),
        name="band_attention_d%d" % dil,
    )(*args)
    if others is None:
        return res[0].reshape(b, s, W_A), res[1].reshape(b, s, LANES)
    return res


def _attention_prompt(za):
    others = []
    for window, dil in A_BRANCHES[1:]:
        others.append(_band_attention(za, dil, window // dil))
    window, dil = A_BRANCHES[0]
    assert dil == 1 and len(others) == 2
    return _band_attention(za, dil, window // dil, others)


def _attn_sample_kernel(q_ref, kn_ref, vn_ref, kc_ref, vc_ref, o_ref):
    t = q_ref.shape[1]
    n_past = kc_ref.shape[1]
    h = pl.program_id(1)
    slope = jnp.exp2(-8.0 * (h + 1).astype(F32) / H_A)
    tp = 2 * SUBLANES
    q = _pad_rows(q_ref[0] * HD_A ** -0.5, tp)
    kn = _pad_rows(kn_ref[0], tp)
    vn = _pad_rows(vn_ref[0], tp)
    s_c = _dot3(q, kc_ref[0], _NT)
    s_n = _dot3(q, kn, _NT)
    dist_c = (n_past + lax.broadcasted_iota(jnp.int32, s_c.shape, 0)
              - lax.broadcasted_iota(jnp.int32, s_c.shape, 1))
    dist_n = lax.broadcasted_iota(jnp.int32, s_n.shape, 0) - lax.broadcasted_iota(jnp.int32, s_n.shape, 1)
    new_ok = lax.broadcasted_iota(jnp.int32, s_n.shape, 1) < t
    b_c = s_c - slope * dist_c.astype(F32)
    b_n = s_n - slope * dist_n.astype(F32)
    outs, lses = [], []
    for window, dil in A_BRANCHES:
        ok_c = (dist_c <= window) & (lax.rem(dist_c, dil) == 0)
        ok_n = (dist_n >= 0) & (dist_n <= window) & (lax.rem(dist_n, dil) == 0) & new_ok
        sc = jnp.where(ok_c, b_c, NEG)
        sn = jnp.where(ok_n, b_n, NEG)
        m = jnp.maximum(jnp.max(sc, axis=-1, keepdims=True), jnp.max(sn, axis=-1, keepdims=True))
        pc = jnp.exp(sc - m)
        pn = jnp.exp(sn - m)
        l = jnp.sum(pc, axis=-1, keepdims=True) + jnp.sum(pn, axis=-1, keepdims=True)
        outs.append((_dot3(pc, vc_ref[0]) + _dot3(pn, vn)) / l)
        lses.append(m + jnp.log(l))
    mx = functools.reduce(jnp.maximum, lses)
    es = [jnp.exp(x - mx) for x in lses]
    tot = functools.reduce(lambda a, b: a + b, es)
    o = functools.reduce(lambda a, b: a + b, [e * x for e, x in zip(es, outs)]) / tot
    o_ref[0] = o[:t].astype(o_ref.dtype)


def _attention_sample(za, cache):
    b, t, _ = za.shape
    l = cache.shape[1]
    return pl.pallas_call(
        _attn_sample_kernel,
        grid=(b, H_A),
        in_specs=[pl.BlockSpec((1, t, HD_A), lambda bi, h: (bi, 0, h)),
                  pl.BlockSpec((1, t, HD_A), lambda bi, h: (bi, 0, H_A + h)),
                  pl.BlockSpec((1, t, HD_A), lambda bi, h: (bi, 0, 2 * H_A + h)),
                  pl.BlockSpec((1, l, HD_A), lambda bi, h: (bi, 0, h)),
                  pl.BlockSpec((1, l, HD_A), lambda bi, h: (bi, 0, H_A + h))],
        out_specs=pl.BlockSpec((1, t, HD_A), lambda bi, h: (bi, 0, h)),
        out_shape=jax.ShapeDtypeStruct((b, t, W_A), BF16),
        compiler_params=_cparams("parallel", "arbitrary"),
        name="attention_sample",
    )(za, za, za, cache, cache)


def _chunk_plan(t, chunk):
    c = chunk if t >= chunk else max(2 * SUBLANES, t)
    assert t % c == 0 or t < c
    tc = min(t, c)
    return c, tc, max(t // c, 1)


def _gla_kernel(q_ref, f_ref, i_ref, g_ref, lb_ref, nrm_ref, s0_ref, o_ref, sfin_ref, st_ref, *, layer, c, tc, nchunks):
    nlev = int(math.log2(c))
    assert 1 << nlev == c
    depth = lb_ref.shape[0]
    lbs = [lb_ref[i, 0] for i in range(depth)]
    mx = functools.reduce(jnp.maximum, lbs)
    es = [jnp.exp(x - mx) for x in lbs]
    tot = functools.reduce(lambda a, b: a + b, es)
    lower = functools.reduce(lambda a, b: a + b, [es[i] / tot for i in range(layer + 1)]) - es[0] / tot
    one_minus_lb = 1.0 - lower

    row = lax.broadcasted_iota(jnp.int32, (c, c), 0)
    col = lax.broadcasted_iota(jnp.int32, (c, c), 1)
    prefix = [_onehot(col <= row)]
    lmask = []
    for lev in range(1, nlev + 1):
        s = c >> lev
        grp = ~(2 * s - 1)
        prefix.append(_onehot(col <= (row & grp) + (s - 1)))
        lmask.append((((row ^ col) & grp) == 0) & ((row & s) != 0) & ((col & s) == 0))
    mstack = jnp.concatenate(prefix, axis=0)
    eye = row == col
    valid = lax.broadcasted_iota(jnp.int32, (c, 1), 0) < tc
    nrm = nrm_ref[...]

    st_ref[...] = s0_ref[0, 0].T

    def chunk(t0):
        rows = pl.ds(t0, tc)
        q = _silu(_pad_rows(q_ref[0, rows, :], c))
        kb = one_minus_lb * _sigmoid(-_pad_rows(f_ref[0, rows, :], c))
        logf = jnp.log1p(-kb)
        if tc < c:
            kb = jnp.where(valid, kb, 0.0)
            logf = jnp.where(valid, logf, 0.0)
        v = _pad_rows(i_ref[0, rows, :], c)
        bs = _sel_dot(mstack, logf)
        b = bs[0:c]
        att = jnp.where(eye, jnp.sum(q * kb, axis=-1, keepdims=True), 0.0)
        for lev in range(1, nlev + 1):
            br = bs[lev * c:(lev + 1) * c]
            ql = q * jnp.exp(jnp.minimum(b - br, 0.0))
            kl = kb * jnp.exp(jnp.minimum(br - b, 0.0))
            att = att + jnp.where(lmask[lev - 1], _dot3(ql, kl, _NT), 0.0)
        st = st_ref[...]
        o = _dot3(q * jnp.exp(b), st, _NT) + _dot3(att, v)
        bend = b[c - 1:c]
        st_ref[...] = st * jnp.exp(bend) + _dot3(v, kb * jnp.exp(bend - b), _TN)
        on = o * lax.rsqrt(jnp.mean(o * o, axis=-1, keepdims=True) + EPS) * nrm
        out = on * _silu(_pad_rows(g_ref[0, rows, :], c))
        o_ref[0, rows, :] = out[0:tc].astype(o_ref.dtype)

    if nchunks == 1:
        chunk(0)
    else:
        def body(ci, carry):
            chunk(pl.multiple_of(ci * c, c))
            return carry
        lax.fori_loop(0, nchunks, body, 0)
    sfin_ref[0, 0] = st_ref[...].T


def _hgrn(zb, lb, nrm, s0, layer, chunk):
    b, t, _ = zb.shape
    c, tc, nchunks = _chunk_plan(t, chunk)
    depth = lb.shape[0]
    blk = (1, t, HD_B)
    return pl.pallas_call(
        functools.partial(_gla_kernel, layer=layer, c=c, tc=tc, nchunks=nchunks),
        grid=(b, H_B),
        in_specs=[pl.BlockSpec(blk, lambda bi, h: (bi, 0, h)),
                  pl.BlockSpec(blk, lambda bi, h: (bi, 0, H_B + h)),
                  pl.BlockSpec(blk, lambda bi, h: (bi, 0, 2 * H_B + h)),
                  pl.BlockSpec(blk, lambda bi, h: (bi, 0, 3 * H_B + h)),
                  pl.BlockSpec((depth, 1, 1, HD_B), lambda bi, h: (0, h, 0, 0)),
                  pl.BlockSpec((1, HD_B), lambda bi, h: (0, 0)),
                  pl.BlockSpec((1, 1, HD_B, HD_B), lambda bi, h: (bi, h, 0, 0))],
        out_specs=[pl.BlockSpec(blk, lambda bi, h: (bi, 0, h)),
                   pl.BlockSpec((1, 1, HD_B, HD_B), lambda bi, h: (bi, h, 0, 0))],
        out_shape=[jax.ShapeDtypeStruct((b, t, W_B), BF16), jax.ShapeDtypeStruct((b, H_B, HD_B, HD_B), F32)],
        scratch_shapes=[pltpu.VMEM((HD_B, HD_B), F32)],
        compiler_params=_cparams("parallel", "arbitrary"),
        name="hgrn2",
    )(zb, zb, zb, zb, lb.reshape(depth, H_B, 1, HD_B), nrm.reshape(1, HD_B), s0)


DELTA_SUB = 16


def _delta_kernel(q_ref, k_ref, v_ref, z_ref, zg_ref, zgt_ref, wq_ref, wk_ref, wv_ref, bq_ref, bk_ref, bv_ref,
                  al_ref, dl_ref, ac_ref, dc_ref, nrm_ref, s0_ref, o_ref, sfin_ref, st_ref, *, c, tc, nchunks):
    h = pl.program_id(1)
    sb = min(DELTA_SUB, c)
    row = lax.broadcasted_iota(jnp.int32, (c, c), 0)
    col = lax.broadcasted_iota(jnp.int32, (c, c), 1)
    lower_incl = _onehot(col <= row)
    upper_incl = _onehot(row <= col)
    eye = jnp.where(row == col, 1.0, 0.0)
    same_sub = ((row ^ col) & ~(sb - 1)) == 0
    valid = lax.broadcasted_iota(jnp.int32, (c, 1), 0) < tc
    valid_lane = lax.broadcasted_iota(jnp.int32, (1, c), 1) < tc
    lane = lax.broadcasted_iota(jnp.int32, (c, LANES), 1)
    sub16 = lax.broadcasted_iota(jnp.int32, (2 * SUBLANES, c), 0)
    nrm = nrm_ref[...]
    st_ref[...] = s0_ref[0, 0].T

    def conv(x_ref, w_ref, buf_ref, ci, t0):
        x = x_ref[0, pl.ds(t0, tc), :]
        halo = buf_ref[0]
        if nchunks > 1:
            prev = x_ref[0, pl.ds(jnp.maximum(t0 - SUBLANES, 0), SUBLANES), :]
            halo = jnp.where(ci == 0, halo, prev)
        xh = jnp.concatenate([halo, x], axis=0)
        w = w_ref[...]
        y = w[C_CONV - 1:C_CONV] * x
        for s in range(1, C_CONV):
            y = y + w[C_CONV - 1 - s:C_CONV - s] * pltpu.roll(xh, s, 0)[SUBLANES:]
        return _pad_rows(_silu(y), c)

    def l2n(x):
        return x * lax.rsqrt(jnp.sum(x * x, axis=-1, keepdims=True) + EPS)

    def chunk(ci, t0):
        rows = pl.ds(t0, tc)
        q = l2n(conv(q_ref, wq_ref, bq_ref, ci, t0)) * HD_C ** -0.5
        k = l2n(conv(k_ref, wk_ref, bk_ref, ci, t0))
        v = conv(v_ref, wv_ref, bv_ref, ci, t0)
        zg = _pad_rows(zg_ref[0, rows, :], c)
        beta = jnp.sum(jnp.where(lane == h, _sigmoid(zg), 0.0), axis=-1, keepdims=True)
        la_all = -jnp.exp(al_ref[...]) * _softplus(zg + dl_ref[...])
        la = jnp.sum(jnp.where(lane == H_C_V + h, la_all, 0.0), axis=-1, keepdims=True)
        lat_all = -jnp.exp(ac_ref[...]) * _softplus(zgt_ref[0, ci] + dc_ref[...])
        lat = jnp.sum(jnp.where(sub16 == H_C_V + h, lat_all, 0.0), axis=0, keepdims=True)
        if tc < c:
            k = jnp.where(valid, k, 0.0)
            beta = jnp.where(valid, beta, 0.0)
            la = jnp.where(valid, la, 0.0)
            lat = jnp.where(valid_lane, lat, 0.0)
        bcol = _sel_dot(lower_incl, jnp.broadcast_to(la, (c, c)))
        brow = _dot_sel(jnp.broadcast_to(lat, (c, c)), upper_incl)
        dec = jnp.exp(jnp.where(row >= col, bcol - brow, NEG))
        b1 = bcol[:, 0:1]
        eb = jnp.exp(b1)
        bend = b1[c - 1:c]

        st = st_ref[...]
        n = jnp.where(row > col, beta * (_dot3(k, k, _NT) * dec), 0.0)
        nd = jnp.where(same_sub, n, 0.0)
        x = eye - nd
        p = nd
        for _ in range(int(math.log2(sb)) - 1):
            p = _dot3(p, p)
            x = x + _dot3(x, p)
        nblk = c // sb
        if nblk > 1:
            mm = _dot3(x, n - nd)
            y = eye - mm
            p = mm
            for _ in range(int(math.log2(nblk)) - 1):
                p = _dot3(p, p)
                y = y + _dot3(y, p)
            x = _dot3(y, x)
        rhs = beta * (v - eb * _dot3(k, st, _NT))
        u = _dot3(x, rhs)
        att = _dot3(q, k, _NT) * dec
        o = eb * _dot3(q, st, _NT) + _dot3(att, u)
        st_ref[...] = jnp.exp(bend) * st + _dot3(u, k * jnp.exp(bend - b1), _TN)
        on = o * lax.rsqrt(jnp.mean(o * o, axis=-1, keepdims=True) + EPS) * nrm
        out = on * _silu(_pad_rows(z_ref[0, rows, :], c))
        o_ref[0, rows, :] = out[0:tc].astype(o_ref.dtype)

    if nchunks == 1:
        chunk(0, 0)
    else:
        def body(ci, carry):
            chunk(ci, pl.multiple_of(ci * c, c))
            return carry
        lax.fori_loop(0, nchunks, body, 0)
    sfin_ref[0, 0] = st_ref[...].T


def _delta(zc, zg, conv_w, buf, a_log, dt_bias, nrm, s0, chunk):
    b, t, _ = zc.shape
    c, tc, nchunks = _chunk_plan(t, chunk)
    rep = H_C_V // H_C_QK
    nq = W_CQK // HD_C
    buf8 = jnp.pad(buf, ((0, 0), (SUBLANES - (C_CONV - 1), 0), (0, 0)))
    zgt = jnp.swapaxes(zg[:, :, :2 * SUBLANES], 1, 2)
    zgt = jnp.pad(zgt, ((0, 0), (0, 0), (0, nchunks * c - t))).reshape(b, 2 * SUBLANES, nchunks, c)
    zgt = jnp.swapaxes(zgt, 1, 2)
    pad_l = jnp.zeros((H_C_V,), F32)
    a_lane = jnp.pad(jnp.concatenate([pad_l, a_log]), (0, LANES - 2 * H_C_V)).reshape(1, LANES)
    d_lane = jnp.pad(jnp.concatenate([pad_l, dt_bias]), (0, LANES - 2 * H_C_V)).reshape(1, LANES)
    a_col = jnp.concatenate([pad_l, a_log]).reshape(2 * H_C_V, 1)
    d_col = jnp.concatenate([pad_l, dt_bias]).reshape(2 * H_C_V, 1)
    blk = (1, t, HD_C)
    wblk = (C_CONV, HD_C)
    bblk = (1, SUBLANES, HD_C)
    qmap = lambda bi, h: (bi, 0, h // rep)
    kmap = lambda bi, h: (bi, 0, nq + h // rep)
    vmap = lambda bi, h: (bi, 0, 2 * nq + h)
    zmap = lambda bi, h: (bi, 0, 2 * nq + H_C_V + h)
    full = lambda bi, h: (0, 0)
    return pl.pallas_call(
        functools.partial(_delta_kernel, c=c, tc=tc, nchunks=nchunks),
        grid=(b, H_C_V),
        in_specs=[pl.BlockSpec(blk, qmap), pl.BlockSpec(blk, kmap), pl.BlockSpec(blk, vmap), pl.BlockSpec(blk, zmap),
                  pl.BlockSpec((1, t, W_CG), lambda bi, h: (bi, 0, 0)),
                  pl.BlockSpec((1, nchunks, 2 * SUBLANES, c), lambda bi, h: (bi, 0, 0, 0)),
                  pl.BlockSpec(wblk, lambda bi, h: (0, h // rep)),
                  pl.BlockSpec(wblk, lambda bi, h: (0, nq + h // rep)),
                  pl.BlockSpec(wblk, lambda bi, h: (0, 2 * nq + h)),
                  pl.BlockSpec(bblk, qmap), pl.BlockSpec(bblk, kmap), pl.BlockSpec(bblk, vmap),
                  pl.BlockSpec((1, LANES), full), pl.BlockSpec((1, LANES), full),
                  pl.BlockSpec((2 * H_C_V, 1), full), pl.BlockSpec((2 * H_C_V, 1), full),
                  pl.BlockSpec((1, HD_C), full),
                  pl.BlockSpec((1, 1, HD_C, HD_C), lambda bi, h: (bi, h, 0, 0))],
        out_specs=[pl.BlockSpec(blk, lambda bi, h: (bi, 0, h)),
                   pl.BlockSpec((1, 1, HD_C, HD_C), lambda bi, h: (bi, h, 0, 0))],
        out_shape=[jax.ShapeDtypeStruct((b, t, W_CV), BF16), jax.ShapeDtypeStruct((b, H_C_V, HD_C, HD_C), F32)],
        scratch_shapes=[pltpu.VMEM((HD_C, HD_C), F32)],
        compiler_params=_cparams("parallel", "arbitrary"),
        name="gated_deltanet",
    )(zc, zc, zc, zc, zg, zgt, conv_w, conv_w, conv_w, buf8, buf8, buf8,
      a_lane, d_lane, a_col, d_col, nrm.reshape(1, HD_C), s0)


def _ret_kernel(q_ref, k_ref, v_ref, g_ref, nrm_ref, s0_ref, o_ref, sfin_ref, st_ref, *, c, tc, nchunks):
    h = pl.program_id(1)
    hf = jnp.full((1, 1), h, jnp.int32).astype(F32)
    lg = jnp.log1p(-jnp.exp2(-5.0 - hf))
    row = lax.broadcasted_iota(jnp.int32, (c, c), 0)
    col = lax.broadcasted_iota(jnp.int32, (c, c), 1)
    steps = (jnp.minimum(row + 1, tc) - jnp.minimum(col + 1, tc)).astype(F32)
    dec = jnp.exp(jnp.where(row >= col, steps * lg, NEG))
    r1 = lax.broadcasted_iota(jnp.int32, (c, 1), 0)
    b1 = jnp.minimum(r1 + 1, tc).astype(F32) * lg
    eb = jnp.exp(b1)
    bend = float(tc) * lg
    kdec = jnp.exp(bend - b1)
    valid = r1 < tc
    nrm = nrm_ref[...]
    st_ref[...] = s0_ref[0, 0].T

    def chunk(t0):
        rows = pl.ds(t0, tc)
        q = _pad_rows(q_ref[0, rows, :], c)
        k = _pad_rows(k_ref[0, rows, :], c) * HD_D ** -0.5
        if tc < c:
            k = jnp.where(valid, k, 0.0)
        v = _pad_rows(v_ref[0, rows, :], c)
        st = st_ref[...]
        att = _dot3(q, k, _NT) * dec
        o = eb * _dot3(q, st, _NT) + _dot3(att, v)
        st_ref[...] = jnp.exp(bend) * st + _dot3(v, k * kdec, _TN)
        mu = jnp.mean(o, axis=-1, keepdims=True)
        oc = o - mu
        var = jnp.mean(oc * oc, axis=-1, keepdims=True)
        on = oc * lax.rsqrt(var + EPS) * nrm
        out = on * _silu(_pad_rows(g_ref[0, rows, :], c))
        o_ref[0, rows, :] = out[0:tc].astype(o_ref.dtype)

    if nchunks == 1:
        chunk(0)
    else:
        def body(ci, carry):
            chunk(pl.multiple_of(ci * c, c))
            return carry
        lax.fori_loop(0, nchunks, body, 0)
    sfin_ref[0, 0] = st_ref[...].T


def _retention(zd, nrm, s0, chunk):
    b, t, _ = zd.shape
    c, tc, nchunks = _chunk_plan(t, chunk)
    blk = (1, t, HD_D)
    return pl.pallas_call(
        functools.partial(_ret_kernel, c=c, tc=tc, nchunks=nchunks),
        grid=(b, H_D),
        in_specs=[pl.BlockSpec(blk, lambda bi, h: (bi, 0, h)),
                  pl.BlockSpec(blk, lambda bi, h: (bi, 0, H_D + h)),
                  pl.BlockSpec(blk, lambda bi, h: (bi, 0, 2 * H_D + h)),
                  pl.BlockSpec(blk, lambda bi, h: (bi, 0, 3 * H_D + h)),
                  pl.BlockSpec((1, HD_D), lambda bi, h: (0, 0)),
                  pl.BlockSpec((1, 1, HD_D, HD_D), lambda bi, h: (bi, h, 0, 0))],
        out_specs=[pl.BlockSpec(blk, lambda bi, h: (bi, 0, h)),
                   pl.BlockSpec((1, 1, HD_D, HD_D), lambda bi, h: (bi, h, 0, 0))],
        out_shape=[jax.ShapeDtypeStruct((b, t, W_D), BF16), jax.ShapeDtypeStruct((b, H_D, HD_D, HD_D), F32)],
        scratch_shapes=[pltpu.VMEM((HD_D, HD_D), F32)],
        compiler_params=_cparams("parallel", "arbitrary"),
        name="retention",
    )(zd, zd, zd, zd, nrm.reshape(1, HD_D), s0)


GLA_CHUNK = 64
DELTA_CHUNK = 64
RET_CHUNK = 256


def _block(x, pe, kv_buf, s_hgrn, s_delta, buf_delta, s_ret, buf_ffn, lw, layer):
    bsz, t, d = x.shape
    m = bsz * t
    x2 = x.reshape(m, d)
    hn = _rmsnorm(x2, lw['attn_norm'], BF16)
    za = _matmul(hn, lw['w_in_a']).reshape(bsz, t, -1)
    zb = _matmul(hn, lw['w_in_b']).reshape(bsz, t, -1)
    zc = _matmul(hn, lw['w_in_c']).reshape(bsz, t, -1)
    zg = _matmul(hn, lw['w_in_g']).reshape(bsz, t, -1)
    zd = _matmul(hn, lw['w_in_d']).reshape(bsz, t, -1)

    kv_new = za[:, :, W_A:].reshape(bsz, t, 2, H_A, HD_A)
    if kv_buf is None:
        o_a = _attention_prompt(za)
        kv_state = kv_new[:, t - min(A_BRANCHES[-1][0], t):]
    else:
        n_past = kv_buf.shape[1]
        o_a = _attention_sample(za, kv_buf.reshape(bsz, n_past, 2 * W_A))
        kv_state = jnp.concatenate([kv_buf, kv_new], axis=1)[:, t:]

    o_b, s_hgrn_new = _hgrn(zb, lw['hgrn_lb'], lw['hgrn_norm'], s_hgrn, layer, GLA_CHUNK)
    o_c, s_delta_new = _delta(zc, zg, lw['delta_conv'], buf_delta, lw['delta_A_log'], lw['delta_dt_bias'],
                              lw['delta_norm'], s_delta, DELTA_CHUNK)
    pre = jnp.concatenate([buf_delta, zc[:, :, :2 * W_CQK + W_CV]], axis=1)
    buf_delta_new = pre[:, pre.shape[1] - (C_CONV - 1):]
    o_d, s_ret_new = _retention(zd, lw['ret_norm'], s_ret, RET_CHUNK)

    parts = [o.reshape(m, -1) for o in (o_a, o_b, o_c, o_d)]
    x2 = _matmul_residual(parts, lw['w_out'], x2, (1024, 512, 256, 64), (512, 256, 128))

    hf = _rmsnorm(x2, lw['ffn_norm'], BF16)
    if kv_buf is None:
        act, buf_ffn_new = _ffn_gate_up_prompt(hf, lw['w_gate'], lw['w_up'], lw['ffn_conv'], t)
    else:
        act, buf_ffn_new = _ffn_gate_up_sample(hf, lw['w_gate'], lw['w_up'], lw['ffn_conv'], buf_ffn, t)
    x2 = _matmul_residual([act], lw['w_down'], x2, (512, 256, 64), (256, 128))

    hp = _rmsnorm(x2, lw['ple_norm'], BF16)
    x2 = _ple(hp, lw['ple_gate'], pe.reshape(m, -1).astype(BF16), lw['ple_proj'], x2)
    states = (kv_state, s_hgrn_new, s_delta_new, buf_delta_new, s_ret_new, buf_ffn_new)
    return x2.reshape(bsz, t, d), states


def kernel(x_prompt, x_sample, cache_attn_kv, state_hgrn, state_delta, state_delta_conv, state_ret,
           state_ffn_conv, p_prompt, p_sample, attn_norm, w_in, hgrn_lb, hgrn_norm, delta_conv,
           delta_A_log, delta_dt_bias, delta_norm, ret_norm, w_out, ffn_norm, w_gate, w_up, ffn_conv,
           w_down, ple_norm, ple_gate, ple_proj, final_norm):
    depth = w_in.shape[0]
    bp = x_prompt.shape[0]
    o_b = 3 * W_A
    o_c = o_b + 4 * W_B
    o_g = o_c + 2 * W_CQK + 2 * W_CV
    o_d = o_g + 2 * H_C_V
    xp, xs = x_prompt, x_sample
    st_p, st_s = [], []
    for l in range(depth):
        wl = w_in[l]
        lw = {'attn_norm': attn_norm[l],
              'w_in_a': wl[:, :o_b].astype(BF16), 'w_in_b': wl[:, o_b:o_c].astype(BF16),
              'w_in_c': wl[:, o_c:o_g].astype(BF16),
              'w_in_g': jnp.pad(wl[:, o_g:o_d], ((0, 0), (0, W_CG - 2 * H_C_V))).astype(BF16),
              'w_in_d': wl[:, o_d:].astype(BF16),
              'hgrn_lb': hgrn_lb, 'hgrn_norm': hgrn_norm[l],
              'delta_conv': delta_conv[l], 'delta_A_log': delta_A_log[l], 'delta_dt_bias': delta_dt_bias[l],
              'delta_norm': delta_norm[l], 'ret_norm': ret_norm[l], 'w_out': w_out[l].astype(BF16),
              'ffn_norm': ffn_norm[l], 'w_gate': w_gate[l].astype(BF16), 'w_up': w_up[l].astype(BF16),
              'ffn_conv': ffn_conv[l], 'w_down': w_down[l].astype(BF16), 'ple_norm': ple_norm[l],
              'ple_gate': ple_gate[l].astype(BF16), 'ple_proj': ple_proj[l].astype(BF16)}
        zero = lambda *s: jnp.zeros((bp,) + s, F32)
        xp, sp = _block(xp, p_prompt[l], None, zero(H_B, HD_B, HD_B), zero(H_C_V, HD_C, HD_C),
                        zero(C_CONV - 1, 2 * W_CQK + W_CV), zero(H_D, HD_D, HD_D), None, lw, l)
        xs, ss = _block(xs, p_sample[l], cache_attn_kv[l], state_hgrn[l], state_delta[l], state_delta_conv[l],
                        state_ret[l], state_ffn_conv[l], lw, l)
        st_p.append(sp)
        st_s.append(ss)

    def stack(sts, i):
        return jnp.stack([s[i] for s in sts])

    def final(x):
        return _rmsnorm(x.reshape(-1, x.shape[-1]), final_norm, F32).reshape(x.shape)

    return (final(xp), final(xs),
            stack(st_p, 0), stack(st_s, 0), stack(st_p, 1), stack(st_s, 1), stack(st_p, 2), stack(st_s, 2),
            stack(st_p, 3), stack(st_s, 3), stack(st_p, 4), stack(st_s, 4), stack(st_p, 5), stack(st_s, 5))
```

```python
import functools
import math

import jax
import jax.numpy as jnp
from jax import lax
from jax.experimental import pallas as pl
from jax.experimental.pallas import tpu as pltpu

F32 = jnp.float32
BF16 = jnp.bfloat16
EPS = 1e-6
NEG = -1e30

H_A = 8
HD_A = 128
A_BRANCHES = ((128, 1), (512, 4), (2048, 16))
A_BLOCK = 128
H_B = 8
HD_B = 128
H_C_QK = 4
H_C_V = 8
HD_C = 128
C_CONV = 4
H_D = 4
HD_D = 256
FFN_CONV = 3

W_A = H_A * HD_A
W_B = H_B * HD_B
W_CQK = H_C_QK * HD_C
W_CV = H_C_V * HD_C
W_D = H_D * HD_D
W_CG = 128

VMEM_LIMIT_BYTES = 52 * 1024 * 1024
SUBLANES = 8
LANES = 128

GLA_HEADS_PER_STEP = 4
DELTA_QK_HEADS_PER_STEP = 2
RET_HEADS_PER_STEP = 2
GLA_CHUNK = 64
DELTA_CHUNK = 64
RET_CHUNK = 256
DELTA_SUB = 16

_NN = (((1,), (0,)), ((), ()))
_NT = (((1,), (1,)), ((), ()))
_TN = (((0,), (0,)), ((), ()))


def _cparams(*sem):
    return pltpu.CompilerParams(dimension_semantics=sem, vmem_limit_bytes=VMEM_LIMIT_BYTES)


def _dg(a, b, dn=_NN):
    return lax.dot_general(a, b, dn, preferred_element_type=F32)


def _dot1(a, b, dn=_NN):
    return _dg(a.astype(BF16), b.astype(BF16), dn)


def _split3(x):
    x1 = x.astype(BF16)
    r1 = x - x1.astype(F32)
    x2 = r1.astype(BF16)
    x3 = (r1 - x2.astype(F32)).astype(BF16)
    return x1, x2, x3


def _sel_dot(m, x):
    x1, x2, x3 = _split3(x)
    return _dg(m, x1) + (_dg(m, x2) + _dg(m, x3))


def _sigmoid(x):
    return 1.0 / (1.0 + jnp.exp(-x))


def _silu(x):
    return x * _sigmoid(x)


def _softplus(x):
    return jnp.maximum(x, 0.0) + jnp.log1p(jnp.exp(-jnp.abs(x)))


def _pad_rows(x, rows):
    if x.shape[0] == rows:
        return x
    return jnp.concatenate([x, jnp.zeros((rows - x.shape[0],) + x.shape[1:], x.dtype)], axis=0)


def _onehot(cond):
    return jnp.where(cond, 1.0, 0.0).astype(BF16)


def _pick(n, prefs):
    for p in prefs:
        if n % p == 0:
            return p
    return n


def _rmsnorm_kernel(x_ref, g_ref, o_ref):
    x = x_ref[...]
    y = x * lax.rsqrt(jnp.mean(x * x, axis=-1, keepdims=True) + EPS)
    o_ref[...] = (y * g_ref[...]).astype(o_ref.dtype)


def _rmsnorm(x, g, out_dtype):
    m, d = x.shape
    tm = _pick(m, (256, 64, 8))
    return pl.pallas_call(
        _rmsnorm_kernel,
        grid=(m // tm,),
        in_specs=[pl.BlockSpec((tm, d), lambda i: (i, 0)), pl.BlockSpec((1, d), lambda i: (0, 0))],
        out_specs=pl.BlockSpec((tm, d), lambda i: (i, 0)),
        out_shape=jax.ShapeDtypeStruct((m, d), out_dtype),
        compiler_params=_cparams("parallel"),
        name="rmsnorm",
    )(x, g.reshape(1, d))


def _mm_kernel(a_ref, w_ref, o_ref):
    o_ref[...] = _dg(a_ref[...], w_ref[...]).astype(o_ref.dtype)


def _matmul(a, w, layer, out_dtype=F32):
    m, k = a.shape
    n = w.shape[2]
    tm = _pick(m, (1024, 512, 256, 64))
    tn = _pick(n, (512, 256, 128))
    return pl.pallas_call(
        _mm_kernel,
        grid=(m // tm, n // tn),
        in_specs=[pl.BlockSpec((tm, k), lambda i, j: (i, 0)),
                  pl.BlockSpec((None, k, tn), lambda i, j: (layer, 0, j))],
        out_specs=pl.BlockSpec((tm, tn), lambda i, j: (i, j)),
        out_shape=jax.ShapeDtypeStruct((m, n), out_dtype),
        compiler_params=_cparams("parallel", "arbitrary"),
        name="matmul",
    )(a, w)


def _mm_res_kernel(*refs, nparts):
    a_refs = refs[:nparts]
    w_refs = refs[nparts:2 * nparts]
    x_ref = refs[2 * nparts]
    o_ref = refs[2 * nparts + 1]
    acc = x_ref[...]
    for a_ref, w_ref in zip(a_refs, w_refs):
        acc = acc + _dg(a_ref[...], w_ref[...])
    o_ref[...] = acc


def _matmul_residual(parts, w, layer, x, tm_prefs, tn_prefs):
    m, n = x.shape
    kp = parts[0].shape[1]
    nparts = len(parts)
    tm = _pick(m, tm_prefs)
    tn = _pick(n, tn_prefs)

    def wmap(p):
        return lambda i, j: (layer, p, j)

    in_specs = [pl.BlockSpec((tm, kp), lambda i, j: (i, 0)) for _ in parts]
    in_specs += [pl.BlockSpec((None, kp, tn), wmap(p)) for p in range(nparts)]
    in_specs += [pl.BlockSpec((tm, tn), lambda i, j: (i, j))]
    return pl.pallas_call(
        functools.partial(_mm_res_kernel, nparts=nparts),
        grid=(m // tm, n // tn),
        in_specs=in_specs,
        out_specs=pl.BlockSpec((tm, tn), lambda i, j: (i, j)),
        out_shape=jax.ShapeDtypeStruct((m, n), F32),
        compiler_params=_cparams("parallel", "arbitrary"),
        name="matmul_residual",
    )(*parts, *([w] * nparts), x)


def _ple_kernel(a_ref, wg_ref, pe_ref, wp_ref, x_ref, o_ref):
    gate = _sigmoid(_dg(a_ref[...], wg_ref[...]))
    o_ref[...] = x_ref[...] + gate * _dg(pe_ref[...], wp_ref[...])


def _ple(hp, wg, pe, wp, layer, x):
    m, d = x.shape
    k = hp.shape[1]
    kp = pe.shape[1]
    tm = _pick(m, (1024, 512, 256, 64))
    tn = _pick(d, (512, 256, 128))
    return pl.pallas_call(
        _ple_kernel,
        grid=(m // tm, d // tn),
        in_specs=[pl.BlockSpec((tm, k), lambda i, j: (i, 0)),
                  pl.BlockSpec((None, k, tn), lambda i, j: (layer, 0, j)),
                  pl.BlockSpec((tm, kp), lambda i, j: (i, 0)),
                  pl.BlockSpec((None, kp, tn), lambda i, j: (layer, 0, j)),
                  pl.BlockSpec((tm, tn), lambda i, j: (i, j))],
        out_specs=pl.BlockSpec((tm, tn), lambda i, j: (i, j)),
        out_shape=jax.ShapeDtypeStruct((m, d), F32),
        compiler_params=_cparams("parallel", "arbitrary"),
        name="ple",
    )(hp, wg, pe, wp, x)


def _ffn_act(g, p1, p2, cw, u):
    gc = cw[0:1] * p2 + cw[1:2] * p1 + cw[2:3] * g
    return (_silu(gc) * u).astype(BF16)


def _ffn_gu_prompt_kernel(h_ref, halo_ref, wg_ref, wu_ref, cw_ref, o_ref, tail_ref, *, seq):
    tm = h_ref.shape[0]
    a = h_ref[...]
    g = _dg(a, wg_ref[...])
    u = _dg(a, wu_ref[...])
    gh = _dg(halo_ref[...], wg_ref[...])
    seq_start = lax.rem(pl.program_id(0) * tm, seq) == 0
    gh = jnp.where(seq_start, 0.0, gh)
    row = lax.broadcasted_iota(jnp.int32, g.shape, 0)
    p1 = jnp.where(row == 0, gh[7:8], pltpu.roll(g, 1, 0))
    p2 = jnp.where(row == 0, gh[6:7], jnp.where(row == 1, gh[7:8], pltpu.roll(g, 2, 0)))
    o_ref[...] = _ffn_act(g, p1, p2, cw_ref[...], u)
    tail_ref[0] = g[tm - SUBLANES:tm]


def _ffn_gate_up_prompt(hf, wg, wu, cw, layer, seq):
    m, d = hf.shape
    f = wg.shape[2]
    tm = _pick(seq, (1024, 512, 256, 128, 64, 8))
    tf = _pick(f, (256, 128))
    hb = tm // SUBLANES
    wmap = lambda i, j: (layer, 0, j)
    out, tail = pl.pallas_call(
        functools.partial(_ffn_gu_prompt_kernel, seq=seq),
        grid=(m // tm, f // tf),
        in_specs=[pl.BlockSpec((tm, d), lambda i, j: (i, 0)),
                  pl.BlockSpec((SUBLANES, d), lambda i, j: (jnp.maximum(i * hb - 1, 0), 0)),
                  pl.BlockSpec((None, d, tf), wmap),
                  pl.BlockSpec((None, d, tf), wmap),
                  pl.BlockSpec((None, FFN_CONV, tf), wmap)],
        out_specs=[pl.BlockSpec((tm, tf), lambda i, j: (i, j)),
                   pl.BlockSpec((1, SUBLANES, tf), lambda i, j: (i, 0, j))],
        out_shape=[jax.ShapeDtypeStruct((m, f), BF16),
                   jax.ShapeDtypeStruct((m // tm, SUBLANES, f), F32)],
        compiler_params=_cparams("parallel", "arbitrary"),
        name="ffn_gate_up_prompt",
    )(hf, hf, wg, wu, cw)
    per_seq = seq // tm
    tail = tail.reshape(m // seq, per_seq, SUBLANES, f)[:, per_seq - 1, SUBLANES - (FFN_CONV - 1):]
    return out, tail


def _ffn_gu_sample_kernel(h_ref, wg_ref, wu_ref, cw_ref, b1_ref, b2_ref, o_ref, g_ref, *, t):
    a = h_ref[...]
    g = _dg(a, wg_ref[...])
    u = _dg(a, wu_ref[...])
    pos = lax.rem(lax.broadcasted_iota(jnp.int32, g.shape, 0), t)
    p1 = jnp.where(pos == 0, b1_ref[...], pltpu.roll(g, 1, 0))
    p2 = jnp.where(pos < 2, b2_ref[...], pltpu.roll(g, 2, 0))
    o_ref[...] = _ffn_act(g, p1, p2, cw_ref[...], u)
    g_ref[...] = g


def _ffn_gate_up_sample(hf, wg, wu, cw, layer, buf, t):
    m, d = hf.shape
    f = wg.shape[2]
    nb = m // t
    tf = _pick(f, (256, 128))
    zeros = jnp.zeros((nb, t - 1, f), F32)
    b1 = jnp.concatenate([buf[:, 1:2], zeros], axis=1).reshape(m, f)
    b2 = jnp.concatenate([buf[:, 0:2], zeros[:, 1:]], axis=1).reshape(m, f)
    wmap = lambda j: (layer, 0, j)
    out, g = pl.pallas_call(
        functools.partial(_ffn_gu_sample_kernel, t=t),
        grid=(f // tf,),
        in_specs=[pl.BlockSpec((m, d), lambda j: (0, 0)),
                  pl.BlockSpec((None, d, tf), wmap),
                  pl.BlockSpec((None, d, tf), wmap),
                  pl.BlockSpec((None, FFN_CONV, tf), wmap),
                  pl.BlockSpec((m, tf), lambda j: (0, j)),
                  pl.BlockSpec((m, tf), lambda j: (0, j))],
        out_specs=[pl.BlockSpec((m, tf), lambda j: (0, j)), pl.BlockSpec((m, tf), lambda j: (0, j))],
        out_shape=[jax.ShapeDtypeStruct((m, f), BF16), jax.ShapeDtypeStruct((m, f), F32)],
        compiler_params=_cparams("arbitrary"),
        name="ffn_gate_up_sample",
    )(hf, wg, wu, cw, b1, b2)
    tail = g.reshape(nb, t, f)[:, t - (FFN_CONV - 1):]
    return out, tail


ATTN_HEADS_PER_STEP = 2


def _band_softmax_many(probs, dist_cf, dist_pf, valid_c):
    scs = [_dot1(pr[0], pr[1], _NT) for pr in probs]
    sps = [None if pr[2] is None else _dot1(pr[0], pr[2], _NT) for pr in probs]
    pcs, pps, ls, lses = [], [], [], []
    for pr, sc, sp in zip(probs, scs, sps):
        slope, prev_ok = pr[5], pr[6]
        sc = jnp.where(valid_c, sc - slope * dist_cf, NEG)
        if sp is None:
            m = jnp.max(sc, axis=-1, keepdims=True)
            pc = jnp.exp(sc - m)
            pp = None
            l = jnp.sum(pc, axis=-1, keepdims=True)
        else:
            sp = jnp.where(prev_ok, sp - slope * dist_pf, NEG)
            m = jnp.max(jnp.maximum(sc, sp), axis=-1, keepdims=True)
            pc = jnp.exp(sc - m)
            pp = jnp.exp(sp - m)
            l = jnp.sum(pc + pp, axis=-1, keepdims=True)
        pcs.append(pc)
        pps.append(pp)
        ls.append(l)
        lses.append(m + jnp.log(l))
    ocs = [_dot1(pc, pr[3]) for pc, pr in zip(pcs, probs)]
    ops = [None if pp is None else _dot1(pp, pr[4]) for pp, pr in zip(pps, probs)]
    outs = [(oc if op is None else oc + op) / l for oc, op, l in zip(ocs, ops, ls)]
    return outs, lses


def _band_kernel(*refs, s):
    nh = ATTN_HEADS_PER_STEP
    q_refs, k_refs, v_refs = refs[0:nh], refs[nh:2 * nh], refs[2 * nh:3 * nh]
    o_ref, o2_ref, o3_ref, l2_ref, l3_ref = refs[3 * nh:]
    nq = A_BLOCK
    hg = pl.program_id(1)
    qi = lax.broadcasted_iota(jnp.int32, (nq, nq), 0)
    kj = lax.broadcasted_iota(jnp.int32, (nq, nq), 1)
    dist_c = qi - kj
    dist_p = dist_c + nq
    valid_c = dist_c >= 0
    dist_cf = dist_c.astype(F32)
    dist_pf = dist_p.astype(F32)
    (w1, d1), (w2, d2), (w3, d3) = A_BRANCHES
    assert d1 == 1 and s % (d2 * nq) == 0 and s == d3 * nq and max(w1 // d1, w2 // d2, w3 // d3) <= nq

    def problem(h, rows_c, rows_p, has_prev, band, dil):
        hv = jnp.full((1, 1), hg * nh + h + 1, jnp.int32).astype(F32)
        slope = float(dil) * jnp.exp2(-8.0 * hv / H_A)
        q = q_refs[h][0, rows_c, :] * HD_A ** -0.5
        if rows_p is None:
            return (q, k_refs[h][0, rows_c, :], None, v_refs[h][0, rows_c, :], None, slope, None)
        return (q, k_refs[h][0, rows_c, :], k_refs[h][0, rows_p, :], v_refs[h][0, rows_c, :],
                v_refs[h][0, rows_p, :], slope, (dist_p <= band) & has_prev)

    def dilated(i, carry):
        rows3 = pl.ds(i, nq, stride=d3)
        r = lax.rem(i, d2)
        b = i // d2
        rows2 = pl.ds(b * (d2 * nq) + r, nq, stride=d2)
        rows2p = pl.ds(jnp.maximum(b - 1, 0) * (d2 * nq) + r, nq, stride=d2)
        probs = [problem(h, rows3, None, False, w3 // d3, d3) for h in range(nh)]
        probs += [problem(h, rows2, rows2p, b > 0, w2 // d2, d2) for h in range(nh)]
        outs, lses = _band_softmax_many(probs, dist_cf, dist_pf, valid_c)
        for h in range(nh):
            o3_ref[h, rows3, :] = outs[h]
            l3_ref[h, rows3, :] = jnp.broadcast_to(lses[h], (nq, HD_A))
            o2_ref[h, rows2, :] = outs[nh + h]
            l2_ref[h, rows2, :] = jnp.broadcast_to(lses[nh + h], (nq, HD_A))
        return carry

    lax.fori_loop(0, s // nq, dilated, 0)

    def dense(i, carry):
        blocks = []
        for j in range(2):
            bi = 2 * i + j
            rows = pl.ds(pl.multiple_of(bi * nq, nq), nq)
            rows_p = pl.ds(pl.multiple_of(jnp.maximum(bi - 1, 0) * nq, nq), nq)
            blocks += [(h, rows, problem(h, rows, rows_p, bi > 0, w1 // d1, d1)) for h in range(nh)]
        outs, lses = _band_softmax_many([blk[2] for blk in blocks], dist_cf, dist_pf, valid_c)
        for (h, rows, _), o1, l1 in zip(blocks, outs, lses):
            l2 = l2_ref[h, rows, :]
            l3 = l3_ref[h, rows, :]
            mx = jnp.maximum(l1, jnp.maximum(l2, l3))
            e1 = jnp.exp(l1 - mx)
            e2 = jnp.exp(l2 - mx)
            e3 = jnp.exp(l3 - mx)
            o = (e1 * o1 + e2 * o2_ref[h, rows, :] + e3 * o3_ref[h, rows, :]) / (e1 + e2 + e3)
            o_ref[0, rows, h * HD_A:(h + 1) * HD_A] = o.astype(o_ref.dtype)
        return carry

    assert (s // nq) % 2 == 0
    lax.fori_loop(0, s // (2 * nq), dense, 0)


def _attention_prompt(za):
    b, s, _ = za.shape
    nh = ATTN_HEADS_PER_STEP
    ng = H_A // nh
    blk = (1, s, HD_A)

    def head_spec(first, h):
        return pl.BlockSpec(blk, lambda bi, g: (bi, 0, first + g * nh + h))

    in_specs = [head_spec(part * H_A, h) for part in range(3) for h in range(nh)]
    return pl.pallas_call(
        functools.partial(_band_kernel, s=s),
        grid=(b, ng),
        in_specs=in_specs,
        out_specs=pl.BlockSpec((1, s, nh * HD_A), lambda bi, g: (bi, 0, g)),
        out_shape=jax.ShapeDtypeStruct((b, s, W_A), BF16),
        scratch_shapes=[pltpu.VMEM((nh, s, HD_A), F32) for _ in range(4)],
        compiler_params=_cparams("parallel", "arbitrary"),
        name="band_attention",
    )(*([za] * (3 * nh)))


def _attn_sample_kernel(q_ref, kn_ref, vn_ref, kc_ref, vc_ref, o_ref):
    t = q_ref.shape[1]
    n_past = kc_ref.shape[0]
    tp = 2 * SUBLANES
    shape_c = (tp, n_past)
    shape_n = (tp, tp)
    dist_c = n_past + lax.broadcasted_iota(jnp.int32, shape_c, 0) - lax.broadcasted_iota(jnp.int32, shape_c, 1)
    dist_n = lax.broadcasted_iota(jnp.int32, shape_n, 0) - lax.broadcasted_iota(jnp.int32, shape_n, 1)
    new_ok = lax.broadcasted_iota(jnp.int32, shape_n, 1) < t
    dist_cf = dist_c.astype(F32)
    dist_nf = dist_n.astype(F32)
    oks = []
    for window, dil in A_BRANCHES:
        oks.append(((dist_c <= window) & (lax.rem(dist_c, dil) == 0),
                    (dist_n >= 0) & (dist_n <= window) & (lax.rem(dist_n, dil) == 0) & new_ok))
    for h in range(H_A):
        slope = 2.0 ** (-8.0 * (h + 1) / H_A)
        sl = slice(h * HD_A, (h + 1) * HD_A)
        q = _pad_rows(q_ref[0, :, sl] * HD_A ** -0.5, tp)
        kn = _pad_rows(kn_ref[0, :, sl], tp)
        vn = _pad_rows(vn_ref[0, :, sl], tp).astype(BF16)
        vc = vc_ref[:, h, :].astype(BF16)
        b_c = _dot1(q, kc_ref[:, h, :], _NT) - slope * dist_cf
        b_n = _dot1(q, kn, _NT) - slope * dist_nf
        outs, lses = [], []
        for ok_c, ok_n in oks:
            sc = jnp.where(ok_c, b_c, NEG)
            sn = jnp.where(ok_n, b_n, NEG)
            m = jnp.maximum(jnp.max(sc, axis=-1, keepdims=True), jnp.max(sn, axis=-1, keepdims=True))
            pc = jnp.exp(sc - m)
            pn = jnp.exp(sn - m)
            l = jnp.sum(pc, axis=-1, keepdims=True) + jnp.sum(pn, axis=-1, keepdims=True)
            outs.append((_dg(pc.astype(BF16), vc) + _dg(pn.astype(BF16), vn)) / l)
            lses.append(m + jnp.log(l))
        mx = functools.reduce(jnp.maximum, lses)
        es = [jnp.exp(x - mx) for x in lses]
        tot = functools.reduce(lambda a, b: a + b, es)
        o = functools.reduce(lambda a, b: a + b, [e * x for e, x in zip(es, outs)]) / tot
        o_ref[0, :, sl] = o[:t].astype(o_ref.dtype)


def _attention_sample(za, cache, layer):
    b, t, _ = za.shape
    l = cache.shape[2]
    cblk = (None, None, l, None, H_A, HD_A)
    return pl.pallas_call(
        _attn_sample_kernel,
        grid=(b,),
        in_specs=[pl.BlockSpec((1, t, W_A), lambda bi: (bi, 0, 0)),
                  pl.BlockSpec((1, t, W_A), lambda bi: (bi, 0, 1)),
                  pl.BlockSpec((1, t, W_A), lambda bi: (bi, 0, 2)),
                  pl.BlockSpec(cblk, lambda bi: (layer, bi, 0, 0, 0, 0)),
                  pl.BlockSpec(cblk, lambda bi: (layer, bi, 0, 1, 0, 0))],
        out_specs=pl.BlockSpec((1, t, W_A), lambda bi: (bi, 0, 0)),
        out_shape=jax.ShapeDtypeStruct((b, t, W_A), BF16),
        compiler_params=_cparams("parallel"),
        name="attention_sample",
    )(za, za, za, cache, cache)


def _chunk_plan(t, chunk):
    c = chunk if t >= chunk else max(2 * SUBLANES, t)
    assert t % c == 0 or t < c
    tc = min(t, c)
    return c, tc, max(t // c, 1)


def _run_chunks(nchunks, c, chunk):
    if nchunks == 1:
        chunk(0, 0)
    else:
        def body(ci, carry):
            chunk(ci, pl.multiple_of(ci * c, c))
            return carry
        lax.fori_loop(0, nchunks, body, 0)


def _gla_kernel(q_ref, f_ref, i_ref, g_ref, lb_ref, nrm_ref, s0_ref, o_ref, sfin_ref, st_ref, *, layer, c, tc, nchunks):
    nlev = int(math.log2(c))
    assert 1 << nlev == c
    depth = lb_ref.shape[0]
    nh = st_ref.shape[0]
    lbs = [lb_ref[i] for i in range(depth)]
    mx = functools.reduce(jnp.maximum, lbs)
    es = [jnp.exp(x - mx) for x in lbs]
    tot = functools.reduce(lambda a, b: a + b, es)
    lower = functools.reduce(lambda a, b: a + b, [es[i] / tot for i in range(layer + 1)]) - es[0] / tot
    one_minus_lb = 1.0 - lower

    row = lax.broadcasted_iota(jnp.int32, (c, c), 0)
    col = lax.broadcasted_iota(jnp.int32, (c, c), 1)
    prefix = [_onehot(col <= row)]
    lmask = []
    for lev in range(1, nlev + 1):
        s = c >> lev
        grp = ~(2 * s - 1)
        prefix.append(_onehot(col <= (row & grp) + (s - 1)))
        lmask.append((((row ^ col) & grp) == 0) & ((row & s) != 0) & ((col & s) == 0))
    mstack = jnp.concatenate(prefix, axis=0)
    eye = row == col
    valid = lax.broadcasted_iota(jnp.int32, (c, 1), 0) < tc
    nrm = nrm_ref[...]
    for h in range(nh):
        st_ref[h] = s0_ref[0, h].T

    def chunk(ci, t0):
        rows = pl.ds(t0, tc)
        hs = range(nh)
        sls = [slice(h * HD_B, (h + 1) * HD_B) for h in hs]
        qs = [_silu(_pad_rows(q_ref[0, rows, sl], c)) for sl in sls]
        kbs = [one_minus_lb[:, sl] * _sigmoid(-_pad_rows(f_ref[0, rows, sl], c)) for sl in sls]
        logfs = [jnp.log1p(-kb) for kb in kbs]
        if tc < c:
            kbs = [jnp.where(valid, kb, 0.0) for kb in kbs]
            logfs = [jnp.where(valid, lf, 0.0) for lf in logfs]
        vs = [_pad_rows(i_ref[0, rows, sl], c).astype(BF16) for sl in sls]
        bs_all = _sel_dot(mstack, jnp.concatenate(logfs, axis=1))
        bss = [bs_all[:, sl] for sl in sls]
        b = [bs[0:c] for bs in bss]
        atts = [jnp.where(eye, jnp.sum(q * kb, axis=-1, keepdims=True), 0.0) for q, kb in zip(qs, kbs)]
        for lev in range(1, nlev + 1):
            brs = [bs[lev * c:(lev + 1) * c] for bs in bss]
            qls = [qs[h] * jnp.exp(jnp.minimum(b[h] - brs[h], 0.0)) for h in hs]
            kls = [kbs[h] * jnp.exp(jnp.minimum(brs[h] - b[h], 0.0)) for h in hs]
            prods = [_dot1(ql, kl, _NT) for ql, kl in zip(qls, kls)]
            atts = [att + jnp.where(lmask[lev - 1], pr, 0.0) for att, pr in zip(atts, prods)]
        sts = [st_ref[h] for h in hs]
        o_inter = [_dot1(qs[h] * jnp.exp(b[h]), sts[h], _NT) for h in hs]
        o_intra = [_dg(atts[h].astype(BF16), vs[h]) for h in hs]
        bends = [b[h][c - 1:c] for h in hs]
        upd = [_dg(vs[h], (kbs[h] * jnp.exp(bends[h] - b[h])).astype(BF16), _TN) for h in hs]
        for h in hs:
            st_ref[h] = sts[h] * jnp.exp(bends[h]) + upd[h]
            o = o_inter[h] + o_intra[h]
            on = o * lax.rsqrt(jnp.mean(o * o, axis=-1, keepdims=True) + EPS) * nrm
            out = on * _silu(_pad_rows(g_ref[0, rows, sls[h]], c))
            o_ref[0, rows, sls[h]] = out[0:tc].astype(o_ref.dtype)

    _run_chunks(nchunks, c, chunk)
    for h in range(nh):
        sfin_ref[0, h] = st_ref[h].T


def _hgrn(zb, lb, nrm, s0, layer, chunk):
    b, t, _ = zb.shape
    c, tc, nchunks = _chunk_plan(t, chunk)
    depth = lb.shape[0]
    nh = GLA_HEADS_PER_STEP
    ng = H_B // nh
    blk = (1, t, nh * HD_B)
    sblk = (1, nh, HD_B, HD_B)
    return pl.pallas_call(
        functools.partial(_gla_kernel, layer=layer, c=c, tc=tc, nchunks=nchunks),
        grid=(b, ng),
        in_specs=[pl.BlockSpec(blk, lambda bi, g: (bi, 0, g)),
                  pl.BlockSpec(blk, lambda bi, g: (bi, 0, ng + g)),
                  pl.BlockSpec(blk, lambda bi, g: (bi, 0, 2 * ng + g)),
                  pl.BlockSpec(blk, lambda bi, g: (bi, 0, 3 * ng + g)),
                  pl.BlockSpec((depth, 1, nh * HD_B), lambda bi, g: (0, 0, g)),
                  pl.BlockSpec((1, HD_B), lambda bi, g: (0, 0)),
                  pl.BlockSpec(sblk, lambda bi, g: (bi, g, 0, 0))],
        out_specs=[pl.BlockSpec(blk, lambda bi, g: (bi, 0, g)),
                   pl.BlockSpec(sblk, lambda bi, g: (bi, g, 0, 0))],
        out_shape=[jax.ShapeDtypeStruct((b, t, W_B), BF16), jax.ShapeDtypeStruct((b, H_B, HD_B, HD_B), F32)],
        scratch_shapes=[pltpu.VMEM((nh, HD_B, HD_B), F32)],
        compiler_params=_cparams("parallel", "arbitrary"),
        name="hgrn2",
    )(zb, zb, zb, zb, lb.reshape(depth, 1, W_B), nrm.reshape(1, HD_B), s0)


def _delta_kernel(q_ref, k_ref, v_ref, z_ref, zg_ref, wq_ref, wk_ref, wv_ref, bq_ref, bk_ref, bv_ref,
                  al_ref, dl_ref, nrm_ref, s0_ref, o_ref, sfin_ref, st_ref, *, c, tc, nchunks):
    g = pl.program_id(1)
    nqk = DELTA_QK_HEADS_PER_STEP
    rep = H_C_V // H_C_QK
    sb = min(DELTA_SUB, c)
    row = lax.broadcasted_iota(jnp.int32, (c, c), 0)
    col = lax.broadcasted_iota(jnp.int32, (c, c), 1)
    lower_incl = _onehot(col <= row)
    eye = jnp.where(row == col, 1.0, 0.0)
    same_sub = ((row ^ col) & ~(sb - 1)) == 0
    valid = lax.broadcasted_iota(jnp.int32, (c, 1), 0) < tc
    lane = lax.broadcasted_iota(jnp.int32, (c, LANES), 1)
    nrm = nrm_ref[...]
    for h in range(nqk * rep):
        st_ref[h] = s0_ref[0, h].T

    def conv(x_ref, w_ref, buf_ref, sl, ci, t0):
        x = x_ref[0, pl.ds(t0, tc), sl]
        halo = buf_ref[0, :, sl]
        if nchunks > 1:
            prev = x_ref[0, pl.ds(pl.multiple_of(jnp.maximum(t0 - SUBLANES, 0), SUBLANES), SUBLANES), sl]
            halo = jnp.where(ci == 0, halo, prev)
        xh = jnp.concatenate([halo, x], axis=0)
        w = w_ref[:, sl]
        y = w[C_CONV - 1:C_CONV] * x
        for s in range(1, C_CONV):
            y = y + w[C_CONV - 1 - s:C_CONV - s] * pltpu.roll(xh, s, 0)[SUBLANES:]
        return _pad_rows(_silu(y), c)

    def l2n(x):
        return x * lax.rsqrt(jnp.sum(x * x, axis=-1, keepdims=True) + EPS)

    def chunk(ci, t0):
        rows = pl.ds(t0, tc)
        zg = _pad_rows(zg_ref[0, rows, :], c)
        beta_all = _sigmoid(zg)
        la_all = -jnp.exp(al_ref[...]) * _softplus(zg + dl_ref[...])
        nv = nqk * rep
        hs = range(nv)
        qsls = [slice(qh * HD_C, (qh + 1) * HD_C) for qh in range(nqk)]
        vsls = [slice(h * HD_C, (h + 1) * HD_C) for h in hs]
        qs = [l2n(conv(q_ref, wq_ref, bq_ref, sl, ci, t0)) * HD_C ** -0.5 for sl in qsls]
        ks = [l2n(conv(k_ref, wk_ref, bk_ref, sl, ci, t0)) for sl in qsls]
        if tc < c:
            ks = [jnp.where(valid, k, 0.0) for k in ks]
        qbs = [q.astype(BF16) for q in qs]
        kbs = [k.astype(BF16) for k in ks]
        kks = [_dg(kb, kb, _NT) for kb in kbs]
        qks = [_dg(qb, kb, _NT) for qb, kb in zip(qbs, kbs)]
        vs = [conv(v_ref, wv_ref, bv_ref, sl, ci, t0) for sl in vsls]
        betas = [jnp.sum(jnp.where(lane == g * nv + h, beta_all, 0.0), axis=-1, keepdims=True) for h in hs]
        las = [jnp.sum(jnp.where(lane == H_C_V + g * nv + h, la_all, 0.0), axis=-1, keepdims=True) for h in hs]
        if tc < c:
            betas = [jnp.where(valid, x, 0.0) for x in betas]
            las = [jnp.where(valid, x, 0.0) for x in las]
        bc_all = _sel_dot(lower_incl, jnp.concatenate([jnp.broadcast_to(la, (c, LANES)) for la in las], axis=1))
        bcols = [bc_all[:, h * LANES:h * LANES + c] for h in hs]
        decs = [jnp.exp(jnp.where(row >= col, bc - bc.T, NEG)) for bc in bcols]
        b1s = [bc[:, 0:1] for bc in bcols]
        ebs = [jnp.exp(b1) for b1 in b1s]
        bends = [b1[c - 1:c] for b1 in b1s]
        ns = [jnp.where(row > col, betas[h] * (kks[h // rep] * decs[h]), 0.0) for h in hs]
        nds = [jnp.where(same_sub, n, 0.0) for n in ns]
        xs = [eye - nd for nd in nds]
        ps = nds
        for _ in range(int(math.log2(sb)) - 1):
            ps = [_dot1(p, p) for p in ps]
            xs = [x + _dot1(x, p) for x, p in zip(xs, ps)]
        nblk = c // sb
        if nblk > 1:
            mms = [_dot1(x, n - nd) for x, n, nd in zip(xs, ns, nds)]
            ys = [eye - mm for mm in mms]
            ps = mms
            for _ in range(int(math.log2(nblk)) - 1):
                ps = [_dot1(p, p) for p in ps]
                ys = [y + _dot1(y, p) for y, p in zip(ys, ps)]
            xs = [_dot1(y, x) for y, x in zip(ys, xs)]
        sts = [st_ref[h] for h in hs]
        stbs = [st.astype(BF16) for st in sts]
        ksts = [_dg(kbs[h // rep], stbs[h], _NT) for h in hs]
        qsts = [_dg(qbs[h // rep], stbs[h], _NT) for h in hs]
        rhss = [betas[h] * (vs[h] - ebs[h] * ksts[h]) for h in hs]
        ubs = [_dot1(x, rhs).astype(BF16) for x, rhs in zip(xs, rhss)]
        o_intra = [_dg((qks[h // rep] * decs[h]).astype(BF16), ubs[h]) for h in hs]
        upd = [_dg(ubs[h], (ks[h // rep] * jnp.exp(bends[h] - b1s[h])).astype(BF16), _TN) for h in hs]
        for h in hs:
            st_ref[h] = jnp.exp(bends[h]) * sts[h] + upd[h]
            o = ebs[h] * qsts[h] + o_intra[h]
            on = o * lax.rsqrt(jnp.mean(o * o, axis=-1, keepdims=True) + EPS) * nrm
            out = on * _silu(_pad_rows(z_ref[0, rows, vsls[h]], c))
            o_ref[0, rows, vsls[h]] = out[0:tc].astype(o_ref.dtype)

    _run_chunks(nchunks, c, chunk)
    for h in range(nqk * rep):
        sfin_ref[0, h] = st_ref[h].T


def _delta(zc, zg, conv_w, buf, a_log, dt_bias, nrm, s0, chunk):
    b, t, _ = zc.shape
    c, tc, nchunks = _chunk_plan(t, chunk)
    rep = H_C_V // H_C_QK
    nqk = DELTA_QK_HEADS_PER_STEP
    nv = nqk * rep
    ng = H_C_QK // nqk
    qw = nqk * HD_C
    vw = nv * HD_C
    assert W_CQK % qw == 0 and (2 * W_CQK) % vw == 0
    kq0 = W_CQK // qw
    v0 = 2 * W_CQK // vw
    z0 = v0 + W_CV // vw
    buf8 = jnp.pad(buf, ((0, 0), (SUBLANES - (C_CONV - 1), 0), (0, 0)))
    pad_l = jnp.zeros((H_C_V,), F32)
    a_lane = jnp.pad(jnp.concatenate([pad_l, a_log]), (0, LANES - 2 * H_C_V)).reshape(1, LANES)
    d_lane = jnp.pad(jnp.concatenate([pad_l, dt_bias]), (0, LANES - 2 * H_C_V)).reshape(1, LANES)
    qmap = lambda bi, g: (bi, 0, g)
    kmap = lambda bi, g: (bi, 0, kq0 + g)
    vmap = lambda bi, g: (bi, 0, v0 + g)
    zmap = lambda bi, g: (bi, 0, z0 + g)
    full = lambda bi, g: (0, 0)
    sblk = (1, nv, HD_C, HD_C)
    return pl.pallas_call(
        functools.partial(_delta_kernel, c=c, tc=tc, nchunks=nchunks),
        grid=(b, ng),
        in_specs=[pl.BlockSpec((1, t, qw), qmap), pl.BlockSpec((1, t, qw), kmap),
                  pl.BlockSpec((1, t, vw), vmap), pl.BlockSpec((1, t, vw), zmap),
                  pl.BlockSpec((1, t, W_CG), lambda bi, g: (bi, 0, 0)),
                  pl.BlockSpec((C_CONV, qw), lambda bi, g: (0, g)),
                  pl.BlockSpec((C_CONV, qw), lambda bi, g: (0, kq0 + g)),
                  pl.BlockSpec((C_CONV, vw), lambda bi, g: (0, v0 + g)),
                  pl.BlockSpec((1, SUBLANES, qw), qmap), pl.BlockSpec((1, SUBLANES, qw), kmap),
                  pl.BlockSpec((1, SUBLANES, vw), vmap),
                  pl.BlockSpec((1, LANES), full), pl.BlockSpec((1, LANES), full),
                  pl.BlockSpec((1, HD_C), full),
                  pl.BlockSpec(sblk, lambda bi, g: (bi, g, 0, 0))],
        out_specs=[pl.BlockSpec((1, t, vw), lambda bi, g: (bi, 0, g)),
                   pl.BlockSpec(sblk, lambda bi, g: (bi, g, 0, 0))],
        out_shape=[jax.ShapeDtypeStruct((b, t, W_CV), BF16), jax.ShapeDtypeStruct((b, H_C_V, HD_C, HD_C), F32)],
        scratch_shapes=[pltpu.VMEM((nv, HD_C, HD_C), F32)],
        compiler_params=_cparams("parallel", "arbitrary"),
        name="gated_deltanet",
    )(zc, zc, zc, zc, zg, conv_w, conv_w, conv_w, buf8, buf8, buf8, a_lane, d_lane, nrm.reshape(1, HD_C), s0)


def _ret_kernel(q_ref, k_ref, v_ref, g_ref, nrm_ref, s0_ref, o_ref, sfin_ref, st_ref, *, c, tc, nchunks):
    nh = st_ref.shape[0]
    row = lax.broadcasted_iota(jnp.int32, (c, c), 0)
    col = lax.broadcasted_iota(jnp.int32, (c, c), 1)
    steps = (jnp.minimum(row + 1, tc) - jnp.minimum(col + 1, tc)).astype(F32)
    r1 = lax.broadcasted_iota(jnp.int32, (c, 1), 0)
    nsteps = jnp.minimum(r1 + 1, tc).astype(F32)
    valid = r1 < tc
    nrm = nrm_ref[...]
    lgs, decs = [], []
    for h in range(nh):
        hf = jnp.full((1, 1), pl.program_id(1) * nh + h, jnp.int32).astype(F32)
        lg = jnp.log1p(-jnp.exp2(-5.0 - hf))
        lgs.append(lg)
        decs.append(jnp.exp(jnp.where(row >= col, steps * lg, NEG)))
        st_ref[h] = s0_ref[0, h].T

    def chunk(ci, t0):
        rows = pl.ds(t0, tc)
        hs = range(nh)
        sls = [slice(h * HD_D, (h + 1) * HD_D) for h in hs]
        b1s = [nsteps * lgs[h] for h in hs]
        bends = [float(tc) * lgs[h] for h in hs]
        qs = [_pad_rows(q_ref[0, rows, sl], c).astype(BF16) for sl in sls]
        ks = [_pad_rows(k_ref[0, rows, sl], c) * HD_D ** -0.5 for sl in sls]
        if tc < c:
            ks = [jnp.where(valid, k, 0.0) for k in ks]
        vs = [_pad_rows(v_ref[0, rows, sl], c).astype(BF16) for sl in sls]
        sts = [st_ref[h] for h in hs]
        atts = [_dg(qs[h], ks[h].astype(BF16), _NT) * decs[h] for h in hs]
        o_inter = [_dg(qs[h], sts[h].astype(BF16), _NT) for h in hs]
        o_intra = [_dg(atts[h].astype(BF16), vs[h]) for h in hs]
        upd = [_dg(vs[h], (ks[h] * jnp.exp(bends[h] - b1s[h])).astype(BF16), _TN) for h in hs]
        for h in hs:
            sl = sls[h]
            st_ref[h] = jnp.exp(bends[h]) * sts[h] + upd[h]
            o = jnp.exp(b1s[h]) * o_inter[h] + o_intra[h]
            mu = jnp.mean(o, axis=-1, keepdims=True)
            oc = o - mu
            var = jnp.mean(oc * oc, axis=-1, keepdims=True)
            on = oc * lax.rsqrt(var + EPS) * nrm
            out = on * _silu(_pad_rows(g_ref[0, rows, sl], c))
            o_ref[0, rows, sl] = out[0:tc].astype(o_ref.dtype)

    _run_chunks(nchunks, c, chunk)
    for h in range(nh):
        sfin_ref[0, h] = st_ref[h].T


def _retention(zd, nrm, s0, chunk):
    b, t, _ = zd.shape
    c, tc, nchunks = _chunk_plan(t, chunk)
    nh = RET_HEADS_PER_STEP
    ng = H_D // nh
    blk = (1, t, nh * HD_D)
    sblk = (1, nh, HD_D, HD_D)
    return pl.pallas_call(
        functools.partial(_ret_kernel, c=c, tc=tc, nchunks=nchunks),
        grid=(b, ng),
        in_specs=[pl.BlockSpec(blk, lambda bi, g: (bi, 0, g)),
                  pl.BlockSpec(blk, lambda bi, g: (bi, 0, ng + g)),
                  pl.BlockSpec(blk, lambda bi, g: (bi, 0, 2 * ng + g)),
                  pl.BlockSpec(blk, lambda bi, g: (bi, 0, 3 * ng + g)),
                  pl.BlockSpec((1, HD_D), lambda bi, g: (0, 0)),
                  pl.BlockSpec(sblk, lambda bi, g: (bi, g, 0, 0))],
        out_specs=[pl.BlockSpec(blk, lambda bi, g: (bi, 0, g)),
                   pl.BlockSpec(sblk, lambda bi, g: (bi, g, 0, 0))],
        out_shape=[jax.ShapeDtypeStruct((b, t, W_D), BF16), jax.ShapeDtypeStruct((b, H_D, HD_D, HD_D), F32)],
        scratch_shapes=[pltpu.VMEM((nh, HD_D, HD_D), F32)],
        compiler_params=_cparams("parallel", "arbitrary"),
        name="retention",
    )(zd, zd, zd, zd, nrm.reshape(1, HD_D), s0)


def _block(x, pe, cache, s_hgrn, s_delta, buf_delta, s_ret, buf_ffn, p, layer):
    bsz, t, d = x.shape
    m = bsz * t
    x2 = x.reshape(m, d)
    hn = _rmsnorm(x2, p['attn_norm'][layer], BF16)
    za = _matmul(hn, p['w_in_a'], layer).reshape(bsz, t, -1)
    zb = _matmul(hn, p['w_in_b'], layer).reshape(bsz, t, -1)
    zc = _matmul(hn, p['w_in_c'], layer).reshape(bsz, t, -1)
    zg = _matmul(hn, p['w_in_g'], layer).reshape(bsz, t, -1)
    zd = _matmul(hn, p['w_in_d'], layer).reshape(bsz, t, -1)

    kv_new = za[:, :, W_A:].reshape(bsz, t, 2, H_A, HD_A)
    if cache is None:
        o_a = _attention_prompt(za)
        kv_new = kv_new[:, t - min(A_BRANCHES[-1][0], t):]
    else:
        o_a = _attention_sample(za, cache, layer)

    o_b, s_hgrn_new = _hgrn(zb, p['hgrn_lb'], p['hgrn_norm'][layer], s_hgrn, layer, GLA_CHUNK)
    o_c, s_delta_new = _delta(zc, zg, p['delta_conv'][layer], buf_delta, p['delta_A_log'][layer],
                              p['delta_dt_bias'][layer], p['delta_norm'][layer], s_delta, DELTA_CHUNK)
    pre = jnp.concatenate([buf_delta, zc[:, :, :2 * W_CQK + W_CV]], axis=1)
    buf_delta_new = pre[:, pre.shape[1] - (C_CONV - 1):]
    o_d, s_ret_new = _retention(zd, p['ret_norm'][layer], s_ret, RET_CHUNK)

    parts = [o.reshape(m, -1) for o in (o_a, o_b, o_c, o_d)]
    x2 = _matmul_residual(parts, p['w_out'], layer, x2, (1024, 512, 256, 64), (512, 256, 128))

    hf = _rmsnorm(x2, p['ffn_norm'][layer], BF16)
    if cache is None:
        act, buf_ffn_new = _ffn_gate_up_prompt(hf, p['w_gate'], p['w_up'], p['ffn_conv'], layer, t)
    else:
        act, buf_ffn_new = _ffn_gate_up_sample(hf, p['w_gate'], p['w_up'], p['ffn_conv'], layer, buf_ffn, t)
    x2 = _matmul_residual([act], p['w_down'], layer, x2, (512, 256, 64), (256, 128))

    hp = _rmsnorm(x2, p['ple_norm'][layer], BF16)
    x2 = _ple(hp, p['ple_gate'], pe.reshape(m, -1).astype(BF16), p['ple_proj'], layer, x2)
    states = (kv_new, s_hgrn_new, s_delta_new, buf_delta_new, s_ret_new, buf_ffn_new)
    return x2.reshape(bsz, t, d), states


def kernel(x_prompt, x_sample, cache_attn_kv, state_hgrn, state_delta, state_delta_conv, state_ret,
           state_ffn_conv, p_prompt, p_sample, attn_norm, w_in, hgrn_lb, hgrn_norm, delta_conv,
           delta_A_log, delta_dt_bias, delta_norm, ret_norm, w_out, ffn_norm, w_gate, w_up, ffn_conv,
           w_down, ple_norm, ple_gate, ple_proj, final_norm):
    depth = w_in.shape[0]
    bp = x_prompt.shape[0]
    t_s = x_sample.shape[1]
    o_b = 3 * W_A
    o_c = o_b + 4 * W_B
    o_g = o_c + 2 * W_CQK + 2 * W_CV
    o_d = o_g + 2 * H_C_V
    p = {'attn_norm': attn_norm,
         'w_in_a': w_in[:, :, :o_b].astype(BF16), 'w_in_b': w_in[:, :, o_b:o_c].astype(BF16),
         'w_in_c': w_in[:, :, o_c:o_g].astype(BF16),
         'w_in_g': jnp.pad(w_in[:, :, o_g:o_d], ((0, 0), (0, 0), (0, W_CG - 2 * H_C_V))).astype(BF16),
         'w_in_d': w_in[:, :, o_d:].astype(BF16),
         'hgrn_lb': hgrn_lb, 'hgrn_norm': hgrn_norm, 'delta_conv': delta_conv, 'delta_A_log': delta_A_log,
         'delta_dt_bias': delta_dt_bias, 'delta_norm': delta_norm, 'ret_norm': ret_norm,
         'w_out': w_out.astype(BF16), 'ffn_norm': ffn_norm, 'w_gate': w_gate.astype(BF16),
         'w_up': w_up.astype(BF16), 'ffn_conv': ffn_conv, 'w_down': w_down.astype(BF16),
         'ple_norm': ple_norm, 'ple_gate': ple_gate.astype(BF16), 'ple_proj': ple_proj.astype(BF16)}
    xp, xs = x_prompt, x_sample
    st_p, st_s = [], []
    for l in range(depth):
        zero = lambda *s: jnp.zeros((bp,) + s, F32)
        xp, sp = _block(xp, p_prompt[l], None, zero(H_B, HD_B, HD_B), zero(H_C_V, HD_C, HD_C),
                        zero(C_CONV - 1, 2 * W_CQK + W_CV), zero(H_D, HD_D, HD_D), None, p, l)
        xs, ss = _block(xs, p_sample[l], cache_attn_kv, state_hgrn[l], state_delta[l], state_delta_conv[l],
                        state_ret[l], state_ffn_conv[l], p, l)
        st_p.append(sp)
        st_s.append(ss)

    def stack(sts, i):
        return jnp.stack([s[i] for s in sts])

    def final(x):
        return _rmsnorm(x.reshape(-1, x.shape[-1]), final_norm, F32).reshape(x.shape)

    kv_sample = jnp.concatenate([cache_attn_kv[:, :, t_s:], stack(st_s, 0)], axis=2)
    return (final(xp), final(xs),
            stack(st_p, 0), kv_sample, stack(st_p, 1), stack(st_s, 1), stack(st_p, 2), stack(st_s, 2),
            stack(st_p, 3), stack(st_s, 3), stack(st_p, 4), stack(st_s, 4), stack(st_p, 5), stack(st_s, 5))
```

```python
import functools
import math

import jax
import jax.numpy as jnp
from jax import lax
from jax.experimental import pallas as pl
from jax.experimental.pallas import tpu as pltpu

F32 = jnp.float32
BF16 = jnp.bfloat16
EPS = 1e-6
NEG = -1e30

H_A = 8
HD_A = 128
A_BRANCHES = ((128, 1), (512, 4), (2048, 16))
A_BLOCK = 128
H_B = 8
HD_B = 128
H_C_QK = 4
H_C_V = 8
HD_C = 128
C_CONV = 4
H_D = 4
HD_D = 256
FFN_CONV = 3

W_A = H_A * HD_A
W_B = H_B * HD_B
W_CQK = H_C_QK * HD_C
W_CV = H_C_V * HD_C
W_D = H_D * HD_D
W_CG = 128

VMEM_LIMIT_BYTES = 52 * 1024 * 1024
SUBLANES = 8
LANES = 128

GLA_HEADS_PER_STEP = 8
DELTA_QK_HEADS_PER_STEP = 4
RET_HEADS_PER_STEP = 2
MIXER_ROW_BLOCK = 1024
GLA_CHUNK = 64
DELTA_CHUNK = 64
RET_CHUNK = 256
DELTA_SUB = 16

_NN = (((1,), (0,)), ((), ()))
_NT = (((1,), (1,)), ((), ()))
_TN = (((0,), (0,)), ((), ()))


def _cparams(*sem):
    return pltpu.CompilerParams(dimension_semantics=sem, vmem_limit_bytes=VMEM_LIMIT_BYTES)


def _dg(a, b, dn=_NN):
    return lax.dot_general(a, b, dn, preferred_element_type=F32)


def _dot1(a, b, dn=_NN):
    return _dg(a.astype(BF16), b.astype(BF16), dn)


def _split3(x):
    x1 = x.astype(BF16)
    r1 = x - x1.astype(F32)
    x2 = r1.astype(BF16)
    x3 = (r1 - x2.astype(F32)).astype(BF16)
    return x1, x2, x3


def _sel_dot(m, x):
    x1, x2, x3 = _split3(x)
    return _dg(m, x1) + (_dg(m, x2) + _dg(m, x3))


def _sigmoid(x):
    return 1.0 / (1.0 + jnp.exp(-x))


def _silu(x):
    return x * _sigmoid(x)


def _softplus(x):
    return jnp.maximum(x, 0.0) + jnp.log1p(jnp.exp(-jnp.abs(x)))


def _pad_rows(x, rows):
    if x.shape[0] == rows:
        return x
    return jnp.concatenate([x, jnp.zeros((rows - x.shape[0],) + x.shape[1:], x.dtype)], axis=0)


def _onehot(cond):
    return jnp.where(cond, 1.0, 0.0).astype(BF16)


def _pick(n, prefs):
    for p in prefs:
        if n % p == 0:
            return p
    return n


def _rmsnorm_kernel(x_ref, g_ref, o_ref):
    x = x_ref[...]
    y = x * lax.rsqrt(jnp.mean(x * x, axis=-1, keepdims=True) + EPS)
    o_ref[...] = (y * g_ref[...]).astype(o_ref.dtype)


def _rmsnorm(x, g, out_dtype):
    m, d = x.shape
    tm = _pick(m, (256, 64, 8))
    return pl.pallas_call(
        _rmsnorm_kernel,
        grid=(m // tm,),
        in_specs=[pl.BlockSpec((tm, d), lambda i: (i, 0)), pl.BlockSpec((1, d), lambda i: (0, 0))],
        out_specs=pl.BlockSpec((tm, d), lambda i: (i, 0)),
        out_shape=jax.ShapeDtypeStruct((m, d), out_dtype),
        compiler_params=_cparams("parallel"),
        name="rmsnorm",
    )(x, g.reshape(1, d))


def _mm_kernel(a_ref, w_ref, o_ref):
    o_ref[...] = _dg(a_ref[...], w_ref[...]).astype(o_ref.dtype)


def _matmul(a, w, layer, out_dtype=F32):
    m, k = a.shape
    n = w.shape[2]
    tm = _pick(m, (1024, 512, 256, 64))
    tn = _pick(n, (512, 256, 128))
    return pl.pallas_call(
        _mm_kernel,
        grid=(m // tm, n // tn),
        in_specs=[pl.BlockSpec((tm, k), lambda i, j: (i, 0)),
                  pl.BlockSpec((None, k, tn), lambda i, j: (layer, 0, j))],
        out_specs=pl.BlockSpec((tm, tn), lambda i, j: (i, j)),
        out_shape=jax.ShapeDtypeStruct((m, n), out_dtype),
        compiler_params=_cparams("parallel", "arbitrary"),
        name="matmul",
    )(a, w)


def _mm_res_kernel(*refs, nparts):
    a_refs = refs[:nparts]
    w_refs = refs[nparts:2 * nparts]
    x_ref = refs[2 * nparts]
    o_ref = refs[2 * nparts + 1]
    acc = x_ref[...]
    for a_ref, w_ref in zip(a_refs, w_refs):
        acc = acc + _dg(a_ref[...], w_ref[...])
    o_ref[...] = acc


def _matmul_residual(parts, w, layer, x, tm_prefs, tn_prefs):
    m, n = x.shape
    kp = parts[0].shape[1]
    nparts = len(parts)
    tm = _pick(m, tm_prefs)
    tn = _pick(n, tn_prefs)

    def wmap(p):
        return lambda i, j: (layer, p, j)

    in_specs = [pl.BlockSpec((tm, kp), lambda i, j: (i, 0)) for _ in parts]
    in_specs += [pl.BlockSpec((None, kp, tn), wmap(p)) for p in range(nparts)]
    in_specs += [pl.BlockSpec((tm, tn), lambda i, j: (i, j))]
    return pl.pallas_call(
        functools.partial(_mm_res_kernel, nparts=nparts),
        grid=(m // tm, n // tn),
        in_specs=in_specs,
        out_specs=pl.BlockSpec((tm, tn), lambda i, j: (i, j)),
        out_shape=jax.ShapeDtypeStruct((m, n), F32),
        compiler_params=_cparams("parallel", "arbitrary"),
        name="matmul_residual",
    )(*parts, *([w] * nparts), x)


def _ple_kernel(a_ref, wg_ref, pe_ref, wp_ref, x_ref, o_ref):
    gate = _sigmoid(_dg(a_ref[...], wg_ref[...]))
    o_ref[...] = x_ref[...] + gate * _dg(pe_ref[...], wp_ref[...])


def _ple(hp, wg, pe, wp, layer, x):
    m, d = x.shape
    k = hp.shape[1]
    kp = pe.shape[1]
    tm = _pick(m, (1024, 512, 256, 64))
    tn = _pick(d, (512, 256, 128))
    return pl.pallas_call(
        _ple_kernel,
        grid=(m // tm, d // tn),
        in_specs=[pl.BlockSpec((tm, k), lambda i, j: (i, 0)),
                  pl.BlockSpec((None, k, tn), lambda i, j: (layer, 0, j)),
                  pl.BlockSpec((tm, kp), lambda i, j: (i, 0)),
                  pl.BlockSpec((None, kp, tn), lambda i, j: (layer, 0, j)),
                  pl.BlockSpec((tm, tn), lambda i, j: (i, j))],
        out_specs=pl.BlockSpec((tm, tn), lambda i, j: (i, j)),
        out_shape=jax.ShapeDtypeStruct((m, d), F32),
        compiler_params=_cparams("parallel", "arbitrary"),
        name="ple",
    )(hp, wg, pe, wp, x)


def _ffn_act(g, p1, p2, cw, u):
    gc = cw[0:1] * p2 + cw[1:2] * p1 + cw[2:3] * g
    return (_silu(gc) * u).astype(BF16)


def _ffn_gu_prompt_kernel(h_ref, halo_ref, wg_ref, wu_ref, cw_ref, o_ref, tail_ref, *, seq):
    tm = h_ref.shape[0]
    a = h_ref[...]
    g = _dg(a, wg_ref[...])
    u = _dg(a, wu_ref[...])
    gh = _dg(halo_ref[...], wg_ref[...])
    seq_start = lax.rem(pl.program_id(0) * tm, seq) == 0
    gh = jnp.where(seq_start, 0.0, gh)
    row = lax.broadcasted_iota(jnp.int32, g.shape, 0)
    p1 = jnp.where(row == 0, gh[7:8], pltpu.roll(g, 1, 0))
    p2 = jnp.where(row == 0, gh[6:7], jnp.where(row == 1, gh[7:8], pltpu.roll(g, 2, 0)))
    o_ref[...] = _ffn_act(g, p1, p2, cw_ref[...], u)
    tail_ref[0] = g[tm - SUBLANES:tm]


def _ffn_gate_up_prompt(hf, wg, wu, cw, layer, seq):
    m, d = hf.shape
    f = wg.shape[2]
    tm = _pick(seq, (1024, 512, 256, 128, 64, 8))
    tf = _pick(f, (256, 128))
    hb = tm // SUBLANES
    wmap = lambda i, j: (layer, 0, j)
    out, tail = pl.pallas_call(
        functools.partial(_ffn_gu_prompt_kernel, seq=seq),
        grid=(m // tm, f // tf),
        in_specs=[pl.BlockSpec((tm, d), lambda i, j: (i, 0)),
                  pl.BlockSpec((SUBLANES, d), lambda i, j: (jnp.maximum(i * hb - 1, 0), 0)),
                  pl.BlockSpec((None, d, tf), wmap),
                  pl.BlockSpec((None, d, tf), wmap),
                  pl.BlockSpec((None, FFN_CONV, tf), wmap)],
        out_specs=[pl.BlockSpec((tm, tf), lambda i, j: (i, j)),
                   pl.BlockSpec((1, SUBLANES, tf), lambda i, j: (i, 0, j))],
        out_shape=[jax.ShapeDtypeStruct((m, f), BF16),
                   jax.ShapeDtypeStruct((m // tm, SUBLANES, f), F32)],
        compiler_params=_cparams("parallel", "arbitrary"),
        name="ffn_gate_up_prompt",
    )(hf, hf, wg, wu, cw)
    per_seq = seq // tm
    tail = tail.reshape(m // seq, per_seq, SUBLANES, f)[:, per_seq - 1, SUBLANES - (FFN_CONV - 1):]
    return out, tail


def _ffn_gu_sample_kernel(h_ref, wg_ref, wu_ref, cw_ref, b1_ref, b2_ref, o_ref, g_ref, *, t):
    a = h_ref[...]
    g = _dg(a, wg_ref[...])
    u = _dg(a, wu_ref[...])
    pos = lax.rem(lax.broadcasted_iota(jnp.int32, g.shape, 0), t)
    p1 = jnp.where(pos == 0, b1_ref[...], pltpu.roll(g, 1, 0))
    p2 = jnp.where(pos < 2, b2_ref[...], pltpu.roll(g, 2, 0))
    o_ref[...] = _ffn_act(g, p1, p2, cw_ref[...], u)
    g_ref[...] = g


def _ffn_gate_up_sample(hf, wg, wu, cw, layer, buf, t):
    m, d = hf.shape
    f = wg.shape[2]
    nb = m // t
    tf = _pick(f, (256, 128))
    zeros = jnp.zeros((nb, t - 1, f), F32)
    b1 = jnp.concatenate([buf[:, 1:2], zeros], axis=1).reshape(m, f)
    b2 = jnp.concatenate([buf[:, 0:2], zeros[:, 1:]], axis=1).reshape(m, f)
    wmap = lambda j: (layer, 0, j)
    out, g = pl.pallas_call(
        functools.partial(_ffn_gu_sample_kernel, t=t),
        grid=(f // tf,),
        in_specs=[pl.BlockSpec((m, d), lambda j: (0, 0)),
                  pl.BlockSpec((None, d, tf), wmap),
                  pl.BlockSpec((None, d, tf), wmap),
                  pl.BlockSpec((None, FFN_CONV, tf), wmap),
                  pl.BlockSpec((m, tf), lambda j: (0, j)),
                  pl.BlockSpec((m, tf), lambda j: (0, j))],
        out_specs=[pl.BlockSpec((m, tf), lambda j: (0, j)), pl.BlockSpec((m, tf), lambda j: (0, j))],
        out_shape=[jax.ShapeDtypeStruct((m, f), BF16), jax.ShapeDtypeStruct((m, f), F32)],
        compiler_params=_cparams("arbitrary"),
        name="ffn_gate_up_sample",
    )(hf, wg, wu, cw, b1, b2)
    tail = g.reshape(nb, t, f)[:, t - (FFN_CONV - 1):]
    return out, tail


ATTN_HEADS_PER_STEP = 2


def _band_softmax_many(probs, dist_cf, dist_pf, valid_c):
    scs = [_dot1(pr[0], pr[1], _NT) for pr in probs]
    sps = [None if pr[2] is None else _dot1(pr[0], pr[2], _NT) for pr in probs]
    pcs, pps, ls, lses = [], [], [], []
    for pr, sc, sp in zip(probs, scs, sps):
        slope, prev_ok = pr[5], pr[6]
        sc = jnp.where(valid_c, sc - slope * dist_cf, NEG)
        if sp is None:
            m = jnp.max(sc, axis=-1, keepdims=True)
            pc = jnp.exp(sc - m)
            pp = None
            l = jnp.sum(pc, axis=-1, keepdims=True)
        else:
            sp = jnp.where(prev_ok, sp - slope * dist_pf, NEG)
            m = jnp.max(jnp.maximum(sc, sp), axis=-1, keepdims=True)
            pc = jnp.exp(sc - m)
            pp = jnp.exp(sp - m)
            l = jnp.sum(pc + pp, axis=-1, keepdims=True)
        pcs.append(pc)
        pps.append(pp)
        ls.append(l)
        lses.append(m + jnp.log(l))
    ocs = [_dot1(pc, pr[3]) for pc, pr in zip(pcs, probs)]
    ops = [None if pp is None else _dot1(pp, pr[4]) for pp, pr in zip(pps, probs)]
    outs = [(oc if op is None else oc + op) / l for oc, op, l in zip(ocs, ops, ls)]
    return outs, lses


def _band_kernel(*refs, s):
    nh = ATTN_HEADS_PER_STEP
    q_refs, k_refs, v_refs = refs[0:nh], refs[nh:2 * nh], refs[2 * nh:3 * nh]
    o_ref, o2_ref, o3_ref, l2_ref, l3_ref = refs[3 * nh:]
    nq = A_BLOCK
    hg = pl.program_id(1)
    qi = lax.broadcasted_iota(jnp.int32, (nq, nq), 0)
    kj = lax.broadcasted_iota(jnp.int32, (nq, nq), 1)
    dist_c = qi - kj
    dist_p = dist_c + nq
    valid_c = dist_c >= 0
    dist_cf = dist_c.astype(F32)
    dist_pf = dist_p.astype(F32)
    (w1, d1), (w2, d2), (w3, d3) = A_BRANCHES
    assert d1 == 1 and s % (d2 * nq) == 0 and s == d3 * nq and max(w1 // d1, w2 // d2, w3 // d3) <= nq

    def problem(h, rows_c, rows_p, has_prev, band, dil):
        hv = jnp.full((1, 1), hg * nh + h + 1, jnp.int32).astype(F32)
        slope = float(dil) * jnp.exp2(-8.0 * hv / H_A)
        q = q_refs[h][0, rows_c, :] * HD_A ** -0.5
        if rows_p is None:
            return (q, k_refs[h][0, rows_c, :], None, v_refs[h][0, rows_c, :], None, slope, None)
        return (q, k_refs[h][0, rows_c, :], k_refs[h][0, rows_p, :], v_refs[h][0, rows_c, :],
                v_refs[h][0, rows_p, :], slope, (dist_p <= band) & has_prev)

    def dilated(i, carry):
        rows3 = pl.ds(i, nq, stride=d3)
        r = lax.rem(i, d2)
        b = i // d2
        rows2 = pl.ds(b * (d2 * nq) + r, nq, stride=d2)
        rows2p = pl.ds(jnp.maximum(b - 1, 0) * (d2 * nq) + r, nq, stride=d2)
        probs = [problem(h, rows3, None, False, w3 // d3, d3) for h in range(nh)]
        probs += [problem(h, rows2, rows2p, b > 0, w2 // d2, d2) for h in range(nh)]
        outs, lses = _band_softmax_many(probs, dist_cf, dist_pf, valid_c)
        for h in range(nh):
            o3_ref[h, rows3, :] = outs[h]
            l3_ref[h, rows3, :] = jnp.broadcast_to(lses[h], (nq, HD_A))
            o2_ref[h, rows2, :] = outs[nh + h]
            l2_ref[h, rows2, :] = jnp.broadcast_to(lses[nh + h], (nq, HD_A))
        return carry

    lax.fori_loop(0, s // nq, dilated, 0)

    def dense(i, carry):
        blocks = []
        for j in range(2):
            bi = 2 * i + j
            rows = pl.ds(pl.multiple_of(bi * nq, nq), nq)
            rows_p = pl.ds(pl.multiple_of(jnp.maximum(bi - 1, 0) * nq, nq), nq)
            blocks += [(h, rows, problem(h, rows, rows_p, bi > 0, w1 // d1, d1)) for h in range(nh)]
        outs, lses = _band_softmax_many([blk[2] for blk in blocks], dist_cf, dist_pf, valid_c)
        for (h, rows, _), o1, l1 in zip(blocks, outs, lses):
            l2 = l2_ref[h, rows, :]
            l3 = l3_ref[h, rows, :]
            mx = jnp.maximum(l1, jnp.maximum(l2, l3))
            e1 = jnp.exp(l1 - mx)
            e2 = jnp.exp(l2 - mx)
            e3 = jnp.exp(l3 - mx)
            o = (e1 * o1 + e2 * o2_ref[h, rows, :] + e3 * o3_ref[h, rows, :]) / (e1 + e2 + e3)
            o_ref[0, rows, h * HD_A:(h + 1) * HD_A] = o.astype(o_ref.dtype)
        return carry

    assert (s // nq) % 2 == 0
    lax.fori_loop(0, s // (2 * nq), dense, 0)


def _attention_prompt(za):
    b, s, _ = za.shape
    nh = ATTN_HEADS_PER_STEP
    ng = H_A // nh
    blk = (1, s, HD_A)

    def head_spec(first, h):
        return pl.BlockSpec(blk, lambda bi, g: (bi, 0, first + g * nh + h))

    in_specs = [head_spec(part * H_A, h) for part in range(3) for h in range(nh)]
    return pl.pallas_call(
        functools.partial(_band_kernel, s=s),
        grid=(b, ng),
        in_specs=in_specs,
        out_specs=pl.BlockSpec((1, s, nh * HD_A), lambda bi, g: (bi, 0, g)),
        out_shape=jax.ShapeDtypeStruct((b, s, W_A), BF16),
        scratch_shapes=[pltpu.VMEM((nh, s, HD_A), F32) for _ in range(4)],
        compiler_params=_cparams("parallel", "arbitrary"),
        name="band_attention",
    )(*([za] * (3 * nh)))


def _attn_sample_kernel(q_ref, kn_ref, vn_ref, c_ref, o_ref):
    t = q_ref.shape[1]
    rows_per_pos = 2 * H_A
    n_past = c_ref.shape[0] // rows_per_pos
    tp = 2 * SUBLANES
    shape_c = (tp, n_past)
    shape_n = (tp, tp)
    dist_c = n_past + lax.broadcasted_iota(jnp.int32, shape_c, 0) - lax.broadcasted_iota(jnp.int32, shape_c, 1)
    dist_n = lax.broadcasted_iota(jnp.int32, shape_n, 0) - lax.broadcasted_iota(jnp.int32, shape_n, 1)
    new_ok = lax.broadcasted_iota(jnp.int32, shape_n, 1) < t
    dist_cf = dist_c.astype(F32)
    dist_nf = dist_n.astype(F32)
    oks = []
    for window, dil in A_BRANCHES:
        oks.append(((dist_c <= window) & (lax.rem(dist_c, dil) == 0),
                    (dist_n >= 0) & (dist_n <= window) & (lax.rem(dist_n, dil) == 0) & new_ok))
    hs = range(H_A)
    slopes = [2.0 ** (-8.0 * (h + 1) / H_A) for h in hs]
    sls = [slice(h * HD_A, (h + 1) * HD_A) for h in hs]
    qs = [_pad_rows(q_ref[0, :, sl] * HD_A ** -0.5, tp).astype(BF16) for sl in sls]
    kns = [_pad_rows(kn_ref[0, :, sl], tp).astype(BF16) for sl in sls]
    vns = [_pad_rows(vn_ref[0, :, sl], tp).astype(BF16) for sl in sls]
    kcs = [c_ref[pl.ds(h, n_past, stride=rows_per_pos), :].astype(BF16) for h in hs]
    vcs = [c_ref[pl.ds(H_A + h, n_past, stride=rows_per_pos), :].astype(BF16) for h in hs]
    b_cs = [_dg(qs[h], kcs[h], _NT) - slopes[h] * dist_cf for h in hs]
    b_ns = [_dg(qs[h], kns[h], _NT) - slopes[h] * dist_nf for h in hs]
    pcs, pns, ls, lses = [], [], [], []
    for h in hs:
        for ok_c, ok_n in oks:
            sc = jnp.where(ok_c, b_cs[h], NEG)
            sn = jnp.where(ok_n, b_ns[h], NEG)
            m = jnp.maximum(jnp.max(sc, axis=-1, keepdims=True), jnp.max(sn, axis=-1, keepdims=True))
            pc = jnp.exp(sc - m)
            pn = jnp.exp(sn - m)
            pcs.append(pc.astype(BF16))
            pns.append(pn.astype(BF16))
            ls.append(jnp.sum(pc, axis=-1, keepdims=True) + jnp.sum(pn, axis=-1, keepdims=True))
            lses.append(m + jnp.log(ls[-1]))
    nbr = len(oks)
    outs = [(_dg(pcs[i], vcs[i // nbr]) + _dg(pns[i], vns[i // nbr])) / ls[i] for i in range(len(pcs))]
    for h in hs:
        lse_h = lses[h * nbr:(h + 1) * nbr]
        out_h = outs[h * nbr:(h + 1) * nbr]
        mx = functools.reduce(jnp.maximum, lse_h)
        es = [jnp.exp(x - mx) for x in lse_h]
        tot = functools.reduce(lambda a, b: a + b, es)
        o = functools.reduce(lambda a, b: a + b, [e * x for e, x in zip(es, out_h)]) / tot
        o_ref[0, :, sls[h]] = o[:t].astype(o_ref.dtype)


def _attention_sample(za, cache, layer):
    b, t, _ = za.shape
    depth, _, l = cache.shape[:3]
    rows = l * 2 * H_A
    cache2d = cache.reshape(depth, b, rows, HD_A)
    return pl.pallas_call(
        _attn_sample_kernel,
        grid=(b,),
        in_specs=[pl.BlockSpec((1, t, W_A), lambda bi: (bi, 0, 0)),
                  pl.BlockSpec((1, t, W_A), lambda bi: (bi, 0, 1)),
                  pl.BlockSpec((1, t, W_A), lambda bi: (bi, 0, 2)),
                  pl.BlockSpec((None, None, rows, HD_A), lambda bi: (layer, bi, 0, 0))],
        out_specs=pl.BlockSpec((1, t, W_A), lambda bi: (bi, 0, 0)),
        out_shape=jax.ShapeDtypeStruct((b, t, W_A), BF16),
        compiler_params=_cparams("parallel"),
        name="attention_sample",
    )(za, za, za, cache2d)


def _cast_kernel(x_ref, o_ref):
    o_ref[...] = x_ref[...].astype(o_ref.dtype)


CAST_BLOCK_BYTES = 6 * 1024 * 1024


def _cast_bf16(w, col0=0, ncols=None):
    depth, k, n = w.shape
    ncols = n - col0 if ncols is None else ncols
    tn = next(c for c in (1024, 512, 256, LANES) if ncols % c == 0 and col0 % c == 0)
    packed_rows = 2 * SUBLANES
    tk = max(d for d in range(packed_rows, k + 1, packed_rows) if k % d == 0 and d * tn * 4 <= CAST_BLOCK_BYTES)
    c0 = col0 // tn
    return pl.pallas_call(
        _cast_kernel,
        grid=(depth, k // tk, ncols // tn),
        in_specs=[pl.BlockSpec((1, tk, tn), lambda l, i, j: (l, i, c0 + j))],
        out_specs=pl.BlockSpec((1, tk, tn), lambda l, i, j: (l, i, j)),
        out_shape=jax.ShapeDtypeStruct((depth, k, ncols), BF16),
        compiler_params=_cparams("parallel", "parallel", "parallel"),
        name="cast_bf16",
    )(w)


def _kv_shift_kernel(c_ref, n_ref, o_ref):
    l = c_ref.shape[0]
    t = n_ref.shape[0]
    o_ref[0:l - t] = c_ref[t:l]
    o_ref[l - t:l] = n_ref[...]


def _kv_shift(cache, kv_new):
    depth, b, l, two, h, hd = cache.shape
    t = kv_new.shape[2]
    cblk = (None, None, l, None, h, hd)
    cmap = lambda d, bi, s: (d, bi, 0, s, 0, 0)
    return pl.pallas_call(
        _kv_shift_kernel,
        grid=(depth, b, two),
        in_specs=[pl.BlockSpec(cblk, cmap), pl.BlockSpec((None, None, t, None, h, hd), cmap)],
        out_specs=pl.BlockSpec(cblk, cmap),
        out_shape=jax.ShapeDtypeStruct(cache.shape, cache.dtype),
        compiler_params=_cparams("parallel", "parallel", "parallel"),
        name="kv_shift",
    )(cache, kv_new)


def _chunk_plan(t, chunk):
    c = chunk if t >= chunk else max(2 * SUBLANES, t)
    assert t % c == 0 or t < c
    tc = min(t, c)
    return c, tc, max(t // c, 1)


def _run_chunks(nchunks, c, chunk):
    if nchunks == 1:
        chunk(0, 0)
    else:
        def body(ci, carry):
            chunk(ci, pl.multiple_of(ci * c, c))
            return carry
        lax.fori_loop(0, nchunks, body, 0)


def _gla_kernel(q_ref, f_ref, i_ref, g_ref, lb_ref, nrm_ref, s0_ref, o_ref, sfin_ref, st_ref, *, layer, c, tc, nchunks):
    nlev = int(math.log2(c))
    assert 1 << nlev == c
    depth = lb_ref.shape[0]
    nh = st_ref.shape[0]
    lbs = [lb_ref[i] for i in range(depth)]
    mx = functools.reduce(jnp.maximum, lbs)
    es = [jnp.exp(x - mx) for x in lbs]
    tot = functools.reduce(lambda a, b: a + b, es)
    lower = functools.reduce(lambda a, b: a + b, [es[i] / tot for i in range(layer + 1)]) - es[0] / tot
    one_minus_lb = 1.0 - lower

    row = lax.broadcasted_iota(jnp.int32, (c, c), 0)
    col = lax.broadcasted_iota(jnp.int32, (c, c), 1)
    prefix = [_onehot(col <= row)]
    lmask = []
    for lev in range(1, nlev + 1):
        s = c >> lev
        grp = ~(2 * s - 1)
        prefix.append(_onehot(col <= (row & grp) + (s - 1)))
        lmask.append((((row ^ col) & grp) == 0) & ((row & s) != 0) & ((col & s) == 0))
    mstack = jnp.concatenate(prefix, axis=0)
    eye = row == col
    valid = lax.broadcasted_iota(jnp.int32, (c, 1), 0) < tc
    nrm = nrm_ref[...]

    @pl.when(pl.program_id(2) == 0)
    def _():
        for h in range(nh):
            st_ref[h] = s0_ref[0, h].T

    def chunk(ci, t0):
        rows = pl.ds(t0, tc)
        hs = range(nh)
        sls = [slice(h * HD_B, (h + 1) * HD_B) for h in hs]
        qs = [_silu(_pad_rows(q_ref[0, rows, sl], c)) for sl in sls]
        kbs = [one_minus_lb[:, sl] * _sigmoid(-_pad_rows(f_ref[0, rows, sl], c)) for sl in sls]
        logfs = [jnp.log1p(-kb) for kb in kbs]
        if tc < c:
            kbs = [jnp.where(valid, kb, 0.0) for kb in kbs]
            logfs = [jnp.where(valid, lf, 0.0) for lf in logfs]
        vs = [_pad_rows(i_ref[0, rows, sl], c).astype(BF16) for sl in sls]
        bs_all = _sel_dot(mstack, jnp.concatenate(logfs, axis=1))
        bss = [bs_all[:, sl] for sl in sls]
        b = [bs[0:c] for bs in bss]
        atts = [jnp.where(eye, jnp.sum(q * kb, axis=-1, keepdims=True), 0.0) for q, kb in zip(qs, kbs)]
        for lev in range(1, nlev + 1):
            brs = [bs[lev * c:(lev + 1) * c] for bs in bss]
            qls = [qs[h] * jnp.exp(jnp.minimum(b[h] - brs[h], 0.0)) for h in hs]
            kls = [kbs[h] * jnp.exp(jnp.minimum(brs[h] - b[h], 0.0)) for h in hs]
            prods = [_dot1(ql, kl, _NT) for ql, kl in zip(qls, kls)]
            atts = [att + jnp.where(lmask[lev - 1], pr, 0.0) for att, pr in zip(atts, prods)]
        sts = [st_ref[h] for h in hs]
        o_inter = [_dot1(qs[h] * jnp.exp(b[h]), sts[h], _NT) for h in hs]
        o_intra = [_dg(atts[h].astype(BF16), vs[h]) for h in hs]
        bends = [b[h][c - 1:c] for h in hs]
        upd = [_dg(vs[h], (kbs[h] * jnp.exp(bends[h] - b[h])).astype(BF16), _TN) for h in hs]
        for h in hs:
            st_ref[h] = sts[h] * jnp.exp(bends[h]) + upd[h]
            o = o_inter[h] + o_intra[h]
            on = o * lax.rsqrt(jnp.mean(o * o, axis=-1, keepdims=True) + EPS) * nrm
            out = on * _silu(_pad_rows(g_ref[0, rows, sls[h]], c))
            o_ref[0, rows, sls[h]] = out[0:tc].astype(o_ref.dtype)

    _run_chunks(nchunks, c, chunk)

    @pl.when(pl.program_id(2) == pl.num_programs(2) - 1)
    def _():
        for h in range(nh):
            sfin_ref[0, h] = st_ref[h].T


def _hgrn(zb, lb, nrm, s0, layer, chunk):
    b, t, _ = zb.shape
    tb = min(t, MIXER_ROW_BLOCK)
    assert t % tb == 0
    c, tc, nchunks = _chunk_plan(tb, chunk)
    depth = lb.shape[0]
    nh = GLA_HEADS_PER_STEP
    ng = H_B // nh
    blk = (1, tb, nh * HD_B)
    sblk = (1, nh, HD_B, HD_B)
    return pl.pallas_call(
        functools.partial(_gla_kernel, layer=layer, c=c, tc=tc, nchunks=nchunks),
        grid=(b, ng, t // tb),
        in_specs=[pl.BlockSpec(blk, lambda bi, g, ti: (bi, ti, g)),
                  pl.BlockSpec(blk, lambda bi, g, ti: (bi, ti, ng + g)),
                  pl.BlockSpec(blk, lambda bi, g, ti: (bi, ti, 2 * ng + g)),
                  pl.BlockSpec(blk, lambda bi, g, ti: (bi, ti, 3 * ng + g)),
                  pl.BlockSpec((depth, 1, nh * HD_B), lambda bi, g, ti: (0, 0, g)),
                  pl.BlockSpec((1, HD_B), lambda bi, g, ti: (0, 0)),
                  pl.BlockSpec(sblk, lambda bi, g, ti: (bi, g, 0, 0))],
        out_specs=[pl.BlockSpec(blk, lambda bi, g, ti: (bi, ti, g)),
                   pl.BlockSpec(sblk, lambda bi, g, ti: (bi, g, 0, 0))],
        out_shape=[jax.ShapeDtypeStruct((b, t, W_B), BF16), jax.ShapeDtypeStruct((b, H_B, HD_B, HD_B), F32)],
        scratch_shapes=[pltpu.VMEM((nh, HD_B, HD_B), F32)],
        compiler_params=_cparams("parallel", "parallel", "arbitrary"),
        name="hgrn2",
    )(zb, zb, zb, zb, lb.reshape(depth, 1, W_B), nrm.reshape(1, HD_B), s0)


def _delta_kernel(q_ref, k_ref, v_ref, z_ref, zg_ref, wq_ref, wk_ref, wv_ref, bq_ref, bk_ref, bv_ref,
                  hq_ref, hk_ref, hv_ref, al_ref, dl_ref, nrm_ref, s0_ref, o_ref, sfin_ref, st_ref,
                  *, c, tc, nchunks):
    g = pl.program_id(1)
    first_block = pl.program_id(2) == 0
    nqk = DELTA_QK_HEADS_PER_STEP
    rep = H_C_V // H_C_QK
    sb = min(DELTA_SUB, c)
    row = lax.broadcasted_iota(jnp.int32, (c, c), 0)
    col = lax.broadcasted_iota(jnp.int32, (c, c), 1)
    lower_incl = _onehot(col <= row)
    eye = jnp.where(row == col, 1.0, 0.0)
    same_sub = ((row ^ col) & ~(sb - 1)) == 0
    valid = lax.broadcasted_iota(jnp.int32, (c, 1), 0) < tc
    lane = lax.broadcasted_iota(jnp.int32, (c, LANES), 1)
    nrm = nrm_ref[...]

    @pl.when(first_block)
    def _():
        for h in range(nqk * rep):
            st_ref[h] = s0_ref[0, h].T

    def conv(x_ref, w_ref, buf_ref, halo_ref, sl, ci, t0):
        x = x_ref[0, pl.ds(t0, tc), sl]
        halo = jnp.where(first_block, buf_ref[0, :, sl], halo_ref[0, :, sl])
        if nchunks > 1:
            prev = x_ref[0, pl.ds(pl.multiple_of(jnp.maximum(t0 - SUBLANES, 0), SUBLANES), SUBLANES), sl]
            halo = jnp.where(ci == 0, halo, prev)
        xh = jnp.concatenate([halo, x], axis=0)
        w = w_ref[:, sl]
        y = w[C_CONV - 1:C_CONV] * x
        for s in range(1, C_CONV):
            y = y + w[C_CONV - 1 - s:C_CONV - s] * pltpu.roll(xh, s, 0)[SUBLANES:]
        return _pad_rows(_silu(y), c)

    def l2n(x):
        return x * lax.rsqrt(jnp.sum(x * x, axis=-1, keepdims=True) + EPS)

    def chunk(ci, t0):
        rows = pl.ds(t0, tc)
        zg = _pad_rows(zg_ref[0, rows, :], c)
        beta_all = _sigmoid(zg)
        la_all = -jnp.exp(al_ref[...]) * _softplus(zg + dl_ref[...])
        nv = nqk * rep
        hs = range(nv)
        qsls = [slice(qh * HD_C, (qh + 1) * HD_C) for qh in range(nqk)]
        vsls = [slice(h * HD_C, (h + 1) * HD_C) for h in hs]
        qs = [l2n(conv(q_ref, wq_ref, bq_ref, hq_ref, sl, ci, t0)) * HD_C ** -0.5 for sl in qsls]
        ks = [l2n(conv(k_ref, wk_ref, bk_ref, hk_ref, sl, ci, t0)) for sl in qsls]
        if tc < c:
            ks = [jnp.where(valid, k, 0.0) for k in ks]
        qbs = [q.astype(BF16) for q in qs]
        kbs = [k.astype(BF16) for k in ks]
        kks = [_dg(kb, kb, _NT) for kb in kbs]
        qks = [_dg(qb, kb, _NT) for qb, kb in zip(qbs, kbs)]
        vs = [conv(v_ref, wv_ref, bv_ref, hv_ref, sl, ci, t0) for sl in vsls]
        betas = [jnp.sum(jnp.where(lane == g * nv + h, beta_all, 0.0), axis=-1, keepdims=True) for h in hs]
        las = [jnp.sum(jnp.where(lane == H_C_V + g * nv + h, la_all, 0.0), axis=-1, keepdims=True) for h in hs]
        if tc < c:
            betas = [jnp.where(valid, x, 0.0) for x in betas]
            las = [jnp.where(valid, x, 0.0) for x in las]
        bc_all = _sel_dot(lower_incl, jnp.concatenate([jnp.broadcast_to(la, (c, LANES)) for la in las], axis=1))
        bcols = [bc_all[:, h * LANES:h * LANES + c] for h in hs]
        decs = [jnp.exp(jnp.where(row >= col, bc - bc.T, NEG)) for bc in bcols]
        b1s = [bc[:, 0:1] for bc in bcols]
        ebs = [jnp.exp(b1) for b1 in b1s]
        bends = [b1[c - 1:c] for b1 in b1s]
        ns = [jnp.where(row > col, betas[h] * (kks[h // rep] * decs[h]), 0.0) for h in hs]
        nds = [jnp.where(same_sub, n, 0.0) for n in ns]
        xs = [eye - nd for nd in nds]
        ps = nds
        for _ in range(int(math.log2(sb)) - 1):
            ps = [_dot1(p, p) for p in ps]
            xs = [x + _dot1(x, p) for x, p in zip(xs, ps)]
        nblk = c // sb
        if nblk > 1:
            mms = [_dot1(x, n - nd) for x, n, nd in zip(xs, ns, nds)]
            ys = [eye - mm for mm in mms]
            ps = mms
            for _ in range(int(math.log2(nblk)) - 1):
                ps = [_dot1(p, p) for p in ps]
                ys = [y + _dot1(y, p) for y, p in zip(ys, ps)]
            xs = [_dot1(y, x) for y, x in zip(ys, xs)]
        sts = [st_ref[h] for h in hs]
        stbs = [st.astype(BF16) for st in sts]
        ksts = [_dg(kbs[h // rep], stbs[h], _NT) for h in hs]
        qsts = [_dg(qbs[h // rep], stbs[h], _NT) for h in hs]
        rhss = [betas[h] * (vs[h] - ebs[h] * ksts[h]) for h in hs]
        ubs = [_dot1(x, rhs).astype(BF16) for x, rhs in zip(xs, rhss)]
        o_intra = [_dg((qks[h // rep] * decs[h]).astype(BF16), ubs[h]) for h in hs]
        upd = [_dg(ubs[h], (ks[h // rep] * jnp.exp(bends[h] - b1s[h])).astype(BF16), _TN) for h in hs]
        for h in hs:
            st_ref[h] = jnp.exp(bends[h]) * sts[h] + upd[h]
            o = ebs[h] * qsts[h] + o_intra[h]
            on = o * lax.rsqrt(jnp.mean(o * o, axis=-1, keepdims=True) + EPS) * nrm
            out = on * _silu(_pad_rows(z_ref[0, rows, vsls[h]], c))
            o_ref[0, rows, vsls[h]] = out[0:tc].astype(o_ref.dtype)

    _run_chunks(nchunks, c, chunk)

    @pl.when(pl.program_id(2) == pl.num_programs(2) - 1)
    def _():
        for h in range(nqk * rep):
            sfin_ref[0, h] = st_ref[h].T


def _delta(zc, zg, conv_w, buf, a_log, dt_bias, nrm, s0, chunk):
    b, t, _ = zc.shape
    tb = min(t, MIXER_ROW_BLOCK)
    assert t % tb == 0 and tb % SUBLANES == 0
    c, tc, nchunks = _chunk_plan(tb, chunk)
    rep = H_C_V // H_C_QK
    nqk = DELTA_QK_HEADS_PER_STEP
    nv = nqk * rep
    ng = H_C_QK // nqk
    qw = nqk * HD_C
    vw = nv * HD_C
    assert W_CQK % qw == 0 and (2 * W_CQK) % vw == 0
    kq0 = W_CQK // qw
    v0 = 2 * W_CQK // vw
    z0 = v0 + W_CV // vw
    buf8 = jnp.pad(buf, ((0, 0), (SUBLANES - (C_CONV - 1), 0), (0, 0)))
    pad_l = jnp.zeros((H_C_V,), F32)
    a_lane = jnp.pad(jnp.concatenate([pad_l, a_log]), (0, LANES - 2 * H_C_V)).reshape(1, LANES)
    d_lane = jnp.pad(jnp.concatenate([pad_l, dt_bias]), (0, LANES - 2 * H_C_V)).reshape(1, LANES)
    hb = tb // SUBLANES

    def rows(col0):
        return lambda bi, g, ti: (bi, ti, col0 + g)

    def first(col0):
        return lambda bi, g, ti: (bi, 0, col0 + g)

    def halo(col0):
        return lambda bi, g, ti: (bi, jnp.maximum(ti * hb - 1, 0), col0 + g)

    full = lambda bi, g, ti: (0, 0)
    sblk = (1, nv, HD_C, HD_C)
    smap = lambda bi, g, ti: (bi, g, 0, 0)
    return pl.pallas_call(
        functools.partial(_delta_kernel, c=c, tc=tc, nchunks=nchunks),
        grid=(b, ng, t // tb),
        in_specs=[pl.BlockSpec((1, tb, qw), rows(0)), pl.BlockSpec((1, tb, qw), rows(kq0)),
                  pl.BlockSpec((1, tb, vw), rows(v0)), pl.BlockSpec((1, tb, vw), rows(z0)),
                  pl.BlockSpec((1, tb, W_CG), lambda bi, g, ti: (bi, ti, 0)),
                  pl.BlockSpec((C_CONV, qw), lambda bi, g, ti: (0, g)),
                  pl.BlockSpec((C_CONV, qw), lambda bi, g, ti: (0, kq0 + g)),
                  pl.BlockSpec((C_CONV, vw), lambda bi, g, ti: (0, v0 + g)),
                  pl.BlockSpec((1, SUBLANES, qw), first(0)), pl.BlockSpec((1, SUBLANES, qw), first(kq0)),
                  pl.BlockSpec((1, SUBLANES, vw), first(v0)),
                  pl.BlockSpec((1, SUBLANES, qw), halo(0)), pl.BlockSpec((1, SUBLANES, qw), halo(kq0)),
                  pl.BlockSpec((1, SUBLANES, vw), halo(v0)),
                  pl.BlockSpec((1, LANES), full), pl.BlockSpec((1, LANES), full),
                  pl.BlockSpec((1, HD_C), full),
                  pl.BlockSpec(sblk, smap)],
        out_specs=[pl.BlockSpec((1, tb, vw), rows(0)), pl.BlockSpec(sblk, smap)],
        out_shape=[jax.ShapeDtypeStruct((b, t, W_CV), BF16), jax.ShapeDtypeStruct((b, H_C_V, HD_C, HD_C), F32)],
        scratch_shapes=[pltpu.VMEM((nv, HD_C, HD_C), F32)],
        compiler_params=_cparams("parallel", "parallel", "arbitrary"),
        name="gated_deltanet",
    )(zc, zc, zc, zc, zg, conv_w, conv_w, conv_w, buf8, buf8, buf8, zc, zc, zc,
      a_lane, d_lane, nrm.reshape(1, HD_C), s0)


def _ret_kernel(q_ref, k_ref, v_ref, g_ref, nrm_ref, s0_ref, o_ref, sfin_ref, st_ref, *, c, tc, nchunks):
    nh = st_ref.shape[0]
    row = lax.broadcasted_iota(jnp.int32, (c, c), 0)
    col = lax.broadcasted_iota(jnp.int32, (c, c), 1)
    steps = (jnp.minimum(row + 1, tc) - jnp.minimum(col + 1, tc)).astype(F32)
    r1 = lax.broadcasted_iota(jnp.int32, (c, 1), 0)
    nsteps = jnp.minimum(r1 + 1, tc).astype(F32)
    valid = r1 < tc
    nrm = nrm_ref[...]
    lgs, decs = [], []
    for h in range(nh):
        hf = jnp.full((1, 1), pl.program_id(1) * nh + h, jnp.int32).astype(F32)
        lg = jnp.log1p(-jnp.exp2(-5.0 - hf))
        lgs.append(lg)
        decs.append(jnp.exp(jnp.where(row >= col, steps * lg, NEG)))
        st_ref[h] = s0_ref[0, h].T

    def chunk(ci, t0):
        rows = pl.ds(t0, tc)
        hs = range(nh)
        sls = [slice(h * HD_D, (h + 1) * HD_D) for h in hs]
        b1s = [nsteps * lgs[h] for h in hs]
        bends = [float(tc) * lgs[h] for h in hs]
        qs = [_pad_rows(q_ref[0, rows, sl], c).astype(BF16) for sl in sls]
        ks = [_pad_rows(k_ref[0, rows, sl], c) * HD_D ** -0.5 for sl in sls]
        if tc < c:
            ks = [jnp.where(valid, k, 0.0) for k in ks]
        vs = [_pad_rows(v_ref[0, rows, sl], c).astype(BF16) for sl in sls]
        sts = [st_ref[h] for h in hs]
        atts = [_dg(qs[h], ks[h].astype(BF16), _NT) * decs[h] for h in hs]
        o_inter = [_dg(qs[h], sts[h].astype(BF16), _NT) for h in hs]
        o_intra = [_dg(atts[h].astype(BF16), vs[h]) for h in hs]
        upd = [_dg(vs[h], (ks[h] * jnp.exp(bends[h] - b1s[h])).astype(BF16), _TN) for h in hs]
        for h in hs:
            sl = sls[h]
            st_ref[h] = jnp.exp(bends[h]) * sts[h] + upd[h]
            o = jnp.exp(b1s[h]) * o_inter[h] + o_intra[h]
            mu = jnp.mean(o, axis=-1, keepdims=True)
            oc = o - mu
            var = jnp.mean(oc * oc, axis=-1, keepdims=True)
            on = oc * lax.rsqrt(var + EPS) * nrm
            out = on * _silu(_pad_rows(g_ref[0, rows, sl], c))
            o_ref[0, rows, sl] = out[0:tc].astype(o_ref.dtype)

    _run_chunks(nchunks, c, chunk)
    for h in range(nh):
        sfin_ref[0, h] = st_ref[h].T


def _retention(zd, nrm, s0, chunk):
    b, t, _ = zd.shape
    c, tc, nchunks = _chunk_plan(t, chunk)
    nh = RET_HEADS_PER_STEP
    ng = H_D // nh
    blk = (1, t, nh * HD_D)
    sblk = (1, nh, HD_D, HD_D)
    return pl.pallas_call(
        functools.partial(_ret_kernel, c=c, tc=tc, nchunks=nchunks),
        grid=(b, ng),
        in_specs=[pl.BlockSpec(blk, lambda bi, g: (bi, 0, g)),
                  pl.BlockSpec(blk, lambda bi, g: (bi, 0, ng + g)),
                  pl.BlockSpec(blk, lambda bi, g: (bi, 0, 2 * ng + g)),
                  pl.BlockSpec(blk, lambda bi, g: (bi, 0, 3 * ng + g)),
                  pl.BlockSpec((1, HD_D), lambda bi, g: (0, 0)),
                  pl.BlockSpec(sblk, lambda bi, g: (bi, g, 0, 0))],
        out_specs=[pl.BlockSpec(blk, lambda bi, g: (bi, 0, g)),
                   pl.BlockSpec(sblk, lambda bi, g: (bi, g, 0, 0))],
        out_shape=[jax.ShapeDtypeStruct((b, t, W_D), BF16), jax.ShapeDtypeStruct((b, H_D, HD_D, HD_D), F32)],
        scratch_shapes=[pltpu.VMEM((nh, HD_D, HD_D), F32)],
        compiler_params=_cparams("parallel", "arbitrary"),
        name="retention",
    )(zd, zd, zd, zd, nrm.reshape(1, HD_D), s0)


def _block(x, pe, cache, s_hgrn, s_delta, buf_delta, s_ret, buf_ffn, p, layer):
    bsz, t, d = x.shape
    m = bsz * t
    x2 = x.reshape(m, d)
    hn = _rmsnorm(x2, p['attn_norm'][layer], BF16)
    za = _matmul(hn, p['w_in_a'], layer).reshape(bsz, t, -1)
    zb = _matmul(hn, p['w_in_b'], layer).reshape(bsz, t, -1)
    zc = _matmul(hn, p['w_in_c'], layer).reshape(bsz, t, -1)
    zg = _matmul(hn, p['w_in_g'], layer).reshape(bsz, t, -1)
    zd = _matmul(hn, p['w_in_d'], layer).reshape(bsz, t, -1)

    kv_new = za[:, :, W_A:].reshape(bsz, t, 2, H_A, HD_A)
    if cache is None:
        o_a = _attention_prompt(za)
        kv_new = kv_new[:, t - min(A_BRANCHES[-1][0], t):]
    else:
        o_a = _attention_sample(za, cache, layer)

    o_b, s_hgrn_new = _hgrn(zb, p['hgrn_lb'], p['hgrn_norm'][layer], s_hgrn, layer, GLA_CHUNK)
    o_c, s_delta_new = _delta(zc, zg, p['delta_conv'][layer], buf_delta, p['delta_A_log'][layer],
                              p['delta_dt_bias'][layer], p['delta_norm'][layer], s_delta, DELTA_CHUNK)
    pre = jnp.concatenate([buf_delta, zc[:, :, :2 * W_CQK + W_CV]], axis=1)
    buf_delta_new = pre[:, pre.shape[1] - (C_CONV - 1):]
    o_d, s_ret_new = _retention(zd, p['ret_norm'][layer], s_ret, RET_CHUNK)

    parts = [o.reshape(m, -1) for o in (o_a, o_b, o_c, o_d)]
    x2 = _matmul_residual(parts, p['w_out'], layer, x2, (1024, 512, 256, 64), (512, 256, 128))

    hf = _rmsnorm(x2, p['ffn_norm'][layer], BF16)
    if cache is None:
        act, buf_ffn_new = _ffn_gate_up_prompt(hf, p['w_gate'], p['w_up'], p['ffn_conv'], layer, t)
    else:
        act, buf_ffn_new = _ffn_gate_up_sample(hf, p['w_gate'], p['w_up'], p['ffn_conv'], layer, buf_ffn, t)
    x2 = _matmul_residual([act], p['w_down'], layer, x2, (512, 256, 64), (256, 128))

    hp = _rmsnorm(x2, p['ple_norm'][layer], BF16)
    x2 = _ple(hp, p['ple_gate'], pe.reshape(m, -1).astype(BF16), p['ple_proj'], layer, x2)
    states = (kv_new, s_hgrn_new, s_delta_new, buf_delta_new, s_ret_new, buf_ffn_new)
    return x2.reshape(bsz, t, d), states


def kernel(x_prompt, x_sample, cache_attn_kv, state_hgrn, state_delta, state_delta_conv, state_ret,
           state_ffn_conv, p_prompt, p_sample, attn_norm, w_in, hgrn_lb, hgrn_norm, delta_conv,
           delta_A_log, delta_dt_bias, delta_norm, ret_norm, w_out, ffn_norm, w_gate, w_up, ffn_conv,
           w_down, ple_norm, ple_gate, ple_proj, final_norm):
    depth = w_in.shape[0]
    bp = x_prompt.shape[0]
    o_b = 3 * W_A
    o_c = o_b + 4 * W_B
    o_g = o_c + 2 * W_CQK + 2 * W_CV
    o_d = o_g + 2 * H_C_V
    p = {'attn_norm': attn_norm,
         'w_in_a': _cast_bf16(w_in, 0, o_b), 'w_in_b': _cast_bf16(w_in, o_b, o_c - o_b),
         'w_in_c': _cast_bf16(w_in, o_c, o_g - o_c),
         'w_in_g': jnp.pad(w_in[:, :, o_g:o_d], ((0, 0), (0, 0), (0, W_CG - 2 * H_C_V))).astype(BF16),
         'w_in_d': w_in[:, :, o_d:].astype(BF16),
         'hgrn_lb': hgrn_lb, 'hgrn_norm': hgrn_norm, 'delta_conv': delta_conv, 'delta_A_log': delta_A_log,
         'delta_dt_bias': delta_dt_bias, 'delta_norm': delta_norm, 'ret_norm': ret_norm,
         'w_out': _cast_bf16(w_out), 'ffn_norm': ffn_norm, 'w_gate': _cast_bf16(w_gate),
         'w_up': _cast_bf16(w_up), 'ffn_conv': ffn_conv, 'w_down': _cast_bf16(w_down),
         'ple_norm': ple_norm, 'ple_gate': _cast_bf16(ple_gate), 'ple_proj': _cast_bf16(ple_proj)}
    xp, xs = x_prompt, x_sample
    st_p, st_s = [], []
    for l in range(depth):
        zero = lambda *s: jnp.zeros((bp,) + s, F32)
        xp, sp = _block(xp, p_prompt[l], None, zero(H_B, HD_B, HD_B), zero(H_C_V, HD_C, HD_C),
                        zero(C_CONV - 1, 2 * W_CQK + W_CV), zero(H_D, HD_D, HD_D), None, p, l)
        xs, ss = _block(xs, p_sample[l], cache_attn_kv, state_hgrn[l], state_delta[l], state_delta_conv[l],
                        state_ret[l], state_ffn_conv[l], p, l)
        st_p.append(sp)
        st_s.append(ss)

    def stack(sts, i):
        return jnp.stack([s[i] for s in sts])

    def final(x):
        return _rmsnorm(x.reshape(-1, x.shape[-1]), final_norm, F32).reshape(x.shape)

    kv_sample = _kv_shift(cache_attn_kv, stack(st_s, 0))
    return (final(xp), final(xs),
            stack(st_p, 0), kv_sample, stack(st_p, 1), stack(st_s, 1), stack(st_p, 2), stack(st_s, 2),
            stack(st_p, 3), stack(st_s, 3), stack(st_p, 4), stack(st_s, 4), stack(st_p, 5), stack(st_s, 5))
```

```python
import functools
import math

import jax
import jax.numpy as jnp
from jax import lax
from jax.experimental import pallas as pl
from jax.experimental.pallas import tpu as pltpu

F32 = jnp.float32
BF16 = jnp.bfloat16
EPS = 1e-6
NEG = -1e30

H_A = 8
HD_A = 128
A_BRANCHES = ((128, 1), (512, 4), (2048, 16))
A_BLOCK = 128
H_B = 8
HD_B = 128
H_C_QK = 4
H_C_V = 8
HD_C = 128
C_CONV = 4
H_D = 4
HD_D = 256
FFN_CONV = 3

W_A = H_A * HD_A
W_B = H_B * HD_B
W_CQK = H_C_QK * HD_C
W_CV = H_C_V * HD_C
W_D = H_D * HD_D
W_CG = 128

VMEM_LIMIT_BYTES = 52 * 1024 * 1024
SUBLANES = 8
LANES = 128

GLA_HEADS_PER_STEP = 8
DELTA_QK_HEADS_PER_STEP = 4
RET_HEADS_PER_STEP = 2
MIXER_ROW_BLOCK = 1024
GLA_CHUNK = 64
DELTA_CHUNK = 64
RET_CHUNK = 256
DELTA_SUB = 16

_NN = (((1,), (0,)), ((), ()))
_NT = (((1,), (1,)), ((), ()))
_TN = (((0,), (0,)), ((), ()))


def _cparams(*sem):
    return pltpu.CompilerParams(dimension_semantics=sem, vmem_limit_bytes=VMEM_LIMIT_BYTES)


def _dg(a, b, dn=_NN):
    return lax.dot_general(a, b, dn, preferred_element_type=F32)


def _dot1(a, b, dn=_NN):
    return _dg(a.astype(BF16), b.astype(BF16), dn)


def _split3(x):
    x1 = x.astype(BF16)
    r1 = x - x1.astype(F32)
    x2 = r1.astype(BF16)
    x3 = (r1 - x2.astype(F32)).astype(BF16)
    return x1, x2, x3


def _sel_dot(m, x):
    x1, x2, x3 = _split3(x)
    return _dg(m, x1) + (_dg(m, x2) + _dg(m, x3))


def _sigmoid(x):
    return 1.0 / (1.0 + jnp.exp(-x))


def _silu(x):
    return x * _sigmoid(x)


def _softplus(x):
    return jnp.maximum(x, 0.0) + jnp.log1p(jnp.exp(-jnp.abs(x)))


def _pad_rows(x, rows):
    if x.shape[0] == rows:
        return x
    return jnp.concatenate([x, jnp.zeros((rows - x.shape[0],) + x.shape[1:], x.dtype)], axis=0)


def _onehot(cond):
    return jnp.where(cond, 1.0, 0.0).astype(BF16)


def _pick(n, prefs):
    for p in prefs:
        if n % p == 0:
            return p
    return n


def _rmsnorm_kernel(x_ref, g_ref, o_ref):
    x = x_ref[...]
    y = x * lax.rsqrt(jnp.mean(x * x, axis=-1, keepdims=True) + EPS)
    o_ref[...] = (y * g_ref[...]).astype(o_ref.dtype)


def _rmsnorm(x, g, out_dtype):
    m, d = x.shape
    tm = _pick(m, (256, 64, 8))
    return pl.pallas_call(
        _rmsnorm_kernel,
        grid=(m // tm,),
        in_specs=[pl.BlockSpec((tm, d), lambda i: (i, 0)), pl.BlockSpec((1, d), lambda i: (0, 0))],
        out_specs=pl.BlockSpec((tm, d), lambda i: (i, 0)),
        out_shape=jax.ShapeDtypeStruct((m, d), out_dtype),
        compiler_params=_cparams("parallel"),
        name="rmsnorm",
    )(x, g.reshape(1, d))


def _mm_kernel(a_ref, w_ref, o_ref):
    o_ref[...] = _dg(a_ref[...], w_ref[...]).astype(o_ref.dtype)


def _matmul(a, w, layer, out_dtype=F32):
    m, k = a.shape
    n = w.shape[2]
    tm = _pick(m, (1024, 512, 256, 64))
    tn = _pick(n, (512, 256, 128))
    return pl.pallas_call(
        _mm_kernel,
        grid=(m // tm, n // tn),
        in_specs=[pl.BlockSpec((tm, k), lambda i, j: (i, 0)),
                  pl.BlockSpec((None, k, tn), lambda i, j: (layer, 0, j))],
        out_specs=pl.BlockSpec((tm, tn), lambda i, j: (i, j)),
        out_shape=jax.ShapeDtypeStruct((m, n), out_dtype),
        compiler_params=_cparams("parallel", "arbitrary"),
        name="matmul",
    )(a, w)


def _mm_res_kernel(*refs, nparts):
    a_refs = refs[:nparts]
    w_refs = refs[nparts:2 * nparts]
    x_ref = refs[2 * nparts]
    o_ref = refs[2 * nparts + 1]
    acc = x_ref[...]
    for a_ref, w_ref in zip(a_refs, w_refs):
        acc = acc + _dg(a_ref[...], w_ref[...])
    o_ref[...] = acc


def _matmul_residual(parts, w, layer, x, tm_prefs, tn_prefs):
    m, n = x.shape
    kp = parts[0].shape[1]
    nparts = len(parts)
    tm = _pick(m, tm_prefs)
    tn = _pick(n, tn_prefs)

    def wmap(p):
        return lambda i, j: (layer, p, j)

    in_specs = [pl.BlockSpec((tm, kp), lambda i, j: (i, 0)) for _ in parts]
    in_specs += [pl.BlockSpec((None, kp, tn), wmap(p)) for p in range(nparts)]
    in_specs += [pl.BlockSpec((tm, tn), lambda i, j: (i, j))]
    return pl.pallas_call(
        functools.partial(_mm_res_kernel, nparts=nparts),
        grid=(m // tm, n // tn),
        in_specs=in_specs,
        out_specs=pl.BlockSpec((tm, tn), lambda i, j: (i, j)),
        out_shape=jax.ShapeDtypeStruct((m, n), F32),
        compiler_params=_cparams("parallel", "arbitrary"),
        name="matmul_residual",
    )(*parts, *([w] * nparts), x)


def _ple_kernel(a_ref, wg_ref, pe_ref, wp_ref, x_ref, o_ref):
    gate = _sigmoid(_dg(a_ref[...], wg_ref[...]))
    o_ref[...] = x_ref[...] + gate * _dg(pe_ref[...], wp_ref[...])


def _ple(hp, wg, pe, wp, layer, x):
    m, d = x.shape
    k = hp.shape[1]
    kp = pe.shape[1]
    tm = _pick(m, (1024, 512, 256, 64))
    tn = _pick(d, (512, 256, 128))
    return pl.pallas_call(
        _ple_kernel,
        grid=(m // tm, d // tn),
        in_specs=[pl.BlockSpec((tm, k), lambda i, j: (i, 0)),
                  pl.BlockSpec((None, k, tn), lambda i, j: (layer, 0, j)),
                  pl.BlockSpec((tm, kp), lambda i, j: (i, 0)),
                  pl.BlockSpec((None, kp, tn), lambda i, j: (layer, 0, j)),
                  pl.BlockSpec((tm, tn), lambda i, j: (i, j))],
        out_specs=pl.BlockSpec((tm, tn), lambda i, j: (i, j)),
        out_shape=jax.ShapeDtypeStruct((m, d), F32),
        compiler_params=_cparams("parallel", "arbitrary"),
        name="ple",
    )(hp, wg, pe, wp, x)


def _ffn_act(g, p1, p2, cw, u):
    gc = cw[0:1] * p2 + cw[1:2] * p1 + cw[2:3] * g
    return (_silu(gc) * u).astype(BF16)


def _ffn_gu_prompt_kernel(h_ref, halo_ref, wg_ref, wu_ref, cw_ref, o_ref, tail_ref, *, seq):
    tm = h_ref.shape[0]
    a = h_ref[...]
    g = _dg(a, wg_ref[...])
    u = _dg(a, wu_ref[...])
    gh = _dg(halo_ref[...], wg_ref[...])
    seq_start = lax.rem(pl.program_id(0) * tm, seq) == 0
    gh = jnp.where(seq_start, 0.0, gh)
    row = lax.broadcasted_iota(jnp.int32, g.shape, 0)
    p1 = jnp.where(row == 0, gh[7:8], pltpu.roll(g, 1, 0))
    p2 = jnp.where(row == 0, gh[6:7], jnp.where(row == 1, gh[7:8], pltpu.roll(g, 2, 0)))
    o_ref[...] = _ffn_act(g, p1, p2, cw_ref[...], u)
    tail_ref[0] = g[tm - SUBLANES:tm]


def _ffn_gate_up_prompt(hf, wg, wu, cw, layer, seq):
    m, d = hf.shape
    f = wg.shape[2]
    tm = _pick(seq, (1024, 512, 256, 128, 64, 8))
    tf = _pick(f, (256, 128))
    hb = tm // SUBLANES
    wmap = lambda i, j: (layer, 0, j)
    out, tail = pl.pallas_call(
        functools.partial(_ffn_gu_prompt_kernel, seq=seq),
        grid=(m // tm, f // tf),
        in_specs=[pl.BlockSpec((tm, d), lambda i, j: (i, 0)),
                  pl.BlockSpec((SUBLANES, d), lambda i, j: (jnp.maximum(i * hb - 1, 0), 0)),
                  pl.BlockSpec((None, d, tf), wmap),
                  pl.BlockSpec((None, d, tf), wmap),
                  pl.BlockSpec((None, FFN_CONV, tf), wmap)],
        out_specs=[pl.BlockSpec((tm, tf), lambda i, j: (i, j)),
                   pl.BlockSpec((1, SUBLANES, tf), lambda i, j: (i, 0, j))],
        out_shape=[jax.ShapeDtypeStruct((m, f), BF16),
                   jax.ShapeDtypeStruct((m // tm, SUBLANES, f), F32)],
        compiler_params=_cparams("parallel", "arbitrary"),
        name="ffn_gate_up_prompt",
    )(hf, hf, wg, wu, cw)
    per_seq = seq // tm
    tail = tail.reshape(m // seq, per_seq, SUBLANES, f)[:, per_seq - 1, SUBLANES - (FFN_CONV - 1):]
    return out, tail


def _ffn_gu_sample_kernel(h_ref, wg_ref, wu_ref, cw_ref, b1_ref, b2_ref, o_ref, g_ref, *, t):
    a = h_ref[...]
    g = _dg(a, wg_ref[...])
    u = _dg(a, wu_ref[...])
    pos = lax.rem(lax.broadcasted_iota(jnp.int32, g.shape, 0), t)
    p1 = jnp.where(pos == 0, b1_ref[...], pltpu.roll(g, 1, 0))
    p2 = jnp.where(pos < 2, b2_ref[...], pltpu.roll(g, 2, 0))
    o_ref[...] = _ffn_act(g, p1, p2, cw_ref[...], u)
    g_ref[...] = g


def _ffn_gate_up_sample(hf, wg, wu, cw, layer, buf, t):
    m, d = hf.shape
    f = wg.shape[2]
    nb = m // t
    tf = _pick(f, (256, 128))
    zeros = jnp.zeros((nb, t - 1, f), F32)
    b1 = jnp.concatenate([buf[:, 1:2], zeros], axis=1).reshape(m, f)
    b2 = jnp.concatenate([buf[:, 0:2], zeros[:, 1:]], axis=1).reshape(m, f)
    wmap = lambda j: (layer, 0, j)
    out, g = pl.pallas_call(
        functools.partial(_ffn_gu_sample_kernel, t=t),
        grid=(f // tf,),
        in_specs=[pl.BlockSpec((m, d), lambda j: (0, 0)),
                  pl.BlockSpec((None, d, tf), wmap),
                  pl.BlockSpec((None, d, tf), wmap),
                  pl.BlockSpec((None, FFN_CONV, tf), wmap),
                  pl.BlockSpec((m, tf), lambda j: (0, j)),
                  pl.BlockSpec((m, tf), lambda j: (0, j))],
        out_specs=[pl.BlockSpec((m, tf), lambda j: (0, j)), pl.BlockSpec((m, tf), lambda j: (0, j))],
        out_shape=[jax.ShapeDtypeStruct((m, f), BF16), jax.ShapeDtypeStruct((m, f), F32)],
        compiler_params=_cparams("arbitrary"),
        name="ffn_gate_up_sample",
    )(hf, wg, wu, cw, b1, b2)
    tail = g.reshape(nb, t, f)[:, t - (FFN_CONV - 1):]
    return out, tail


ATTN_HEADS_PER_STEP = 2


def _band_softmax_many(probs, dist_cf, dist_pf, valid_c):
    scs = [_dot1(pr[0], pr[1], _NT) for pr in probs]
    sps = [None if pr[2] is None else _dot1(pr[0], pr[2], _NT) for pr in probs]
    pcs, pps, ls, lses = [], [], [], []
    for pr, sc, sp in zip(probs, scs, sps):
        slope, prev_ok = pr[5], pr[6]
        sc = jnp.where(valid_c, sc - slope * dist_cf, NEG)
        if sp is None:
            m = jnp.max(sc, axis=-1, keepdims=True)
            pc = jnp.exp(sc - m)
            pp = None
            l = jnp.sum(pc, axis=-1, keepdims=True)
        else:
            sp = jnp.where(prev_ok, sp - slope * dist_pf, NEG)
            m = jnp.max(jnp.maximum(sc, sp), axis=-1, keepdims=True)
            pc = jnp.exp(sc - m)
            pp = jnp.exp(sp - m)
            l = jnp.sum(pc + pp, axis=-1, keepdims=True)
        pcs.append(pc)
        pps.append(pp)
        ls.append(l)
        lses.append(m + jnp.log(l))
    ocs = [_dot1(pc, pr[3]) for pc, pr in zip(pcs, probs)]
    ops = [None if pp is None else _dot1(pp, pr[4]) for pp, pr in zip(pps, probs)]
    outs = [(oc if op is None else oc + op) / l for oc, op, l in zip(ocs, ops, ls)]
    return outs, lses


def _band_kernel(*refs, s):
    nh = ATTN_HEADS_PER_STEP
    q_refs, k_refs, v_refs = refs[0:nh], refs[nh:2 * nh], refs[2 * nh:3 * nh]
    o_ref, o2_ref, o3_ref, l2_ref, l3_ref = refs[3 * nh:]
    nq = A_BLOCK
    hg = pl.program_id(1)
    qi = lax.broadcasted_iota(jnp.int32, (nq, nq), 0)
    kj = lax.broadcasted_iota(jnp.int32, (nq, nq), 1)
    dist_c = qi - kj
    dist_p = dist_c + nq
    valid_c = dist_c >= 0
    dist_cf = dist_c.astype(F32)
    dist_pf = dist_p.astype(F32)
    (w1, d1), (w2, d2), (w3, d3) = A_BRANCHES
    assert d1 == 1 and s % (d2 * nq) == 0 and s == d3 * nq and max(w1 // d1, w2 // d2, w3 // d3) <= nq

    def problem(h, rows_c, rows_p, has_prev, band, dil):
        hv = jnp.full((1, 1), hg * nh + h + 1, jnp.int32).astype(F32)
        slope = float(dil) * jnp.exp2(-8.0 * hv / H_A)
        q = q_refs[h][0, rows_c, :] * HD_A ** -0.5
        if rows_p is None:
            return (q, k_refs[h][0, rows_c, :], None, v_refs[h][0, rows_c, :], None, slope, None)
        return (q, k_refs[h][0, rows_c, :], k_refs[h][0, rows_p, :], v_refs[h][0, rows_c, :],
                v_refs[h][0, rows_p, :], slope, (dist_p <= band) & has_prev)

    def dilated(i, carry):
        rows3 = pl.ds(i, nq, stride=d3)
        r = lax.rem(i, d2)
        b = i // d2
        rows2 = pl.ds(b * (d2 * nq) + r, nq, stride=d2)
        rows2p = pl.ds(jnp.maximum(b - 1, 0) * (d2 * nq) + r, nq, stride=d2)
        probs = [problem(h, rows3, None, False, w3 // d3, d3) for h in range(nh)]
        probs += [problem(h, rows2, rows2p, b > 0, w2 // d2, d2) for h in range(nh)]
        outs, lses = _band_softmax_many(probs, dist_cf, dist_pf, valid_c)
        for h in range(nh):
            o3_ref[h, rows3, :] = outs[h]
            l3_ref[h, rows3, :] = jnp.broadcast_to(lses[h], (nq, HD_A))
            o2_ref[h, rows2, :] = outs[nh + h]
            l2_ref[h, rows2, :] = jnp.broadcast_to(lses[nh + h], (nq, HD_A))
        return carry

    lax.fori_loop(0, s // nq, dilated, 0)

    def dense(i, carry):
        blocks = []
        for j in range(2):
            bi = 2 * i + j
            rows = pl.ds(pl.multiple_of(bi * nq, nq), nq)
            rows_p = pl.ds(pl.multiple_of(jnp.maximum(bi - 1, 0) * nq, nq), nq)
            blocks += [(h, rows, problem(h, rows, rows_p, bi > 0, w1 // d1, d1)) for h in range(nh)]
        outs, lses = _band_softmax_many([blk[2] for blk in blocks], dist_cf, dist_pf, valid_c)
        for (h, rows, _), o1, l1 in zip(blocks, outs, lses):
            l2 = l2_ref[h, rows, :]
            l3 = l3_ref[h, rows, :]
            mx = jnp.maximum(l1, jnp.maximum(l2, l3))
            e1 = jnp.exp(l1 - mx)
            e2 = jnp.exp(l2 - mx)
            e3 = jnp.exp(l3 - mx)
            o = (e1 * o1 + e2 * o2_ref[h, rows, :] + e3 * o3_ref[h, rows, :]) / (e1 + e2 + e3)
            o_ref[0, rows, h * HD_A:(h + 1) * HD_A] = o.astype(o_ref.dtype)
        return carry

    assert (s // nq) % 2 == 0
    lax.fori_loop(0, s // (2 * nq), dense, 0)


def _attention_prompt(za):
    b, s, _ = za.shape
    nh = ATTN_HEADS_PER_STEP
    ng = H_A // nh
    blk = (1, s, HD_A)

    def head_spec(first, h):
        return pl.BlockSpec(blk, lambda bi, g: (bi, 0, first + g * nh + h))

    in_specs = [head_spec(part * H_A, h) for part in range(3) for h in range(nh)]
    return pl.pallas_call(
        functools.partial(_band_kernel, s=s),
        grid=(b, ng),
        in_specs=in_specs,
        out_specs=pl.BlockSpec((1, s, nh * HD_A), lambda bi, g: (bi, 0, g)),
        out_shape=jax.ShapeDtypeStruct((b, s, W_A), BF16),
        scratch_shapes=[pltpu.VMEM((nh, s, HD_A), F32) for _ in range(4)],
        compiler_params=_cparams("parallel", "arbitrary"),
        name="band_attention",
    )(*([za] * (3 * nh)))


def _attn_sample_kernel(q_ref, kn_ref, vn_ref, c_ref, o_ref):
    t = q_ref.shape[1]
    rows_per_pos = 2 * H_A
    n_past = c_ref.shape[0] // rows_per_pos
    tp = 2 * SUBLANES
    shape_c = (tp, n_past)
    shape_n = (tp, tp)
    dist_c = n_past + lax.broadcasted_iota(jnp.int32, shape_c, 0) - lax.broadcasted_iota(jnp.int32, shape_c, 1)
    dist_n = lax.broadcasted_iota(jnp.int32, shape_n, 0) - lax.broadcasted_iota(jnp.int32, shape_n, 1)
    new_ok = lax.broadcasted_iota(jnp.int32, shape_n, 1) < t
    dist_cf = dist_c.astype(F32)
    dist_nf = dist_n.astype(F32)
    oks = []
    for window, dil in A_BRANCHES:
        oks.append(((dist_c <= window) & (lax.rem(dist_c, dil) == 0),
                    (dist_n >= 0) & (dist_n <= window) & (lax.rem(dist_n, dil) == 0) & new_ok))
    hs = range(H_A)
    slopes = [2.0 ** (-8.0 * (h + 1) / H_A) for h in hs]
    sls = [slice(h * HD_A, (h + 1) * HD_A) for h in hs]
    qs = [_pad_rows(q_ref[0, :, sl] * HD_A ** -0.5, tp).astype(BF16) for sl in sls]
    kns = [_pad_rows(kn_ref[0, :, sl], tp).astype(BF16) for sl in sls]
    vns = [_pad_rows(vn_ref[0, :, sl], tp).astype(BF16) for sl in sls]
    kcs = [c_ref[pl.ds(h, n_past, stride=rows_per_pos), :].astype(BF16) for h in hs]
    vcs = [c_ref[pl.ds(H_A + h, n_past, stride=rows_per_pos), :].astype(BF16) for h in hs]
    b_cs = [_dg(qs[h], kcs[h], _NT) - slopes[h] * dist_cf for h in hs]
    b_ns = [_dg(qs[h], kns[h], _NT) - slopes[h] * dist_nf for h in hs]
    pcs, pns, ls, lses = [], [], [], []
    for h in hs:
        for ok_c, ok_n in oks:
            sc = jnp.where(ok_c, b_cs[h], NEG)
            sn = jnp.where(ok_n, b_ns[h], NEG)
            m = jnp.maximum(jnp.max(sc, axis=-1, keepdims=True), jnp.max(sn, axis=-1, keepdims=True))
            pc = jnp.exp(sc - m)
            pn = jnp.exp(sn - m)
            pcs.append(pc.astype(BF16))
            pns.append(pn.astype(BF16))
            ls.append(jnp.sum(pc, axis=-1, keepdims=True) + jnp.sum(pn, axis=-1, keepdims=True))
            lses.append(m + jnp.log(ls[-1]))
    nbr = len(oks)
    outs = [(_dg(pcs[i], vcs[i // nbr]) + _dg(pns[i], vns[i // nbr])) / ls[i] for i in range(len(pcs))]
    for h in hs:
        lse_h = lses[h * nbr:(h + 1) * nbr]
        out_h = outs[h * nbr:(h + 1) * nbr]
        mx = functools.reduce(jnp.maximum, lse_h)
        es = [jnp.exp(x - mx) for x in lse_h]
        tot = functools.reduce(lambda a, b: a + b, es)
        o = functools.reduce(lambda a, b: a + b, [e * x for e, x in zip(es, out_h)]) / tot
        o_ref[0, :, sls[h]] = o[:t].astype(o_ref.dtype)


def _attention_sample(za, cache, layer):
    b, t, _ = za.shape
    depth, _, l = cache.shape[:3]
    rows = l * 2 * H_A
    cache2d = cache.reshape(depth, b, rows, HD_A)
    return pl.pallas_call(
        _attn_sample_kernel,
        grid=(b,),
        in_specs=[pl.BlockSpec((1, t, W_A), lambda bi: (bi, 0, 0)),
                  pl.BlockSpec((1, t, W_A), lambda bi: (bi, 0, 1)),
                  pl.BlockSpec((1, t, W_A), lambda bi: (bi, 0, 2)),
                  pl.BlockSpec((None, None, rows, HD_A), lambda bi: (layer, bi, 0, 0))],
        out_specs=pl.BlockSpec((1, t, W_A), lambda bi: (bi, 0, 0)),
        out_shape=jax.ShapeDtypeStruct((b, t, W_A), BF16),
        compiler_params=_cparams("parallel"),
        name="attention_sample",
    )(za, za, za, cache2d)


def _cast_kernel(x_ref, o_ref):
    o_ref[...] = x_ref[...].astype(o_ref.dtype)


CAST_BLOCK_BYTES = 6 * 1024 * 1024


def _cast_bf16(w, col0=0, ncols=None):
    depth, k, n = w.shape
    ncols = n - col0 if ncols is None else ncols
    tn = next(c for c in (1024, 512, 256, LANES) if ncols % c == 0 and col0 % c == 0)
    packed_rows = 2 * SUBLANES
    tk = max(d for d in range(packed_rows, k + 1, packed_rows) if k % d == 0 and d * tn * 4 <= CAST_BLOCK_BYTES)
    c0 = col0 // tn
    return pl.pallas_call(
        _cast_kernel,
        grid=(depth, k // tk, ncols // tn),
        in_specs=[pl.BlockSpec((1, tk, tn), lambda l, i, j: (l, i, c0 + j))],
        out_specs=pl.BlockSpec((1, tk, tn), lambda l, i, j: (l, i, j)),
        out_shape=jax.ShapeDtypeStruct((depth, k, ncols), BF16),
        compiler_params=_cparams("parallel", "parallel", "parallel"),
        name="cast_bf16",
    )(w)


def _cast_tail_kernel(a_ref, b_ref, d_ref, g_ref, *, shift):
    a = a_ref[0]
    tn = a.shape[1]
    x = jnp.concatenate([a, b_ref[0]], axis=1)
    d_ref[0] = x[:, shift:shift + tn].astype(d_ref.dtype)

    @pl.when(pl.program_id(2) == 0)
    def _():
        lane = lax.broadcasted_iota(jnp.int32, (a.shape[0], LANES), 1)
        g_ref[0] = jnp.where(lane < shift, a[:, :LANES], 0.0).astype(g_ref.dtype)


def _cast_tail(w, col0, shift, tn=512):
    depth, k, n = w.shape
    ncols = n - col0 - shift
    assert col0 % tn == 0 and ncols % tn == 0 and 0 < shift < LANES
    tk = 1024 if k % 1024 == 0 else k
    c0 = col0 // tn
    per = tn // LANES
    return pl.pallas_call(
        functools.partial(_cast_tail_kernel, shift=shift),
        grid=(depth, k // tk, ncols // tn),
        in_specs=[pl.BlockSpec((1, tk, tn), lambda l, i, j: (l, i, c0 + j)),
                  pl.BlockSpec((1, tk, LANES), lambda l, i, j: (l, i, (c0 + j + 1) * per))],
        out_specs=[pl.BlockSpec((1, tk, tn), lambda l, i, j: (l, i, j)),
                   pl.BlockSpec((1, tk, LANES), lambda l, i, j: (l, i, 0))],
        out_shape=[jax.ShapeDtypeStruct((depth, k, ncols), BF16), jax.ShapeDtypeStruct((depth, k, LANES), BF16)],
        compiler_params=_cparams("parallel", "parallel", "arbitrary"),
        name="cast_tail",
    )(w, w)


def _kv_pack_kernel(*refs):
    o_ref = refs[-1]
    depth = (len(refs) - 1) // 2
    w = refs[0].shape[2]
    for l in range(depth):
        @pl.when(pl.program_id(0) == l)
        def _(l=l):
            o_ref[0, :, 0:w] = refs[2 * l][0]
            o_ref[0, :, w:2 * w] = refs[2 * l + 1][0]


def _kv_pack(zas, rows):
    depth = len(zas)
    b, s, _ = zas[0].shape
    ts = _pick(rows, (1024, 512, 256, 128, 64, 8))
    first = (s - rows) // ts
    assert (s - rows) % ts == 0
    in_specs, args = [], []
    for l in range(depth):
        for part in (1, 2):
            def imap(d, bi, si, l=l, part=part):
                return (bi, jnp.where(d == l, first + si, first), part)
            in_specs.append(pl.BlockSpec((1, ts, W_A), imap))
            args.append(zas[l])
    out = pl.pallas_call(
        _kv_pack_kernel,
        grid=(depth, b, rows // ts),
        in_specs=in_specs,
        out_specs=pl.BlockSpec((None, 1, ts, 2 * W_A), lambda d, bi, si: (d, bi, si, 0)),
        out_shape=jax.ShapeDtypeStruct((depth, b, rows, 2 * W_A), F32),
        compiler_params=_cparams("parallel", "parallel", "parallel"),
        name="kv_pack",
    )(*args)
    return out.reshape(depth, b, rows, 2, H_A, HD_A)


def _kv_shift_kernel(c_ref, n_ref, o_ref):
    l = c_ref.shape[0]
    t = n_ref.shape[0]
    o_ref[0:l - t] = c_ref[t:l]
    o_ref[l - t:l] = n_ref[...]


def _kv_shift(cache, kv_new):
    depth, b, l, two, h, hd = cache.shape
    t = kv_new.shape[2]
    cblk = (None, None, l, None, h, hd)
    cmap = lambda d, bi, s: (d, bi, 0, s, 0, 0)
    return pl.pallas_call(
        _kv_shift_kernel,
        grid=(depth, b, two),
        in_specs=[pl.BlockSpec(cblk, cmap), pl.BlockSpec((None, None, t, None, h, hd), cmap)],
        out_specs=pl.BlockSpec(cblk, cmap),
        out_shape=jax.ShapeDtypeStruct(cache.shape, cache.dtype),
        compiler_params=_cparams("parallel", "parallel", "parallel"),
        name="kv_shift",
    )(cache, kv_new)


def _chunk_plan(t, chunk):
    c = chunk if t >= chunk else max(2 * SUBLANES, t)
    assert t % c == 0 or t < c
    tc = min(t, c)
    return c, tc, max(t // c, 1)


def _run_chunks(nchunks, c, chunk):
    if nchunks == 1:
        chunk(0, 0)
    else:
        def body(ci, carry):
            chunk(ci, pl.multiple_of(ci * c, c))
            return carry
        lax.fori_loop(0, nchunks, body, 0)


def _gla_kernel(q_ref, f_ref, i_ref, g_ref, lb_ref, nrm_ref, s0_ref, o_ref, sfin_ref, st_ref, *, layer, c, tc, nchunks):
    nlev = int(math.log2(c))
    assert 1 << nlev == c
    depth = lb_ref.shape[0]
    nh = st_ref.shape[0]
    lbs = [lb_ref[i] for i in range(depth)]
    mx = functools.reduce(jnp.maximum, lbs)
    es = [jnp.exp(x - mx) for x in lbs]
    tot = functools.reduce(lambda a, b: a + b, es)
    lower = functools.reduce(lambda a, b: a + b, [es[i] / tot for i in range(layer + 1)]) - es[0] / tot
    one_minus_lb = 1.0 - lower

    row = lax.broadcasted_iota(jnp.int32, (c, c), 0)
    col = lax.broadcasted_iota(jnp.int32, (c, c), 1)
    prefix = [_onehot(col <= row)]
    lmask, second = [], []
    row_hd = lax.broadcasted_iota(jnp.int32, (c, HD_B), 0)
    for lev in range(1, nlev + 1):
        s = c >> lev
        grp = ~(2 * s - 1)
        prefix.append(_onehot(col <= (row & grp) + (s - 1)))
        lmask.append((((row ^ col) & grp) == 0) & ((row & s) != 0) & ((col & s) == 0))
        second.append((row_hd & s) != 0)
    mstack = jnp.concatenate(prefix, axis=0)
    eye = row == col
    valid = lax.broadcasted_iota(jnp.int32, (c, 1), 0) < tc
    nrm = nrm_ref[...]

    @pl.when(pl.program_id(2) == 0)
    def _():
        for h in range(nh):
            st_ref[h] = s0_ref[0, h].T

    def chunk(ci, t0):
        rows = pl.ds(t0, tc)
        hs = range(nh)
        sls = [slice(h * HD_B, (h + 1) * HD_B) for h in hs]
        qs = [_silu(_pad_rows(q_ref[0, rows, sl], c)) for sl in sls]
        kbs = [one_minus_lb[:, sl] * _sigmoid(-_pad_rows(f_ref[0, rows, sl], c)) for sl in sls]
        logfs = [jnp.log1p(-kb) for kb in kbs]
        if tc < c:
            kbs = [jnp.where(valid, kb, 0.0) for kb in kbs]
            logfs = [jnp.where(valid, lf, 0.0) for lf in logfs]
        vs = [_pad_rows(i_ref[0, rows, sl], c).astype(BF16) for sl in sls]
        bs_all = _sel_dot(mstack, jnp.concatenate(logfs, axis=1))
        bss = [bs_all[:, sl] for sl in sls]
        b = [bs[0:c] for bs in bss]
        atts = [jnp.where(eye, jnp.sum(q * kb, axis=-1, keepdims=True), 0.0) for q, kb in zip(qs, kbs)]
        for lev in range(1, nlev + 1):
            brs = [bs[lev * c:(lev + 1) * c] for bs in bss]
            ws = [(jnp.where(second[lev - 1], qs[h], kbs[h]) * jnp.exp(-jnp.abs(b[h] - brs[h]))).astype(BF16)
                  for h in hs]
            prods = [_dg(w, w, _NT) for w in ws]
            atts = [att + jnp.where(lmask[lev - 1], pr, 0.0) for att, pr in zip(atts, prods)]
        sts = [st_ref[h] for h in hs]
        o_inter = [_dot1(qs[h] * jnp.exp(b[h]), sts[h], _NT) for h in hs]
        o_intra = [_dg(atts[h].astype(BF16), vs[h]) for h in hs]
        bends = [b[h][c - 1:c] for h in hs]
        upd = [_dg(vs[h], (kbs[h] * jnp.exp(bends[h] - b[h])).astype(BF16), _TN) for h in hs]
        for h in hs:
            st_ref[h] = sts[h] * jnp.exp(bends[h]) + upd[h]
            o = o_inter[h] + o_intra[h]
            on = o * lax.rsqrt(jnp.mean(o * o, axis=-1, keepdims=True) + EPS) * nrm
            out = on * _silu(_pad_rows(g_ref[0, rows, sls[h]], c))
            o_ref[0, rows, sls[h]] = out[0:tc].astype(o_ref.dtype)

    _run_chunks(nchunks, c, chunk)

    @pl.when(pl.program_id(2) == pl.num_programs(2) - 1)
    def _():
        for h in range(nh):
            sfin_ref[0, h] = st_ref[h].T


def _hgrn(zb, lb, nrm, s0, layer, chunk):
    b, t, _ = zb.shape
    tb = min(t, MIXER_ROW_BLOCK)
    assert t % tb == 0
    c, tc, nchunks = _chunk_plan(tb, chunk)
    depth = lb.shape[0]
    nh = GLA_HEADS_PER_STEP
    ng = H_B // nh
    blk = (1, tb, nh * HD_B)
    sblk = (1, nh, HD_B, HD_B)
    return pl.pallas_call(
        functools.partial(_gla_kernel, layer=layer, c=c, tc=tc, nchunks=nchunks),
        grid=(b, ng, t // tb),
        in_specs=[pl.BlockSpec(blk, lambda bi, g, ti: (bi, ti, g)),
                  pl.BlockSpec(blk, lambda bi, g, ti: (bi, ti, ng + g)),
                  pl.BlockSpec(blk, lambda bi, g, ti: (bi, ti, 2 * ng + g)),
                  pl.BlockSpec(blk, lambda bi, g, ti: (bi, ti, 3 * ng + g)),
                  pl.BlockSpec((depth, 1, nh * HD_B), lambda bi, g, ti: (0, 0, g)),
                  pl.BlockSpec((1, HD_B), lambda bi, g, ti: (0, 0)),
                  pl.BlockSpec(sblk, lambda bi, g, ti: (bi, g, 0, 0))],
        out_specs=[pl.BlockSpec(blk, lambda bi, g, ti: (bi, ti, g)),
                   pl.BlockSpec(sblk, lambda bi, g, ti: (bi, g, 0, 0))],
        out_shape=[jax.ShapeDtypeStruct((b, t, W_B), BF16), jax.ShapeDtypeStruct((b, H_B, HD_B, HD_B), F32)],
        scratch_shapes=[pltpu.VMEM((nh, HD_B, HD_B), F32)],
        compiler_params=_cparams("parallel", "parallel", "arbitrary"),
        name="hgrn2",
    )(zb, zb, zb, zb, lb.reshape(depth, 1, W_B), nrm.reshape(1, HD_B), s0)


def _delta_kernel(q_ref, k_ref, v_ref, z_ref, zg_ref, wq_ref, wk_ref, wv_ref, bq_ref, bk_ref, bv_ref,
                  hq_ref, hk_ref, hv_ref, al_ref, dl_ref, nrm_ref, s0_ref, o_ref, sfin_ref, st_ref,
                  *, c, tc, nchunks):
    g = pl.program_id(1)
    first_block = pl.program_id(2) == 0
    nqk = DELTA_QK_HEADS_PER_STEP
    rep = H_C_V // H_C_QK
    sb = min(DELTA_SUB, c)
    row = lax.broadcasted_iota(jnp.int32, (c, c), 0)
    col = lax.broadcasted_iota(jnp.int32, (c, c), 1)
    lower_incl = _onehot(col <= row)
    eye = jnp.where(row == col, 1.0, 0.0)
    same_sub = ((row ^ col) & ~(sb - 1)) == 0
    valid = lax.broadcasted_iota(jnp.int32, (c, 1), 0) < tc
    lane = lax.broadcasted_iota(jnp.int32, (c, LANES), 1)
    nrm = nrm_ref[...]

    @pl.when(first_block)
    def _():
        for h in range(nqk * rep):
            st_ref[h] = s0_ref[0, h].T

    def conv(x_ref, w_ref, buf_ref, halo_ref, sl, ci, t0):
        x = x_ref[0, pl.ds(t0, tc), sl]
        halo = jnp.where(first_block, buf_ref[0, :, sl], halo_ref[0, :, sl])
        if nchunks > 1:
            prev = x_ref[0, pl.ds(pl.multiple_of(jnp.maximum(t0 - SUBLANES, 0), SUBLANES), SUBLANES), sl]
            halo = jnp.where(ci == 0, halo, prev)
        xh = jnp.concatenate([halo, x], axis=0)
        w = w_ref[:, sl]
        y = w[C_CONV - 1:C_CONV] * x
        for s in range(1, C_CONV):
            y = y + w[C_CONV - 1 - s:C_CONV - s] * pltpu.roll(xh, s, 0)[SUBLANES:]
        return _pad_rows(_silu(y), c)

    def l2n(x):
        return x * lax.rsqrt(jnp.sum(x * x, axis=-1, keepdims=True) + EPS)

    def chunk(ci, t0):
        rows = pl.ds(t0, tc)
        zg = _pad_rows(zg_ref[0, rows, :], c)
        beta_all = _sigmoid(zg)
        la_all = -jnp.exp(al_ref[...]) * _softplus(zg + dl_ref[...])
        nv = nqk * rep
        hs = range(nv)
        qsls = [slice(qh * HD_C, (qh + 1) * HD_C) for qh in range(nqk)]
        vsls = [slice(h * HD_C, (h + 1) * HD_C) for h in hs]
        qs = [l2n(conv(q_ref, wq_ref, bq_ref, hq_ref, sl, ci, t0)) * HD_C ** -0.5 for sl in qsls]
        ks = [l2n(conv(k_ref, wk_ref, bk_ref, hk_ref, sl, ci, t0)) for sl in qsls]
        if tc < c:
            ks = [jnp.where(valid, k, 0.0) for k in ks]
        qbs = [q.astype(BF16) for q in qs]
        kbs = [k.astype(BF16) for k in ks]
        kks = [_dg(kb, kb, _NT) for kb in kbs]
        qks = [_dg(qb, kb, _NT) for qb, kb in zip(qbs, kbs)]
        vs = [conv(v_ref, wv_ref, bv_ref, hv_ref, sl, ci, t0) for sl in vsls]
        betas = [jnp.sum(jnp.where(lane == g * nv + h, beta_all, 0.0), axis=-1, keepdims=True) for h in hs]
        las = [jnp.sum(jnp.where(lane == H_C_V + g * nv + h, la_all, 0.0), axis=-1, keepdims=True) for h in hs]
        if tc < c:
            betas = [jnp.where(valid, x, 0.0) for x in betas]
            las = [jnp.where(valid, x, 0.0) for x in las]
        bc_all = _sel_dot(lower_incl, jnp.concatenate([jnp.broadcast_to(la, (c, LANES)) for la in las], axis=1))
        bcols = [bc_all[:, h * LANES:h * LANES + c] for h in hs]
        decs = [jnp.exp(jnp.where(row >= col, bc - bc.T, NEG)) for bc in bcols]
        b1s = [bc[:, 0:1] for bc in bcols]
        ebs = [jnp.exp(b1) for b1 in b1s]
        bends = [b1[c - 1:c] for b1 in b1s]
        ns = [jnp.where(row > col, betas[h] * (kks[h // rep] * decs[h]), 0.0) for h in hs]
        nds = [jnp.where(same_sub, n, 0.0) for n in ns]
        xs = [eye - nd for nd in nds]
        ps = nds
        for _ in range(int(math.log2(sb)) - 1):
            ps = [_dot1(p, p) for p in ps]
            xs = [x + _dot1(x, p) for x, p in zip(xs, ps)]
        nblk = c // sb
        if nblk > 1:
            mms = [_dot1(x, n - nd) for x, n, nd in zip(xs, ns, nds)]
            ys = [eye - mm for mm in mms]
            ps = mms
            for _ in range(int(math.log2(nblk)) - 1):
                ps = [_dot1(p, p) for p in ps]
                ys = [y + _dot1(y, p) for y, p in zip(ys, ps)]
            xs = [_dot1(y, x) for y, x in zip(ys, xs)]
        sts = [st_ref[h] for h in hs]
        stbs = [st.astype(BF16) for st in sts]
        ksts = [_dg(kbs[h // rep], stbs[h], _NT) for h in hs]
        qsts = [_dg(qbs[h // rep], stbs[h], _NT) for h in hs]
        rhss = [betas[h] * (vs[h] - ebs[h] * ksts[h]) for h in hs]
        ubs = [_dot1(x, rhs).astype(BF16) for x, rhs in zip(xs, rhss)]
        o_intra = [_dg((qks[h // rep] * decs[h]).astype(BF16), ubs[h]) for h in hs]
        upd = [_dg(ubs[h], (ks[h // rep] * jnp.exp(bends[h] - b1s[h])).astype(BF16), _TN) for h in hs]
        for h in hs:
            st_ref[h] = jnp.exp(bends[h]) * sts[h] + upd[h]
            o = ebs[h] * qsts[h] + o_intra[h]
            on = o * lax.rsqrt(jnp.mean(o * o, axis=-1, keepdims=True) + EPS) * nrm
            out = on * _silu(_pad_rows(z_ref[0, rows, vsls[h]], c))
            o_ref[0, rows, vsls[h]] = out[0:tc].astype(o_ref.dtype)

    _run_chunks(nchunks, c, chunk)

    @pl.when(pl.program_id(2) == pl.num_programs(2) - 1)
    def _():
        for h in range(nqk * rep):
            sfin_ref[0, h] = st_ref[h].T


def _delta(zc, zg, conv_w, buf, a_log, dt_bias, nrm, s0, chunk):
    b, t, _ = zc.shape
    tb = min(t, MIXER_ROW_BLOCK)
    assert t % tb == 0 and tb % SUBLANES == 0
    c, tc, nchunks = _chunk_plan(tb, chunk)
    rep = H_C_V // H_C_QK
    nqk = DELTA_QK_HEADS_PER_STEP
    nv = nqk * rep
    ng = H_C_QK // nqk
    qw = nqk * HD_C
    vw = nv * HD_C
    assert W_CQK % qw == 0 and (2 * W_CQK) % vw == 0
    kq0 = W_CQK // qw
    v0 = 2 * W_CQK // vw
    z0 = v0 + W_CV // vw
    buf8 = jnp.pad(buf, ((0, 0), (SUBLANES - (C_CONV - 1), 0), (0, 0)))
    pad_l = jnp.zeros((H_C_V,), F32)
    a_lane = jnp.pad(jnp.concatenate([pad_l, a_log]), (0, LANES - 2 * H_C_V)).reshape(1, LANES)
    d_lane = jnp.pad(jnp.concatenate([pad_l, dt_bias]), (0, LANES - 2 * H_C_V)).reshape(1, LANES)
    hb = tb // SUBLANES

    def rows(col0):
        return lambda bi, g, ti: (bi, ti, col0 + g)

    def first(col0):
        return lambda bi, g, ti: (bi, 0, col0 + g)

    def halo(col0):
        return lambda bi, g, ti: (bi, jnp.maximum(ti * hb - 1, 0), col0 + g)

    full = lambda bi, g, ti: (0, 0)
    sblk = (1, nv, HD_C, HD_C)
    smap = lambda bi, g, ti: (bi, g, 0, 0)
    return pl.pallas_call(
        functools.partial(_delta_kernel, c=c, tc=tc, nchunks=nchunks),
        grid=(b, ng, t // tb),
        in_specs=[pl.BlockSpec((1, tb, qw), rows(0)), pl.BlockSpec((1, tb, qw), rows(kq0)),
                  pl.BlockSpec((1, tb, vw), rows(v0)), pl.BlockSpec((1, tb, vw), rows(z0)),
                  pl.BlockSpec((1, tb, W_CG), lambda bi, g, ti: (bi, ti, 0)),
                  pl.BlockSpec((C_CONV, qw), lambda bi, g, ti: (0, g)),
                  pl.BlockSpec((C_CONV, qw), lambda bi, g, ti: (0, kq0 + g)),
                  pl.BlockSpec((C_CONV, vw), lambda bi, g, ti: (0, v0 + g)),
                  pl.BlockSpec((1, SUBLANES, qw), first(0)), pl.BlockSpec((1, SUBLANES, qw), first(kq0)),
                  pl.BlockSpec((1, SUBLANES, vw), first(v0)),
                  pl.BlockSpec((1, SUBLANES, qw), halo(0)), pl.BlockSpec((1, SUBLANES, qw), halo(kq0)),
                  pl.BlockSpec((1, SUBLANES, vw), halo(v0)),
                  pl.BlockSpec((1, LANES), full), pl.BlockSpec((1, LANES), full),
                  pl.BlockSpec((1, HD_C), full),
                  pl.BlockSpec(sblk, smap)],
        out_specs=[pl.BlockSpec((1, tb, vw), rows(0)), pl.BlockSpec(sblk, smap)],
        out_shape=[jax.ShapeDtypeStruct((b, t, W_CV), BF16), jax.ShapeDtypeStruct((b, H_C_V, HD_C, HD_C), F32)],
        scratch_shapes=[pltpu.VMEM((nv, HD_C, HD_C), F32)],
        compiler_params=_cparams("parallel", "parallel", "arbitrary"),
        name="gated_deltanet",
    )(zc, zc, zc, zc, zg, conv_w, conv_w, conv_w, buf8, buf8, buf8, zc, zc, zc,
      a_lane, d_lane, nrm.reshape(1, HD_C), s0)


def _ret_kernel(q_ref, k_ref, v_ref, g_ref, nrm_ref, s0_ref, o_ref, sfin_ref, st_ref, *, c, tc, nchunks):
    nh = st_ref.shape[0]
    row = lax.broadcasted_iota(jnp.int32, (c, c), 0)
    col = lax.broadcasted_iota(jnp.int32, (c, c), 1)
    steps = (jnp.minimum(row + 1, tc) - jnp.minimum(col + 1, tc)).astype(F32)
    r1 = lax.broadcasted_iota(jnp.int32, (c, 1), 0)
    nsteps = jnp.minimum(r1 + 1, tc).astype(F32)
    valid = r1 < tc
    nrm = nrm_ref[...]
    lgs, decs = [], []
    for h in range(nh):
        hf = jnp.full((1, 1), pl.program_id(1) * nh + h, jnp.int32).astype(F32)
        lg = jnp.log1p(-jnp.exp2(-5.0 - hf))
        lgs.append(lg)
        decs.append(jnp.exp(jnp.where(row >= col, steps * lg, NEG)))
        st_ref[h] = s0_ref[0, h].T

    def chunk(ci, t0):
        rows = pl.ds(t0, tc)
        hs = range(nh)
        sls = [slice(h * HD_D, (h + 1) * HD_D) for h in hs]
        b1s = [nsteps * lgs[h] for h in hs]
        bends = [float(tc) * lgs[h] for h in hs]
        qs = [_pad_rows(q_ref[0, rows, sl], c).astype(BF16) for sl in sls]
        ks = [_pad_rows(k_ref[0, rows, sl], c) * HD_D ** -0.5 for sl in sls]
        if tc < c:
            ks = [jnp.where(valid, k, 0.0) for k in ks]
        vs = [_pad_rows(v_ref[0, rows, sl], c).astype(BF16) for sl in sls]
        sts = [st_ref[h] for h in hs]
        atts = [_dg(qs[h], ks[h].astype(BF16), _NT) * decs[h] for h in hs]
        o_inter = [_dg(qs[h], sts[h].astype(BF16), _NT) for h in hs]
        o_intra = [_dg(atts[h].astype(BF16), vs[h]) for h in hs]
        upd = [_dg(vs[h], (ks[h] * jnp.exp(bends[h] - b1s[h])).astype(BF16), _TN) for h in hs]
        for h in hs:
            sl = sls[h]
            st_ref[h] = jnp.exp(bends[h]) * sts[h] + upd[h]
            o = jnp.exp(b1s[h]) * o_inter[h] + o_intra[h]
            mu = jnp.mean(o, axis=-1, keepdims=True)
            oc = o - mu
            var = jnp.mean(oc * oc, axis=-1, keepdims=True)
            on = oc * lax.rsqrt(var + EPS) * nrm
            out = on * _silu(_pad_rows(g_ref[0, rows, sl], c))
            o_ref[0, rows, sl] = out[0:tc].astype(o_ref.dtype)

    _run_chunks(nchunks, c, chunk)
    for h in range(nh):
        sfin_ref[0, h] = st_ref[h].T


def _retention(zd, nrm, s0, chunk):
    b, t, _ = zd.shape
    c, tc, nchunks = _chunk_plan(t, chunk)
    nh = RET_HEADS_PER_STEP
    ng = H_D // nh
    blk = (1, t, nh * HD_D)
    sblk = (1, nh, HD_D, HD_D)
    return pl.pallas_call(
        functools.partial(_ret_kernel, c=c, tc=tc, nchunks=nchunks),
        grid=(b, ng),
        in_specs=[pl.BlockSpec(blk, lambda bi, g: (bi, 0, g)),
                  pl.BlockSpec(blk, lambda bi, g: (bi, 0, ng + g)),
                  pl.BlockSpec(blk, lambda bi, g: (bi, 0, 2 * ng + g)),
                  pl.BlockSpec(blk, lambda bi, g: (bi, 0, 3 * ng + g)),
                  pl.BlockSpec((1, HD_D), lambda bi, g: (0, 0)),
                  pl.BlockSpec(sblk, lambda bi, g: (bi, g, 0, 0))],
        out_specs=[pl.BlockSpec(blk, lambda bi, g: (bi, 0, g)),
                   pl.BlockSpec(sblk, lambda bi, g: (bi, g, 0, 0))],
        out_shape=[jax.ShapeDtypeStruct((b, t, W_D), BF16), jax.ShapeDtypeStruct((b, H_D, HD_D, HD_D), F32)],
        scratch_shapes=[pltpu.VMEM((nh, HD_D, HD_D), F32)],
        compiler_params=_cparams("parallel", "arbitrary"),
        name="retention",
    )(zd, zd, zd, zd, nrm.reshape(1, HD_D), s0)


def _block(x, pe, cache, s_hgrn, s_delta, buf_delta, s_ret, buf_ffn, p, layer):
    bsz, t, d = x.shape
    m = bsz * t
    x2 = x.reshape(m, d)
    hn = _rmsnorm(x2, p['attn_norm'][layer], BF16)
    za = _matmul(hn, p['w_in_a'], layer).reshape(bsz, t, -1)
    zb = _matmul(hn, p['w_in_b'], layer).reshape(bsz, t, -1)
    zc = _matmul(hn, p['w_in_c'], layer).reshape(bsz, t, -1)
    zg = _matmul(hn, p['w_in_g'], layer).reshape(bsz, t, -1)
    zd = _matmul(hn, p['w_in_d'], layer).reshape(bsz, t, -1)

    if cache is None:
        o_a = _attention_prompt(za)
        kv_new = za
    else:
        o_a = _attention_sample(za, cache, layer)
        kv_new = za[:, :, W_A:].reshape(bsz, t, 2, H_A, HD_A)

    o_b, s_hgrn_new = _hgrn(zb, p['hgrn_lb'], p['hgrn_norm'][layer], s_hgrn, layer, GLA_CHUNK)
    o_c, s_delta_new = _delta(zc, zg, p['delta_conv'][layer], buf_delta, p['delta_A_log'][layer],
                              p['delta_dt_bias'][layer], p['delta_norm'][layer], s_delta, DELTA_CHUNK)
    pre = jnp.concatenate([buf_delta, zc[:, :, :2 * W_CQK + W_CV]], axis=1)
    buf_delta_new = pre[:, pre.shape[1] - (C_CONV - 1):]
    o_d, s_ret_new = _retention(zd, p['ret_norm'][layer], s_ret, RET_CHUNK)

    parts = [o.reshape(m, -1) for o in (o_a, o_b, o_c, o_d)]
    x2 = _matmul_residual(parts, p['w_out'], layer, x2, (1024, 512, 256, 64), (512, 256, 128))

    hf = _rmsnorm(x2, p['ffn_norm'][layer], BF16)
    if cache is None:
        act, buf_ffn_new = _ffn_gate_up_prompt(hf, p['w_gate'], p['w_up'], p['ffn_conv'], layer, t)
    else:
        act, buf_ffn_new = _ffn_gate_up_sample(hf, p['w_gate'], p['w_up'], p['ffn_conv'], layer, buf_ffn, t)
    x2 = _matmul_residual([act], p['w_down'], layer, x2, (512, 256, 64), (256, 128))

    hp = _rmsnorm(x2, p['ple_norm'][layer], BF16)
    x2 = _ple(hp, p['ple_gate'], pe.reshape(m, -1).astype(BF16), p['ple_proj'], layer, x2)
    states = (kv_new, s_hgrn_new, s_delta_new, buf_delta_new, s_ret_new, buf_ffn_new)
    return x2.reshape(bsz, t, d), states


def kernel(x_prompt, x_sample, cache_attn_kv, state_hgrn, state_delta, state_delta_conv, state_ret,
           state_ffn_conv, p_prompt, p_sample, attn_norm, w_in, hgrn_lb, hgrn_norm, delta_conv,
           delta_A_log, delta_dt_bias, delta_norm, ret_norm, w_out, ffn_norm, w_gate, w_up, ffn_conv,
           w_down, ple_norm, ple_gate, ple_proj, final_norm):
    depth = w_in.shape[0]
    bp = x_prompt.shape[0]
    o_b = 3 * W_A
    o_c = o_b + 4 * W_B
    o_g = o_c + 2 * W_CQK + 2 * W_CV
    o_d = o_g + 2 * H_C_V
    w_in_d, w_in_g = _cast_tail(w_in, o_g, o_d - o_g)
    p = {'attn_norm': attn_norm,
         'w_in_a': _cast_bf16(w_in, 0, o_b), 'w_in_b': _cast_bf16(w_in, o_b, o_c - o_b),
         'w_in_c': _cast_bf16(w_in, o_c, o_g - o_c), 'w_in_g': w_in_g, 'w_in_d': w_in_d,
         'hgrn_lb': hgrn_lb, 'hgrn_norm': hgrn_norm, 'delta_conv': delta_conv, 'delta_A_log': delta_A_log,
         'delta_dt_bias': delta_dt_bias, 'delta_norm': delta_norm, 'ret_norm': ret_norm,
         'w_out': _cast_bf16(w_out), 'ffn_norm': ffn_norm, 'w_gate': _cast_bf16(w_gate),
         'w_up': _cast_bf16(w_up), 'ffn_conv': ffn_conv, 'w_down': _cast_bf16(w_down),
         'ple_norm': ple_norm, 'ple_gate': _cast_bf16(ple_gate), 'ple_proj': _cast_bf16(ple_proj)}
    xp, xs = x_prompt, x_sample
    st_p, st_s = [], []
    for l in range(depth):
        zero = lambda *s: jnp.zeros((bp,) + s, F32)
        xp, sp = _block(xp, p_prompt[l], None, zero(H_B, HD_B, HD_B), zero(H_C_V, HD_C, HD_C),
                        zero(C_CONV - 1, 2 * W_CQK + W_CV), zero(H_D, HD_D, HD_D), None, p, l)
        xs, ss = _block(xs, p_sample[l], cache_attn_kv, state_hgrn[l], state_delta[l], state_delta_conv[l],
                        state_ret[l], state_ffn_conv[l], p, l)
        st_p.append(sp)
        st_s.append(ss)

    def stack(sts, i):
        return jnp.stack([s[i] for s in sts])

    def final(x):
        return _rmsnorm(x.reshape(-1, x.shape[-1]), final_norm, F32).reshape(x.shape)

    kv_sample = _kv_shift(cache_attn_kv, stack(st_s, 0))
    s_p = x_prompt.shape[1]
    kv_prompt = _kv_pack([s[0] for s in st_p], min(A_BRANCHES[-1][0], s_p))
    return (final(xp), final(xs),
            kv_prompt, kv_sample, stack(st_p, 1), stack(st_s, 1), stack(st_p, 2), stack(st_s, 2),
            stack(st_p, 3), stack(st_s, 3), stack(st_p, 4), stack(st_s, 4), stack(st_p, 5), stack(st_s, 5))
```

```python
import functools
import math

import jax
import jax.numpy as jnp
from jax import lax
from jax.experimental import pallas as pl
from jax.experimental.pallas import tpu as pltpu

F32 = jnp.float32
BF16 = jnp.bfloat16
EPS = 1e-6
NEG = -1e30

H_A = 8
HD_A = 128
A_BRANCHES = ((128, 1), (512, 4), (2048, 16))
A_BLOCK = 128
H_B = 8
HD_B = 128
H_C_QK = 4
H_C_V = 8
HD_C = 128
C_CONV = 4
H_D = 4
HD_D = 256
FFN_CONV = 3

W_A = H_A * HD_A
W_B = H_B * HD_B
W_CQK = H_C_QK * HD_C
W_CV = H_C_V * HD_C
W_D = H_D * HD_D
W_CG = 128

VMEM_LIMIT_BYTES = 52 * 1024 * 1024
SUBLANES = 8
LANES = 128

GLA_HEADS_PER_STEP = 8
DELTA_QK_HEADS_PER_STEP = 4
RET_HEADS_PER_STEP = 2
MIXER_ROW_BLOCK = 1024
GLA_CHUNK = 64
DELTA_CHUNK = 64
RET_CHUNK = 256
DELTA_SUB = 16

_NN = (((1,), (0,)), ((), ()))
_NT = (((1,), (1,)), ((), ()))
_TN = (((0,), (0,)), ((), ()))


def _cparams(*sem):
    return pltpu.CompilerParams(dimension_semantics=sem, vmem_limit_bytes=VMEM_LIMIT_BYTES)


def _dg(a, b, dn=_NN):
    return lax.dot_general(a, b, dn, preferred_element_type=F32)


def _dot1(a, b, dn=_NN):
    return _dg(a.astype(BF16), b.astype(BF16), dn)


def _split3(x):
    x1 = x.astype(BF16)
    r1 = x - x1.astype(F32)
    x2 = r1.astype(BF16)
    x3 = (r1 - x2.astype(F32)).astype(BF16)
    return x1, x2, x3


def _sel_dot(m, x):
    x1, x2, x3 = _split3(x)
    return _dg(m, x1) + (_dg(m, x2) + _dg(m, x3))


def _sigmoid(x):
    return 1.0 / (1.0 + jnp.exp(-x))


def _silu(x):
    return x * _sigmoid(x)


def _softplus(x):
    return jnp.maximum(x, 0.0) + jnp.log1p(jnp.exp(-jnp.abs(x)))


def _pad_rows(x, rows):
    if x.shape[0] == rows:
        return x
    return jnp.concatenate([x, jnp.zeros((rows - x.shape[0],) + x.shape[1:], x.dtype)], axis=0)


def _onehot(cond):
    return jnp.where(cond, 1.0, 0.0).astype(BF16)


def _pick(n, prefs):
    for p in prefs:
        if n % p == 0:
            return p
    return n


def _rmsnorm_kernel(x_ref, g_ref, o_ref):
    x = x_ref[...]
    y = x * lax.rsqrt(jnp.mean(x * x, axis=-1, keepdims=True) + EPS)
    o_ref[...] = (y * g_ref[...]).astype(o_ref.dtype)


def _rmsnorm(x, g, out_dtype):
    m, d = x.shape
    tm = _pick(m, (256, 64, 8))
    return pl.pallas_call(
        _rmsnorm_kernel,
        grid=(m // tm,),
        in_specs=[pl.BlockSpec((tm, d), lambda i: (i, 0)), pl.BlockSpec((1, d), lambda i: (0, 0))],
        out_specs=pl.BlockSpec((tm, d), lambda i: (i, 0)),
        out_shape=jax.ShapeDtypeStruct((m, d), out_dtype),
        compiler_params=_cparams("parallel"),
        name="rmsnorm",
    )(x, g.reshape(1, d))


def _mm_kernel(a_ref, w_ref, o_ref):
    o_ref[...] = _dg(a_ref[...], w_ref[...].astype(BF16)).astype(o_ref.dtype)


def _matmul(a, w, layer, out_dtype=F32):
    m, k = a.shape
    n = w.shape[2]
    tm = _pick(m, (1024, 512, 256, 64))
    tn = _pick(n, (512, 256, 128))
    return pl.pallas_call(
        _mm_kernel,
        grid=(m // tm, n // tn),
        in_specs=[pl.BlockSpec((tm, k), lambda i, j: (i, 0)),
                  pl.BlockSpec((None, k, tn), lambda i, j: (layer, 0, j))],
        out_specs=pl.BlockSpec((tm, tn), lambda i, j: (i, j)),
        out_shape=jax.ShapeDtypeStruct((m, n), out_dtype),
        compiler_params=_cparams("parallel", "arbitrary"),
        name="matmul",
    )(a, w)


def _mm_res_kernel(*refs, nparts):
    a_refs = refs[:nparts]
    w_refs = refs[nparts:2 * nparts]
    x_ref = refs[2 * nparts]
    o_ref = refs[2 * nparts + 1]
    acc = x_ref[...]
    for a_ref, w_ref in zip(a_refs, w_refs):
        acc = acc + _dg(a_ref[...], w_ref[...].astype(BF16))
    o_ref[...] = acc


def _matmul_residual(parts, w, layer, x, tm_prefs, tn_prefs):
    m, n = x.shape
    kp = parts[0].shape[1]
    nparts = len(parts)
    tm = _pick(m, tm_prefs)
    tn = _pick(n, tn_prefs)

    def wmap(p):
        return lambda i, j: (layer, p, j)

    in_specs = [pl.BlockSpec((tm, kp), lambda i, j: (i, 0)) for _ in parts]
    in_specs += [pl.BlockSpec((None, kp, tn), wmap(p)) for p in range(nparts)]
    in_specs += [pl.BlockSpec((tm, tn), lambda i, j: (i, j))]
    return pl.pallas_call(
        functools.partial(_mm_res_kernel, nparts=nparts),
        grid=(m // tm, n // tn),
        in_specs=in_specs,
        out_specs=pl.BlockSpec((tm, tn), lambda i, j: (i, j)),
        out_shape=jax.ShapeDtypeStruct((m, n), F32),
        compiler_params=_cparams("parallel", "arbitrary"),
        name="matmul_residual",
    )(*parts, *([w] * nparts), x)


def _ple_kernel(a_ref, wg_ref, pe_ref, wp_ref, x_ref, o_ref):
    gate = _sigmoid(_dg(a_ref[...], wg_ref[...].astype(BF16)))
    o_ref[...] = x_ref[...] + gate * _dg(pe_ref[...], wp_ref[...].astype(BF16))


def _ple(hp, wg, pe, wp, layer, x):
    m, d = x.shape
    k = hp.shape[1]
    kp = pe.shape[1]
    tm = _pick(m, (1024, 512, 256, 64))
    tn = _pick(d, (512, 256, 128))
    return pl.pallas_call(
        _ple_kernel,
        grid=(m // tm, d // tn),
        in_specs=[pl.BlockSpec((tm, k), lambda i, j: (i, 0)),
                  pl.BlockSpec((None, k, tn), lambda i, j: (layer, 0, j)),
                  pl.BlockSpec((tm, kp), lambda i, j: (i, 0)),
                  pl.BlockSpec((None, kp, tn), lambda i, j: (layer, 0, j)),
                  pl.BlockSpec((tm, tn), lambda i, j: (i, j))],
        out_specs=pl.BlockSpec((tm, tn), lambda i, j: (i, j)),
        out_shape=jax.ShapeDtypeStruct((m, d), F32),
        compiler_params=_cparams("parallel", "arbitrary"),
        name="ple",
    )(hp, wg, pe, wp, x)


def _ffn_act(g, p1, p2, cw, u):
    gc = cw[0:1] * p2 + cw[1:2] * p1 + cw[2:3] * g
    return (_silu(gc) * u).astype(BF16)


def _ffn_gu_prompt_kernel(h_ref, halo_ref, wg_ref, wu_ref, cw_ref, o_ref, tail_ref, *, seq):
    tm = h_ref.shape[0]
    a = h_ref[...]
    wg = wg_ref[...].astype(BF16)
    g = _dg(a, wg)
    u = _dg(a, wu_ref[...].astype(BF16))
    gh = _dg(halo_ref[...], wg)
    seq_start = lax.rem(pl.program_id(0) * tm, seq) == 0
    gh = jnp.where(seq_start, 0.0, gh)
    row = lax.broadcasted_iota(jnp.int32, g.shape, 0)
    p1 = jnp.where(row == 0, gh[7:8], pltpu.roll(g, 1, 0))
    p2 = jnp.where(row == 0, gh[6:7], jnp.where(row == 1, gh[7:8], pltpu.roll(g, 2, 0)))
    o_ref[...] = _ffn_act(g, p1, p2, cw_ref[...], u)
    tail_ref[0] = g[tm - SUBLANES:tm]


def _ffn_gate_up_prompt(hf, wg, wu, cw, layer, seq):
    m, d = hf.shape
    f = wg.shape[2]
    tm = _pick(seq, (1024, 512, 256, 128, 64, 8))
    tf = _pick(f, (256, 128))
    hb = tm // SUBLANES
    wmap = lambda i, j: (layer, 0, j)
    out, tail = pl.pallas_call(
        functools.partial(_ffn_gu_prompt_kernel, seq=seq),
        grid=(m // tm, f // tf),
        in_specs=[pl.BlockSpec((tm, d), lambda i, j: (i, 0)),
                  pl.BlockSpec((SUBLANES, d), lambda i, j: (jnp.maximum(i * hb - 1, 0), 0)),
                  pl.BlockSpec((None, d, tf), wmap),
                  pl.BlockSpec((None, d, tf), wmap),
                  pl.BlockSpec((None, FFN_CONV, tf), wmap)],
        out_specs=[pl.BlockSpec((tm, tf), lambda i, j: (i, j)),
                   pl.BlockSpec((1, SUBLANES, tf), lambda i, j: (i, 0, j))],
        out_shape=[jax.ShapeDtypeStruct((m, f), BF16),
                   jax.ShapeDtypeStruct((m // tm, SUBLANES, f), F32)],
        compiler_params=_cparams("parallel", "arbitrary"),
        name="ffn_gate_up_prompt",
    )(hf, hf, wg, wu, cw)
    per_seq = seq // tm
    tail = tail.reshape(m // seq, per_seq, SUBLANES, f)[:, per_seq - 1, SUBLANES - (FFN_CONV - 1):]
    return out, tail


def _ffn_gu_sample_kernel(h_ref, wg_ref, wu_ref, cw_ref, b1_ref, b2_ref, o_ref, g_ref, *, t):
    a = h_ref[...]
    g = _dg(a, wg_ref[...].astype(BF16))
    u = _dg(a, wu_ref[...].astype(BF16))
    pos = lax.rem(lax.broadcasted_iota(jnp.int32, g.shape, 0), t)
    p1 = jnp.where(pos == 0, b1_ref[...], pltpu.roll(g, 1, 0))
    p2 = jnp.where(pos < 2, b2_ref[...], pltpu.roll(g, 2, 0))
    o_ref[...] = _ffn_act(g, p1, p2, cw_ref[...], u)
    g_ref[...] = g


def _ffn_gate_up_sample(hf, wg, wu, cw, layer, buf, t):
    m, d = hf.shape
    f = wg.shape[2]
    nb = m // t
    tf = _pick(f, (256, 128))
    zeros = jnp.zeros((nb, t - 1, f), F32)
    b1 = jnp.concatenate([buf[:, 1:2], zeros], axis=1).reshape(m, f)
    b2 = jnp.concatenate([buf[:, 0:2], zeros[:, 1:]], axis=1).reshape(m, f)
    wmap = lambda j: (layer, 0, j)
    out, g = pl.pallas_call(
        functools.partial(_ffn_gu_sample_kernel, t=t),
        grid=(f // tf,),
        in_specs=[pl.BlockSpec((m, d), lambda j: (0, 0)),
                  pl.BlockSpec((None, d, tf), wmap),
                  pl.BlockSpec((None, d, tf), wmap),
                  pl.BlockSpec((None, FFN_CONV, tf), wmap),
                  pl.BlockSpec((m, tf), lambda j: (0, j)),
                  pl.BlockSpec((m, tf), lambda j: (0, j))],
        out_specs=[pl.BlockSpec((m, tf), lambda j: (0, j)), pl.BlockSpec((m, tf), lambda j: (0, j))],
        out_shape=[jax.ShapeDtypeStruct((m, f), BF16), jax.ShapeDtypeStruct((m, f), F32)],
        compiler_params=_cparams("arbitrary"),
        name="ffn_gate_up_sample",
    )(hf, wg, wu, cw, b1, b2)
    tail = g.reshape(nb, t, f)[:, t - (FFN_CONV - 1):]
    return out, tail


ATTN_HEADS_PER_STEP = 2


def _band_softmax_many(probs, dist_cf, dist_pf, valid_c):
    scs = [_dot1(pr[0], pr[1], _NT) for pr in probs]
    sps = [None if pr[2] is None else _dot1(pr[0], pr[2], _NT) for pr in probs]
    pcs, pps, ls, lses = [], [], [], []
    for pr, sc, sp in zip(probs, scs, sps):
        slope, prev_ok = pr[5], pr[6]
        sc = jnp.where(valid_c, sc - slope * dist_cf, NEG)
        if sp is None:
            m = jnp.max(sc, axis=-1, keepdims=True)
            pc = jnp.exp(sc - m)
            pp = None
            l = jnp.sum(pc, axis=-1, keepdims=True)
        else:
            sp = jnp.where(prev_ok, sp - slope * dist_pf, NEG)
            m = jnp.max(jnp.maximum(sc, sp), axis=-1, keepdims=True)
            pc = jnp.exp(sc - m)
            pp = jnp.exp(sp - m)
            l = jnp.sum(pc + pp, axis=-1, keepdims=True)
        pcs.append(pc)
        pps.append(pp)
        ls.append(l)
        lses.append(m + jnp.log(l))
    ocs = [_dot1(pc, pr[3]) for pc, pr in zip(pcs, probs)]
    ops = [None if pp is None else _dot1(pp, pr[4]) for pp, pr in zip(pps, probs)]
    outs = [(oc if op is None else oc + op) / l for oc, op, l in zip(ocs, ops, ls)]
    return outs, lses


def _band_kernel(*refs, s):
    nh = ATTN_HEADS_PER_STEP
    q_refs, k_refs, v_refs = refs[0:nh], refs[nh:2 * nh], refs[2 * nh:3 * nh]
    o_ref, o2_ref, o3_ref, l2_ref, l3_ref = refs[3 * nh:]
    nq = A_BLOCK
    hg = pl.program_id(1)
    qi = lax.broadcasted_iota(jnp.int32, (nq, nq), 0)
    kj = lax.broadcasted_iota(jnp.int32, (nq, nq), 1)
    dist_c = qi - kj
    dist_p = dist_c + nq
    valid_c = dist_c >= 0
    dist_cf = dist_c.astype(F32)
    dist_pf = dist_p.astype(F32)
    (w1, d1), (w2, d2), (w3, d3) = A_BRANCHES
    assert d1 == 1 and s % (d2 * nq) == 0 and s == d3 * nq and max(w1 // d1, w2 // d2, w3 // d3) <= nq

    def problem(h, rows_c, rows_p, has_prev, band, dil):
        hv = jnp.full((1, 1), hg * nh + h + 1, jnp.int32).astype(F32)
        slope = float(dil) * jnp.exp2(-8.0 * hv / H_A)
        q = q_refs[h][0, rows_c, :] * HD_A ** -0.5
        if rows_p is None:
            return (q, k_refs[h][0, rows_c, :], None, v_refs[h][0, rows_c, :], None, slope, None)
        return (q, k_refs[h][0, rows_c, :], k_refs[h][0, rows_p, :], v_refs[h][0, rows_c, :],
                v_refs[h][0, rows_p, :], slope, (dist_p <= band) & has_prev)

    def dilated(i, carry):
        rows3 = pl.ds(i, nq, stride=d3)
        r = lax.rem(i, d2)
        b = i // d2
        rows2 = pl.ds(b * (d2 * nq) + r, nq, stride=d2)
        rows2p = pl.ds(jnp.maximum(b - 1, 0) * (d2 * nq) + r, nq, stride=d2)
        probs = [problem(h, rows3, None, False, w3 // d3, d3) for h in range(nh)]
        probs += [problem(h, rows2, rows2p, b > 0, w2 // d2, d2) for h in range(nh)]
        outs, lses = _band_softmax_many(probs, dist_cf, dist_pf, valid_c)
        for h in range(nh):
            o3_ref[h, rows3, :] = outs[h]
            l3_ref[h, rows3, :] = jnp.broadcast_to(lses[h], (nq, HD_A))
            o2_ref[h, rows2, :] = outs[nh + h]
            l2_ref[h, rows2, :] = jnp.broadcast_to(lses[nh + h], (nq, HD_A))
        return carry

    lax.fori_loop(0, s // nq, dilated, 0)

    def dense(i, carry):
        blocks = []
        for j in range(2):
            bi = 2 * i + j
            rows = pl.ds(pl.multiple_of(bi * nq, nq), nq)
            rows_p = pl.ds(pl.multiple_of(jnp.maximum(bi - 1, 0) * nq, nq), nq)
            blocks += [(h, rows, problem(h, rows, rows_p, bi > 0, w1 // d1, d1)) for h in range(nh)]
        outs, lses = _band_softmax_many([blk[2] for blk in blocks], dist_cf, dist_pf, valid_c)
        for (h, rows, _), o1, l1 in zip(blocks, outs, lses):
            l2 = l2_ref[h, rows, :]
            l3 = l3_ref[h, rows, :]
            mx = jnp.maximum(l1, jnp.maximum(l2, l3))
            e1 = jnp.exp(l1 - mx)
            e2 = jnp.exp(l2 - mx)
            e3 = jnp.exp(l3 - mx)
            o = (e1 * o1 + e2 * o2_ref[h, rows, :] + e3 * o3_ref[h, rows, :]) / (e1 + e2 + e3)
            o_ref[0, rows, h * HD_A:(h + 1) * HD_A] = o.astype(o_ref.dtype)
        return carry

    assert (s // nq) % 2 == 0
    lax.fori_loop(0, s // (2 * nq), dense, 0)


def _attention_prompt(za):
    b, s, _ = za.shape
    nh = ATTN_HEADS_PER_STEP
    ng = H_A // nh
    blk = (1, s, HD_A)

    def head_spec(first, h):
        return pl.BlockSpec(blk, lambda bi, g: (bi, 0, first + g * nh + h))

    in_specs = [head_spec(part * H_A, h) for part in range(3) for h in range(nh)]
    return pl.pallas_call(
        functools.partial(_band_kernel, s=s),
        grid=(b, ng),
        in_specs=in_specs,
        out_specs=pl.BlockSpec((1, s, nh * HD_A), lambda bi, g: (bi, 0, g)),
        out_shape=jax.ShapeDtypeStruct((b, s, W_A), BF16),
        scratch_shapes=[pltpu.VMEM((nh, s, HD_A), F32) for _ in range(4)],
        compiler_params=_cparams("parallel", "arbitrary"),
        name="band_attention",
    )(*([za] * (3 * nh)))


def _attn_sample_kernel(q_ref, kn_ref, vn_ref, c_ref, o_ref):
    t = q_ref.shape[1]
    rows_per_pos = 2 * H_A
    n_past = c_ref.shape[0] // rows_per_pos
    tp = 2 * SUBLANES
    shape_c = (tp, n_past)
    shape_n = (tp, tp)
    dist_c = n_past + lax.broadcasted_iota(jnp.int32, shape_c, 0) - lax.broadcasted_iota(jnp.int32, shape_c, 1)
    dist_n = lax.broadcasted_iota(jnp.int32, shape_n, 0) - lax.broadcasted_iota(jnp.int32, shape_n, 1)
    new_ok = lax.broadcasted_iota(jnp.int32, shape_n, 1) < t
    dist_cf = dist_c.astype(F32)
    dist_nf = dist_n.astype(F32)
    oks = []
    for window, dil in A_BRANCHES:
        oks.append(((dist_c <= window) & (lax.rem(dist_c, dil) == 0),
                    (dist_n >= 0) & (dist_n <= window) & (lax.rem(dist_n, dil) == 0) & new_ok))
    hs = range(H_A)
    slopes = [2.0 ** (-8.0 * (h + 1) / H_A) for h in hs]
    sls = [slice(h * HD_A, (h + 1) * HD_A) for h in hs]
    qs = [_pad_rows(q_ref[0, :, sl] * HD_A ** -0.5, tp).astype(BF16) for sl in sls]
    kns = [_pad_rows(kn_ref[0, :, sl], tp).astype(BF16) for sl in sls]
    vns = [_pad_rows(vn_ref[0, :, sl], tp).astype(BF16) for sl in sls]
    kcs = [c_ref[pl.ds(h, n_past, stride=rows_per_pos), :].astype(BF16) for h in hs]
    vcs = [c_ref[pl.ds(H_A + h, n_past, stride=rows_per_pos), :].astype(BF16) for h in hs]
    b_cs = [_dg(qs[h], kcs[h], _NT) - slopes[h] * dist_cf for h in hs]
    b_ns = [_dg(qs[h], kns[h], _NT) - slopes[h] * dist_nf for h in hs]
    pcs, pns, ls, lses = [], [], [], []
    for h in hs:
        for ok_c, ok_n in oks:
            sc = jnp.where(ok_c, b_cs[h], NEG)
            sn = jnp.where(ok_n, b_ns[h], NEG)
            m = jnp.maximum(jnp.max(sc, axis=-1, keepdims=True), jnp.max(sn, axis=-1, keepdims=True))
            pc = jnp.exp(sc - m)
            pn = jnp.exp(sn - m)
            pcs.append(pc.astype(BF16))
            pns.append(pn.astype(BF16))
            ls.append(jnp.sum(pc, axis=-1, keepdims=True) + jnp.sum(pn, axis=-1, keepdims=True))
            lses.append(m + jnp.log(ls[-1]))
    nbr = len(oks)
    outs = [(_dg(pcs[i], vcs[i // nbr]) + _dg(pns[i], vns[i // nbr])) / ls[i] for i in range(len(pcs))]
    for h in hs:
        lse_h = lses[h * nbr:(h + 1) * nbr]
        out_h = outs[h * nbr:(h + 1) * nbr]
        mx = functools.reduce(jnp.maximum, lse_h)
        es = [jnp.exp(x - mx) for x in lse_h]
        tot = functools.reduce(lambda a, b: a + b, es)
        o = functools.reduce(lambda a, b: a + b, [e * x for e, x in zip(es, out_h)]) / tot
        o_ref[0, :, sls[h]] = o[:t].astype(o_ref.dtype)


def _attention_sample(za, cache, layer):
    b, t, _ = za.shape
    depth, _, l = cache.shape[:3]
    rows = l * 2 * H_A
    cache2d = cache.reshape(depth, b, rows, HD_A)
    return pl.pallas_call(
        _attn_sample_kernel,
        grid=(b,),
        in_specs=[pl.BlockSpec((1, t, W_A), lambda bi: (bi, 0, 0)),
                  pl.BlockSpec((1, t, W_A), lambda bi: (bi, 0, 1)),
                  pl.BlockSpec((1, t, W_A), lambda bi: (bi, 0, 2)),
                  pl.BlockSpec((None, None, rows, HD_A), lambda bi: (layer, bi, 0, 0))],
        out_specs=pl.BlockSpec((1, t, W_A), lambda bi: (bi, 0, 0)),
        out_shape=jax.ShapeDtypeStruct((b, t, W_A), BF16),
        compiler_params=_cparams("parallel"),
        name="attention_sample",
    )(za, za, za, cache2d)


def _cast_kernel(x_ref, o_ref):
    o_ref[...] = x_ref[...].astype(o_ref.dtype)


CAST_BLOCK_BYTES = 6 * 1024 * 1024


def _cast_bf16(w, col0=0, ncols=None):
    depth, k, n = w.shape
    ncols = n - col0 if ncols is None else ncols
    tn = next(c for c in (1024, 512, 256, LANES) if ncols % c == 0 and col0 % c == 0)
    packed_rows = 2 * SUBLANES
    tk = max(d for d in range(packed_rows, k + 1, packed_rows) if k % d == 0 and d * tn * 4 <= CAST_BLOCK_BYTES)
    c0 = col0 // tn
    return pl.pallas_call(
        _cast_kernel,
        grid=(depth, k // tk, ncols // tn),
        in_specs=[pl.BlockSpec((1, tk, tn), lambda l, i, j: (l, i, c0 + j))],
        out_specs=pl.BlockSpec((1, tk, tn), lambda l, i, j: (l, i, j)),
        out_shape=jax.ShapeDtypeStruct((depth, k, ncols), BF16),
        compiler_params=_cparams("parallel", "parallel", "parallel"),
        name="cast_bf16",
    )(w)


def _cast_t_kernel(x_ref, o_ref):
    o_ref[0] = x_ref[0].T.astype(o_ref.dtype)


def _cast_bf16_t(wt, row0, nrows, tn=512):
    depth, _, k = wt.shape
    assert row0 % tn == 0 and nrows % tn == 0
    tk = 1024 if k % 1024 == 0 else k
    r0 = row0 // tn
    return pl.pallas_call(
        _cast_t_kernel,
        grid=(depth, k // tk, nrows // tn),
        in_specs=[pl.BlockSpec((1, tn, tk), lambda l, i, j: (l, r0 + j, i))],
        out_specs=pl.BlockSpec((1, tk, tn), lambda l, i, j: (l, i, j)),
        out_shape=jax.ShapeDtypeStruct((depth, k, nrows), BF16),
        compiler_params=_cparams("parallel", "parallel", "parallel"),
        name="cast_bf16_t",
    )(wt)


def _cast_tail_t_kernel(a_ref, b_ref, d_ref, g_ref, *, shift):
    a = a_ref[0]
    tn = a.shape[0]
    x = jnp.concatenate([a, b_ref[0]], axis=0)
    d_ref[0] = x[shift:shift + tn].T.astype(d_ref.dtype)

    @pl.when(pl.program_id(2) == 0)
    def _():
        lane = lax.broadcasted_iota(jnp.int32, (a.shape[1], LANES), 1)
        g_ref[0] = jnp.where(lane < shift, a[0:LANES].T, 0.0).astype(g_ref.dtype)


def _cast_tail_t(wt, row0, shift, tn=512):
    depth, n, k = wt.shape
    nrows = n - row0 - shift
    packed_rows = 2 * SUBLANES
    assert row0 % tn == 0 and nrows % tn == 0 and shift % packed_rows == 0 and 0 < shift <= LANES
    tk = 1024 if k % 1024 == 0 else k
    r0 = row0 // tn
    per = tn // shift
    return pl.pallas_call(
        functools.partial(_cast_tail_t_kernel, shift=shift),
        grid=(depth, k // tk, nrows // tn),
        in_specs=[pl.BlockSpec((1, tn, tk), lambda l, i, j: (l, r0 + j, i)),
                  pl.BlockSpec((1, shift, tk), lambda l, i, j: (l, (r0 + j + 1) * per, i))],
        out_specs=[pl.BlockSpec((1, tk, tn), lambda l, i, j: (l, i, j)),
                   pl.BlockSpec((1, tk, LANES), lambda l, i, j: (l, i, 0))],
        out_shape=[jax.ShapeDtypeStruct((depth, k, nrows), BF16), jax.ShapeDtypeStruct((depth, k, LANES), BF16)],
        compiler_params=_cparams("parallel", "parallel", "arbitrary"),
        name="cast_tail_t",
    )(wt, wt)


def _kv_pack_kernel(*refs):
    o_ref = refs[-1]
    depth = (len(refs) - 1) // 2
    w = refs[0].shape[2]
    for l in range(depth):
        @pl.when(pl.program_id(0) == l)
        def _(l=l):
            o_ref[0, :, 0:w] = refs[2 * l][0]
            o_ref[0, :, w:2 * w] = refs[2 * l + 1][0]


def _kv_pack(zas, rows):
    depth = len(zas)
    b, s, _ = zas[0].shape
    ts = _pick(rows, (1024, 512, 256, 128, 64, 8))
    first = (s - rows) // ts
    assert (s - rows) % ts == 0
    in_specs, args = [], []
    for l in range(depth):
        for part in (1, 2):
            def imap(d, bi, si, l=l, part=part):
                return (bi, jnp.where(d == l, first + si, first), part)
            in_specs.append(pl.BlockSpec((1, ts, W_A), imap))
            args.append(zas[l])
    out = pl.pallas_call(
        _kv_pack_kernel,
        grid=(depth, b, rows // ts),
        in_specs=in_specs,
        out_specs=pl.BlockSpec((None, 1, ts, 2 * W_A), lambda d, bi, si: (d, bi, si, 0)),
        out_shape=jax.ShapeDtypeStruct((depth, b, rows, 2 * W_A), F32),
        compiler_params=_cparams("parallel", "parallel", "parallel"),
        name="kv_pack",
    )(*args)
    return out.reshape(depth, b, rows, 2, H_A, HD_A)


def _kv_shift_kernel(c_ref, n_ref, o_ref):
    l = c_ref.shape[0]
    t = n_ref.shape[0]
    o_ref[0:l - t] = c_ref[t:l]
    o_ref[l - t:l] = n_ref[...]


def _kv_shift(cache, kv_new):
    depth, b, l, two, h, hd = cache.shape
    t = kv_new.shape[2]
    cblk = (None, None, l, None, h, hd)
    cmap = lambda d, bi, s: (d, bi, 0, s, 0, 0)
    return pl.pallas_call(
        _kv_shift_kernel,
        grid=(depth, b, two),
        in_specs=[pl.BlockSpec(cblk, cmap), pl.BlockSpec((None, None, t, None, h, hd), cmap)],
        out_specs=pl.BlockSpec(cblk, cmap),
        out_shape=jax.ShapeDtypeStruct(cache.shape, cache.dtype),
        compiler_params=_cparams("parallel", "parallel", "parallel"),
        name="kv_shift",
    )(cache, kv_new)


def _chunk_plan(t, chunk):
    c = chunk if t >= chunk else max(2 * SUBLANES, t)
    assert t % c == 0 or t < c
    tc = min(t, c)
    return c, tc, max(t // c, 1)


def _run_chunks(nchunks, c, chunk):
    if nchunks == 1:
        chunk(0, 0)
    else:
        def body(ci, carry):
            chunk(ci, pl.multiple_of(ci * c, c))
            return carry
        lax.fori_loop(0, nchunks, body, 0)


def _gla_kernel(q_ref, f_ref, i_ref, g_ref, lb_ref, nrm_ref, s0_ref, o_ref, sfin_ref, st_ref, *, layer, c, tc, nchunks):
    nlev = int(math.log2(c))
    assert 1 << nlev == c
    depth = lb_ref.shape[0]
    nh = st_ref.shape[0]
    lbs = [lb_ref[i] for i in range(depth)]
    mx = functools.reduce(jnp.maximum, lbs)
    es = [jnp.exp(x - mx) for x in lbs]
    tot = functools.reduce(lambda a, b: a + b, es)
    lower = functools.reduce(lambda a, b: a + b, [es[i] / tot for i in range(layer + 1)]) - es[0] / tot
    one_minus_lb = 1.0 - lower

    row = lax.broadcasted_iota(jnp.int32, (c, c), 0)
    col = lax.broadcasted_iota(jnp.int32, (c, c), 1)
    prefix = [_onehot(col <= row)]
    lmask, second = [], []
    row_hd = lax.broadcasted_iota(jnp.int32, (c, HD_B), 0)
    for lev in range(1, nlev + 1):
        s = c >> lev
        grp = ~(2 * s - 1)
        prefix.append(_onehot(col <= (row & grp) + (s - 1)))
        lmask.append((((row ^ col) & grp) == 0) & ((row & s) != 0) & ((col & s) == 0))
        second.append((row_hd & s) != 0)
    mstack = jnp.concatenate(prefix, axis=0)
    eye = row == col
    valid = lax.broadcasted_iota(jnp.int32, (c, 1), 0) < tc
    nrm = nrm_ref[...]

    @pl.when(pl.program_id(2) == 0)
    def _():
        for h in range(nh):
            st_ref[h] = s0_ref[0, h].T

    def chunk(ci, t0):
        rows = pl.ds(t0, tc)
        hs = range(nh)
        sls = [slice(h * HD_B, (h + 1) * HD_B) for h in hs]
        qs = [_silu(_pad_rows(q_ref[0, rows, sl], c)) for sl in sls]
        kbs = [one_minus_lb[:, sl] * _sigmoid(-_pad_rows(f_ref[0, rows, sl], c)) for sl in sls]
        logfs = [jnp.log1p(-kb) for kb in kbs]
        if tc < c:
            kbs = [jnp.where(valid, kb, 0.0) for kb in kbs]
            logfs = [jnp.where(valid, lf, 0.0) for lf in logfs]
        vs = [_pad_rows(i_ref[0, rows, sl], c).astype(BF16) for sl in sls]
        bs_all = _sel_dot(mstack, jnp.concatenate(logfs, axis=1))
        bss = [bs_all[:, sl] for sl in sls]
        b = [bs[0:c] for bs in bss]
        atts = [jnp.where(eye, jnp.sum(q * kb, axis=-1, keepdims=True), 0.0) for q, kb in zip(qs, kbs)]
        for lev in range(1, nlev + 1):
            brs = [bs[lev * c:(lev + 1) * c] for bs in bss]
            ws = [(jnp.where(second[lev - 1], qs[h], kbs[h]) * jnp.exp(-jnp.abs(b[h] - brs[h]))).astype(BF16)
                  for h in hs]
            prods = [_dg(w, w, _NT) for w in ws]
            atts = [att + jnp.where(lmask[lev - 1], pr, 0.0) for att, pr in zip(atts, prods)]
        sts = [st_ref[h] for h in hs]
        o_inter = [_dot1(qs[h] * jnp.exp(b[h]), sts[h], _NT) for h in hs]
        o_intra = [_dg(atts[h].astype(BF16), vs[h]) for h in hs]
        bends = [b[h][c - 1:c] for h in hs]
        upd = [_dg(vs[h], (kbs[h] * jnp.exp(bends[h] - b[h])).astype(BF16), _TN) for h in hs]
        for h in hs:
            st_ref[h] = sts[h] * jnp.exp(bends[h]) + upd[h]
            o = o_inter[h] + o_intra[h]
            on = o * lax.rsqrt(jnp.mean(o * o, axis=-1, keepdims=True) + EPS) * nrm
            out = on * _silu(_pad_rows(g_ref[0, rows, sls[h]], c))
            o_ref[0, rows, sls[h]] = out[0:tc].astype(o_ref.dtype)

    _run_chunks(nchunks, c, chunk)

    @pl.when(pl.program_id(2) == pl.num_programs(2) - 1)
    def _():
        for h in range(nh):
            sfin_ref[0, h] = st_ref[h].T


def _hgrn(zb, lb, nrm, s0, layer, chunk):
    b, t, _ = zb.shape
    tb = min(t, MIXER_ROW_BLOCK)
    assert t % tb == 0
    c, tc, nchunks = _chunk_plan(tb, chunk)
    depth = lb.shape[0]
    nh = GLA_HEADS_PER_STEP
    ng = H_B // nh
    blk = (1, tb, nh * HD_B)
    sblk = (1, nh, HD_B, HD_B)
    return pl.pallas_call(
        functools.partial(_gla_kernel, layer=layer, c=c, tc=tc, nchunks=nchunks),
        grid=(b, ng, t // tb),
        in_specs=[pl.BlockSpec(blk, lambda bi, g, ti: (bi, ti, g)),
                  pl.BlockSpec(blk, lambda bi, g, ti: (bi, ti, ng + g)),
                  pl.BlockSpec(blk, lambda bi, g, ti: (bi, ti, 2 * ng + g)),
                  pl.BlockSpec(blk, lambda bi, g, ti: (bi, ti, 3 * ng + g)),
                  pl.BlockSpec((depth, 1, nh * HD_B), lambda bi, g, ti: (0, 0, g)),
                  pl.BlockSpec((1, HD_B), lambda bi, g, ti: (0, 0)),
                  pl.BlockSpec(sblk, lambda bi, g, ti: (bi, g, 0, 0))],
        out_specs=[pl.BlockSpec(blk, lambda bi, g, ti: (bi, ti, g)),
                   pl.BlockSpec(sblk, lambda bi, g, ti: (bi, g, 0, 0))],
        out_shape=[jax.ShapeDtypeStruct((b, t, W_B), BF16), jax.ShapeDtypeStruct((b, H_B, HD_B, HD_B), F32)],
        scratch_shapes=[pltpu.VMEM((nh, HD_B, HD_B), F32)],
        compiler_params=_cparams("parallel", "parallel", "arbitrary"),
        name="hgrn2",
    )(zb, zb, zb, zb, lb.reshape(depth, 1, W_B), nrm.reshape(1, HD_B), s0)


def _delta_kernel(q_ref, k_ref, v_ref, z_ref, zg_ref, wq_ref, wk_ref, wv_ref, bq_ref, bk_ref, bv_ref,
                  hq_ref, hk_ref, hv_ref, al_ref, dl_ref, nrm_ref, s0_ref, o_ref, sfin_ref, st_ref,
                  *, c, tc, nchunks):
    g = pl.program_id(1)
    first_block = pl.program_id(2) == 0
    nqk = DELTA_QK_HEADS_PER_STEP
    rep = H_C_V // H_C_QK
    sb = min(DELTA_SUB, c)
    row = lax.broadcasted_iota(jnp.int32, (c, c), 0)
    col = lax.broadcasted_iota(jnp.int32, (c, c), 1)
    lower_incl = _onehot(col <= row)
    eye = jnp.where(row == col, 1.0, 0.0)
    same_sub = ((row ^ col) & ~(sb - 1)) == 0
    valid = lax.broadcasted_iota(jnp.int32, (c, 1), 0) < tc
    lane = lax.broadcasted_iota(jnp.int32, (c, LANES), 1)
    nrm = nrm_ref[...]

    @pl.when(first_block)
    def _():
        for h in range(nqk * rep):
            st_ref[h] = s0_ref[0, h].T

    def conv(x_ref, w_ref, buf_ref, halo_ref, sl, ci, t0):
        x = x_ref[0, pl.ds(t0, tc), sl]
        halo = jnp.where(first_block, buf_ref[0, :, sl], halo_ref[0, :, sl])
        if nchunks > 1:
            prev = x_ref[0, pl.ds(pl.multiple_of(jnp.maximum(t0 - SUBLANES, 0), SUBLANES), SUBLANES), sl]
            halo = jnp.where(ci == 0, halo, prev)
        xh = jnp.concatenate([halo, x], axis=0)
        w = w_ref[:, sl]
        y = w[C_CONV - 1:C_CONV] * x
        for s in range(1, C_CONV):
            y = y + w[C_CONV - 1 - s:C_CONV - s] * pltpu.roll(xh, s, 0)[SUBLANES:]
        return _pad_rows(_silu(y), c)

    def l2n(x):
        return x * lax.rsqrt(jnp.sum(x * x, axis=-1, keepdims=True) + EPS)

    def chunk(ci, t0):
        rows = pl.ds(t0, tc)
        zg = _pad_rows(zg_ref[0, rows, :], c)
        beta_all = _sigmoid(zg)
        la_all = -jnp.exp(al_ref[...]) * _softplus(zg + dl_ref[...])
        nv = nqk * rep
        hs = range(nv)
        qsls = [slice(qh * HD_C, (qh + 1) * HD_C) for qh in range(nqk)]
        vsls = [slice(h * HD_C, (h + 1) * HD_C) for h in hs]
        qs = [l2n(conv(q_ref, wq_ref, bq_ref, hq_ref, sl, ci, t0)) * HD_C ** -0.5 for sl in qsls]
        ks = [l2n(conv(k_ref, wk_ref, bk_ref, hk_ref, sl, ci, t0)) for sl in qsls]
        if tc < c:
            ks = [jnp.where(valid, k, 0.0) for k in ks]
        qbs = [q.astype(BF16) for q in qs]
        kbs = [k.astype(BF16) for k in ks]
        kks = [_dg(kb, kb, _NT) for kb in kbs]
        qks = [_dg(qb, kb, _NT) for qb, kb in zip(qbs, kbs)]
        vs = [conv(v_ref, wv_ref, bv_ref, hv_ref, sl, ci, t0) for sl in vsls]
        betas = [jnp.sum(jnp.where(lane == g * nv + h, beta_all, 0.0), axis=-1, keepdims=True) for h in hs]
        las = [jnp.sum(jnp.where(lane == H_C_V + g * nv + h, la_all, 0.0), axis=-1, keepdims=True) for h in hs]
        if tc < c:
            betas = [jnp.where(valid, x, 0.0) for x in betas]
            las = [jnp.where(valid, x, 0.0) for x in las]
        bc_all = _sel_dot(lower_incl, jnp.concatenate([jnp.broadcast_to(la, (c, LANES)) for la in las], axis=1))
        bcols = [bc_all[:, h * LANES:h * LANES + c] for h in hs]
        decs = [jnp.exp(jnp.where(row >= col, bc - bc.T, NEG)) for bc in bcols]
        b1s = [bc[:, 0:1] for bc in bcols]
        ebs = [jnp.exp(b1) for b1 in b1s]
        bends = [b1[c - 1:c] for b1 in b1s]
        ns = [jnp.where(row > col, betas[h] * (kks[h // rep] * decs[h]), 0.0) for h in hs]
        nds = [jnp.where(same_sub, n, 0.0) for n in ns]
        xs = [eye - nd for nd in nds]
        ps = nds
        for _ in range(int(math.log2(sb)) - 1):
            ps = [_dot1(p, p) for p in ps]
            xs = [x + _dot1(x, p) for x, p in zip(xs, ps)]
        nblk = c // sb
        if nblk > 1:
            mms = [_dot1(x, n - nd) for x, n, nd in zip(xs, ns, nds)]
            ys = [eye - mm for mm in mms]
            ps = mms
            for _ in range(int(math.log2(nblk)) - 1):
                ps = [_dot1(p, p) for p in ps]
                ys = [y + _dot1(y, p) for y, p in zip(ys, ps)]
            xs = [_dot1(y, x) for y, x in zip(ys, xs)]
        sts = [st_ref[h] for h in hs]
        stbs = [st.astype(BF16) for st in sts]
        ksts = [_dg(kbs[h // rep], stbs[h], _NT) for h in hs]
        qsts = [_dg(qbs[h // rep], stbs[h], _NT) for h in hs]
        rhss = [betas[h] * (vs[h] - ebs[h] * ksts[h]) for h in hs]
        ubs = [_dot1(x, rhs).astype(BF16) for x, rhs in zip(xs, rhss)]
        o_intra = [_dg((qks[h // rep] * decs[h]).astype(BF16), ubs[h]) for h in hs]
        upd = [_dg(ubs[h], (ks[h // rep] * jnp.exp(bends[h] - b1s[h])).astype(BF16), _TN) for h in hs]
        for h in hs:
            st_ref[h] = jnp.exp(bends[h]) * sts[h] + upd[h]
            o = ebs[h] * qsts[h] + o_intra[h]
            on = o * lax.rsqrt(jnp.mean(o * o, axis=-1, keepdims=True) + EPS) * nrm
            out = on * _silu(_pad_rows(z_ref[0, rows, vsls[h]], c))
            o_ref[0, rows, vsls[h]] = out[0:tc].astype(o_ref.dtype)

    _run_chunks(nchunks, c, chunk)

    @pl.when(pl.program_id(2) == pl.num_programs(2) - 1)
    def _():
        for h in range(nqk * rep):
            sfin_ref[0, h] = st_ref[h].T


def _delta(zc, zg, conv_w, buf, a_log, dt_bias, nrm, s0, chunk):
    b, t, _ = zc.shape
    tb = min(t, MIXER_ROW_BLOCK)
    assert t % tb == 0 and tb % SUBLANES == 0
    c, tc, nchunks = _chunk_plan(tb, chunk)
    rep = H_C_V // H_C_QK
    nqk = DELTA_QK_HEADS_PER_STEP
    nv = nqk * rep
    ng = H_C_QK // nqk
    qw = nqk * HD_C
    vw = nv * HD_C
    assert W_CQK % qw == 0 and (2 * W_CQK) % vw == 0
    kq0 = W_CQK // qw
    v0 = 2 * W_CQK // vw
    z0 = v0 + W_CV // vw
    buf8 = jnp.pad(buf, ((0, 0), (SUBLANES - (C_CONV - 1), 0), (0, 0)))
    pad_l = jnp.zeros((H_C_V,), F32)
    a_lane = jnp.pad(jnp.concatenate([pad_l, a_log]), (0, LANES - 2 * H_C_V)).reshape(1, LANES)
    d_lane = jnp.pad(jnp.concatenate([pad_l, dt_bias]), (0, LANES - 2 * H_C_V)).reshape(1, LANES)
    hb = tb // SUBLANES

    def rows(col0):
        return lambda bi, g, ti: (bi, ti, col0 + g)

    def first(col0):
        return lambda bi, g, ti: (bi, 0, col0 + g)

    def halo(col0):
        return lambda bi, g, ti: (bi, jnp.maximum(ti * hb - 1, 0), col0 + g)

    full = lambda bi, g, ti: (0, 0)
    sblk = (1, nv, HD_C, HD_C)
    smap = lambda bi, g, ti: (bi, g, 0, 0)
    return pl.pallas_call(
        functools.partial(_delta_kernel, c=c, tc=tc, nchunks=nchunks),
        grid=(b, ng, t // tb),
        in_specs=[pl.BlockSpec((1, tb, qw), rows(0)), pl.BlockSpec((1, tb, qw), rows(kq0)),
                  pl.BlockSpec((1, tb, vw), rows(v0)), pl.BlockSpec((1, tb, vw), rows(z0)),
                  pl.BlockSpec((1, tb, W_CG), lambda bi, g, ti: (bi, ti, 0)),
                  pl.BlockSpec((C_CONV, qw), lambda bi, g, ti: (0, g)),
                  pl.BlockSpec((C_CONV, qw), lambda bi, g, ti: (0, kq0 + g)),
                  pl.BlockSpec((C_CONV, vw), lambda bi, g, ti: (0, v0 + g)),
                  pl.BlockSpec((1, SUBLANES, qw), first(0)), pl.BlockSpec((1, SUBLANES, qw), first(kq0)),
                  pl.BlockSpec((1, SUBLANES, vw), first(v0)),
                  pl.BlockSpec((1, SUBLANES, qw), halo(0)), pl.BlockSpec((1, SUBLANES, qw), halo(kq0)),
                  pl.BlockSpec((1, SUBLANES, vw), halo(v0)),
                  pl.BlockSpec((1, LANES), full), pl.BlockSpec((1, LANES), full),
                  pl.BlockSpec((1, HD_C), full),
                  pl.BlockSpec(sblk, smap)],
        out_specs=[pl.BlockSpec((1, tb, vw), rows(0)), pl.BlockSpec(sblk, smap)],
        out_shape=[jax.ShapeDtypeStruct((b, t, W_CV), BF16), jax.ShapeDtypeStruct((b, H_C_V, HD_C, HD_C), F32)],
        scratch_shapes=[pltpu.VMEM((nv, HD_C, HD_C), F32)],
        compiler_params=_cparams("parallel", "parallel", "arbitrary"),
        name="gated_deltanet",
    )(zc, zc, zc, zc, zg, conv_w, conv_w, conv_w, buf8, buf8, buf8, zc, zc, zc,
      a_lane, d_lane, nrm.reshape(1, HD_C), s0)


def _ret_kernel(q_ref, k_ref, v_ref, g_ref, nrm_ref, s0_ref, o_ref, sfin_ref, st_ref, *, c, tc, nchunks):
    nh = st_ref.shape[0]
    row = lax.broadcasted_iota(jnp.int32, (c, c), 0)
    col = lax.broadcasted_iota(jnp.int32, (c, c), 1)
    steps = (jnp.minimum(row + 1, tc) - jnp.minimum(col + 1, tc)).astype(F32)
    r1 = lax.broadcasted_iota(jnp.int32, (c, 1), 0)
    nsteps = jnp.minimum(r1 + 1, tc).astype(F32)
    valid = r1 < tc
    nrm = nrm_ref[...]
    lgs, decs = [], []
    for h in range(nh):
        hf = jnp.full((1, 1), pl.program_id(1) * nh + h, jnp.int32).astype(F32)
        lg = jnp.log1p(-jnp.exp2(-5.0 - hf))
        lgs.append(lg)
        decs.append(jnp.exp(jnp.where(row >= col, steps * lg, NEG)))
        st_ref[h] = s0_ref[0, h].T

    def chunk(ci, t0):
        rows = pl.ds(t0, tc)
        hs = range(nh)
        sls = [slice(h * HD_D, (h + 1) * HD_D) for h in hs]
        b1s = [nsteps * lgs[h] for h in hs]
        bends = [float(tc) * lgs[h] for h in hs]
        qs = [_pad_rows(q_ref[0, rows, sl], c).astype(BF16) for sl in sls]
        ks = [_pad_rows(k_ref[0, rows, sl], c) * HD_D ** -0.5 for sl in sls]
        if tc < c:
            ks = [jnp.where(valid, k, 0.0) for k in ks]
        vs = [_pad_rows(v_ref[0, rows, sl], c).astype(BF16) for sl in sls]
        sts = [st_ref[h] for h in hs]
        atts = [_dg(qs[h], ks[h].astype(BF16), _NT) * decs[h] for h in hs]
        o_inter = [_dg(qs[h], sts[h].astype(BF16), _NT) for h in hs]
        o_intra = [_dg(atts[h].astype(BF16), vs[h]) for h in hs]
        upd = [_dg(vs[h], (ks[h] * jnp.exp(bends[h] - b1s[h])).astype(BF16), _TN) for h in hs]
        for h in hs:
            sl = sls[h]
            st_ref[h] = jnp.exp(bends[h]) * sts[h] + upd[h]
            o = jnp.exp(b1s[h]) * o_inter[h] + o_intra[h]
            mu = jnp.mean(o, axis=-1, keepdims=True)
            oc = o - mu
            var = jnp.mean(oc * oc, axis=-1, keepdims=True)
            on = oc * lax.rsqrt(var + EPS) * nrm
            out = on * _silu(_pad_rows(g_ref[0, rows, sl], c))
            o_ref[0, rows, sl] = out[0:tc].astype(o_ref.dtype)

    _run_chunks(nchunks, c, chunk)
    for h in range(nh):
        sfin_ref[0, h] = st_ref[h].T


def _retention(zd, nrm, s0, chunk):
    b, t, _ = zd.shape
    c, tc, nchunks = _chunk_plan(t, chunk)
    nh = RET_HEADS_PER_STEP
    ng = H_D // nh
    blk = (1, t, nh * HD_D)
    sblk = (1, nh, HD_D, HD_D)
    return pl.pallas_call(
        functools.partial(_ret_kernel, c=c, tc=tc, nchunks=nchunks),
        grid=(b, ng),
        in_specs=[pl.BlockSpec(blk, lambda bi, g: (bi, 0, g)),
                  pl.BlockSpec(blk, lambda bi, g: (bi, 0, ng + g)),
                  pl.BlockSpec(blk, lambda bi, g: (bi, 0, 2 * ng + g)),
                  pl.BlockSpec(blk, lambda bi, g: (bi, 0, 3 * ng + g)),
                  pl.BlockSpec((1, HD_D), lambda bi, g: (0, 0)),
                  pl.BlockSpec(sblk, lambda bi, g: (bi, g, 0, 0))],
        out_specs=[pl.BlockSpec(blk, lambda bi, g: (bi, 0, g)),
                   pl.BlockSpec(sblk, lambda bi, g: (bi, g, 0, 0))],
        out_shape=[jax.ShapeDtypeStruct((b, t, W_D), BF16), jax.ShapeDtypeStruct((b, H_D, HD_D, HD_D), F32)],
        scratch_shapes=[pltpu.VMEM((nh, HD_D, HD_D), F32)],
        compiler_params=_cparams("parallel", "arbitrary"),
        name="retention",
    )(zd, zd, zd, zd, nrm.reshape(1, HD_D), s0)


def _block(x, pe, cache, s_hgrn, s_delta, buf_delta, s_ret, buf_ffn, p, layer):
    bsz, t, d = x.shape
    m = bsz * t
    x2 = x.reshape(m, d)
    hn = _rmsnorm(x2, p['attn_norm'][layer], BF16)
    za = _matmul(hn, p['w_in_a'], layer).reshape(bsz, t, -1)
    zb = _matmul(hn, p['w_in_b'], layer).reshape(bsz, t, -1)
    zc = _matmul(hn, p['w_in_c'], layer).reshape(bsz, t, -1)
    zg = _matmul(hn, p['w_in_g'], layer).reshape(bsz, t, -1)
    zd = _matmul(hn, p['w_in_d'], layer).reshape(bsz, t, -1)

    if cache is None:
        o_a = _attention_prompt(za)
        kv_new = za
    else:
        o_a = _attention_sample(za, cache, layer)
        kv_new = za[:, :, W_A:].reshape(bsz, t, 2, H_A, HD_A)

    o_b, s_hgrn_new = _hgrn(zb, p['hgrn_lb'], p['hgrn_norm'][layer], s_hgrn, layer, GLA_CHUNK)
    o_c, s_delta_new = _delta(zc, zg, p['delta_conv'][layer], buf_delta, p['delta_A_log'][layer],
                              p['delta_dt_bias'][layer], p['delta_norm'][layer], s_delta, DELTA_CHUNK)
    pre = jnp.concatenate([buf_delta, zc[:, :, :2 * W_CQK + W_CV]], axis=1)
    buf_delta_new = pre[:, pre.shape[1] - (C_CONV - 1):]
    o_d, s_ret_new = _retention(zd, p['ret_norm'][layer], s_ret, RET_CHUNK)

    parts = [o.reshape(m, -1) for o in (o_a, o_b, o_c, o_d)]
    x2 = _matmul_residual(parts, p['w_out'], layer, x2, (1024, 512, 256, 64), (512, 256, 128))

    hf = _rmsnorm(x2, p['ffn_norm'][layer], BF16)
    if cache is None:
        act, buf_ffn_new = _ffn_gate_up_prompt(hf, p['w_gate'], p['w_up'], p['ffn_conv'], layer, t)
    else:
        act, buf_ffn_new = _ffn_gate_up_sample(hf, p['w_gate'], p['w_up'], p['ffn_conv'], layer, buf_ffn, t)
    x2 = _matmul_residual([act], p['w_down'], layer, x2, (512, 256, 64), (256, 128))

    hp = _rmsnorm(x2, p['ple_norm'][layer], BF16)
    x2 = _ple(hp, p['ple_gate'], pe.reshape(m, -1).astype(BF16), p['ple_proj'], layer, x2)
    states = (kv_new, s_hgrn_new, s_delta_new, buf_delta_new, s_ret_new, buf_ffn_new)
    return x2.reshape(bsz, t, d), states


def kernel(x_prompt, x_sample, cache_attn_kv, state_hgrn, state_delta, state_delta_conv, state_ret,
           state_ffn_conv, p_prompt, p_sample, attn_norm, w_in, hgrn_lb, hgrn_norm, delta_conv,
           delta_A_log, delta_dt_bias, delta_norm, ret_norm, w_out, ffn_norm, w_gate, w_up, ffn_conv,
           w_down, ple_norm, ple_gate, ple_proj, final_norm):
    depth = w_in.shape[0]
    bp = x_prompt.shape[0]
    o_b = 3 * W_A
    o_c = o_b + 4 * W_B
    o_g = o_c + 2 * W_CQK + 2 * W_CV
    o_d = o_g + 2 * H_C_V
    w_in_t = jnp.swapaxes(w_in, 1, 2)
    w_in_d, w_in_g = _cast_tail_t(w_in_t, o_g, o_d - o_g)
    p = {'attn_norm': attn_norm,
         'w_in_a': _cast_bf16_t(w_in_t, 0, o_b), 'w_in_b': _cast_bf16_t(w_in_t, o_b, o_c - o_b),
         'w_in_c': _cast_bf16_t(w_in_t, o_c, o_g - o_c), 'w_in_g': w_in_g, 'w_in_d': w_in_d,
         'hgrn_lb': hgrn_lb, 'hgrn_norm': hgrn_norm, 'delta_conv': delta_conv, 'delta_A_log': delta_A_log,
         'delta_dt_bias': delta_dt_bias, 'delta_norm': delta_norm, 'ret_norm': ret_norm,
         'w_out': w_out, 'ffn_norm': ffn_norm, 'w_gate': w_gate, 'w_up': w_up, 'ffn_conv': ffn_conv,
         'w_down': _cast_bf16(w_down), 'ple_norm': ple_norm, 'ple_gate': ple_gate, 'ple_proj': ple_proj}
    xp, xs = x_prompt, x_sample
    st_p, st_s = [], []
    for l in range(depth):
        zero = lambda *s: jnp.zeros((bp,) + s, F32)
        xp, sp = _block(xp, p_prompt[l], None, zero(H_B, HD_B, HD_B), zero(H_C_V, HD_C, HD_C),
                        zero(C_CONV - 1, 2 * W_CQK + W_CV), zero(H_D, HD_D, HD_D), None, p, l)
        xs, ss = _block(xs, p_sample[l], cache_attn_kv, state_hgrn[l], state_delta[l], state_delta_conv[l],
                        state_ret[l], state_ffn_conv[l], p, l)
        st_p.append(sp)
        st_s.append(ss)

    def stack(sts, i):
        return jnp.stack([s[i] for s in sts])

    def final(x):
        return _rmsnorm(x.reshape(-1, x.shape[-1]), final_norm, F32).reshape(x.shape)

    kv_sample = _kv_shift(cache_attn_kv, stack(st_s, 0))
    s_p = x_prompt.shape[1]
    kv_prompt = _kv_pack([s[0] for s in st_p], min(A_BRANCHES[-1][0], s_p))
    return (final(xp), final(xs),
            kv_prompt, kv_sample, stack(st_p, 1), stack(st_s, 1), stack(st_p, 2), stack(st_s, 2),
            stack(st_p, 3), stack(st_s, 3), stack(st_p, 4), stack(st_s, 4), stack(st_p, 5), stack(st_s, 5))
```

```python
import functools
import math

import jax
import jax.numpy as jnp
from jax import lax
from jax.experimental import pallas as pl
from jax.experimental.pallas import tpu as pltpu

F32 = jnp.float32
BF16 = jnp.bfloat16
EPS = 1e-6
NEG = -1e30

H_A = 8
HD_A = 128
A_BRANCHES = ((128, 1), (512, 4), (2048, 16))
A_BLOCK = 128
H_B = 8
HD_B = 128
H_C_QK = 4
H_C_V = 8
HD_C = 128
C_CONV = 4
H_D = 4
HD_D = 256
FFN_CONV = 3

W_A = H_A * HD_A
W_B = H_B * HD_B
W_CQK = H_C_QK * HD_C
W_CV = H_C_V * HD_C
W_D = H_D * HD_D
W_CG = 128

VMEM_LIMIT_BYTES = 52 * 1024 * 1024
SUBLANES = 8
LANES = 128

GLA_HEADS_PER_STEP = 8
DELTA_QK_HEADS_PER_STEP = 4
RET_HEADS_PER_STEP = 2
MIXER_ROW_BLOCK = 1024
GLA_CHUNK = 64
DELTA_CHUNK = 128
RET_CHUNK = 256
DELTA_SUB = 16

_NN = (((1,), (0,)), ((), ()))
_NT = (((1,), (1,)), ((), ()))
_TN = (((0,), (0,)), ((), ()))


def _cparams(*sem):
    return pltpu.CompilerParams(dimension_semantics=sem, vmem_limit_bytes=VMEM_LIMIT_BYTES)


def _dg(a, b, dn=_NN):
    return lax.dot_general(a, b, dn, preferred_element_type=F32)


def _dot1(a, b, dn=_NN):
    return _dg(a.astype(BF16), b.astype(BF16), dn)


def _split3(x):
    x1 = x.astype(BF16)
    r1 = x - x1.astype(F32)
    x2 = r1.astype(BF16)
    x3 = (r1 - x2.astype(F32)).astype(BF16)
    return x1, x2, x3


def _sel_dot(m, x):
    x1, x2, x3 = _split3(x)
    return _dg(m, x1) + (_dg(m, x2) + _dg(m, x3))


def _sigmoid(x):
    return 1.0 / (1.0 + jnp.exp(-x))


def _silu(x):
    return x * _sigmoid(x)


def _softplus(x):
    return jnp.maximum(x, 0.0) + jnp.log1p(jnp.exp(-jnp.abs(x)))


def _pad_rows(x, rows):
    if x.shape[0] == rows:
        return x
    return jnp.concatenate([x, jnp.zeros((rows - x.shape[0],) + x.shape[1:], x.dtype)], axis=0)


def _onehot(cond):
    return jnp.where(cond, 1.0, 0.0).astype(BF16)


def _pick(n, prefs):
    for p in prefs:
        if n % p == 0:
            return p
    return n


def _rmsnorm_kernel(x_ref, g_ref, o_ref):
    x = x_ref[...]
    y = x * lax.rsqrt(jnp.mean(x * x, axis=-1, keepdims=True) + EPS)
    o_ref[...] = (y * g_ref[...]).astype(o_ref.dtype)


def _rmsnorm(x, g, out_dtype):
    m, d = x.shape
    tm = _pick(m, (256, 64, 8))
    return pl.pallas_call(
        _rmsnorm_kernel,
        grid=(m // tm,),
        in_specs=[pl.BlockSpec((tm, d), lambda i: (i, 0)), pl.BlockSpec((1, d), lambda i: (0, 0))],
        out_specs=pl.BlockSpec((tm, d), lambda i: (i, 0)),
        out_shape=jax.ShapeDtypeStruct((m, d), out_dtype),
        compiler_params=_cparams("parallel"),
        name="rmsnorm",
    )(x, g.reshape(1, d))


def _mm_kernel(a_ref, w_ref, o_ref):
    o_ref[...] = _dg(a_ref[...], w_ref[...].astype(BF16)).astype(o_ref.dtype)


def _matmul(a, w, layer, out_dtype=F32):
    m, k = a.shape
    n = w.shape[2]
    tm = _pick(m, (1024, 512, 256, 64))
    tn = _pick(n, (512, 256, 128))
    return pl.pallas_call(
        _mm_kernel,
        grid=(m // tm, n // tn),
        in_specs=[pl.BlockSpec((tm, k), lambda i, j: (i, 0)),
                  pl.BlockSpec((None, k, tn), lambda i, j: (layer, 0, j))],
        out_specs=pl.BlockSpec((tm, tn), lambda i, j: (i, j)),
        out_shape=jax.ShapeDtypeStruct((m, n), out_dtype),
        compiler_params=_cparams("parallel", "arbitrary"),
        name="matmul",
    )(a, w)


def _mm_res_kernel(*refs, nparts):
    a_refs = refs[:nparts]
    w_refs = refs[nparts:2 * nparts]
    x_ref = refs[2 * nparts]
    o_ref = refs[2 * nparts + 1]
    acc = x_ref[...]
    for a_ref, w_ref in zip(a_refs, w_refs):
        acc = acc + _dg(a_ref[...], w_ref[...].astype(BF16))
    o_ref[...] = acc


def _matmul_residual(parts, w, layer, x, tm_prefs, tn_prefs):
    m, n = x.shape
    kp = parts[0].shape[1]
    nparts = len(parts)
    tm = _pick(m, tm_prefs)
    tn = _pick(n, tn_prefs)

    def wmap(p):
        return lambda i, j: (layer, p, j)

    in_specs = [pl.BlockSpec((tm, kp), lambda i, j: (i, 0)) for _ in parts]
    in_specs += [pl.BlockSpec((None, kp, tn), wmap(p)) for p in range(nparts)]
    in_specs += [pl.BlockSpec((tm, tn), lambda i, j: (i, j))]
    return pl.pallas_call(
        functools.partial(_mm_res_kernel, nparts=nparts),
        grid=(m // tm, n // tn),
        in_specs=in_specs,
        out_specs=pl.BlockSpec((tm, tn), lambda i, j: (i, j)),
        out_shape=jax.ShapeDtypeStruct((m, n), F32),
        compiler_params=_cparams("parallel", "arbitrary"),
        name="matmul_residual",
    )(*parts, *([w] * nparts), x)


def _ple_kernel(a_ref, wg_ref, pe_ref, wp_ref, x_ref, o_ref):
    gate = _sigmoid(_dg(a_ref[...], wg_ref[...].astype(BF16)))
    o_ref[...] = x_ref[...] + gate * _dg(pe_ref[...], wp_ref[...].astype(BF16))


def _ple(hp, wg, pe, wp, layer, x):
    m, d = x.shape
    k = hp.shape[1]
    kp = pe.shape[1]
    tm = _pick(m, (1024, 512, 256, 64))
    tn = _pick(d, (512, 256, 128))
    return pl.pallas_call(
        _ple_kernel,
        grid=(m // tm, d // tn),
        in_specs=[pl.BlockSpec((tm, k), lambda i, j: (i, 0)),
                  pl.BlockSpec((None, k, tn), lambda i, j: (layer, 0, j)),
                  pl.BlockSpec((tm, kp), lambda i, j: (i, 0)),
                  pl.BlockSpec((None, kp, tn), lambda i, j: (layer, 0, j)),
                  pl.BlockSpec((tm, tn), lambda i, j: (i, j))],
        out_specs=pl.BlockSpec((tm, tn), lambda i, j: (i, j)),
        out_shape=jax.ShapeDtypeStruct((m, d), F32),
        compiler_params=_cparams("parallel", "arbitrary"),
        name="ple",
    )(hp, wg, pe, wp, x)


def _ffn_act(g, p1, p2, cw, u):
    gc = cw[0:1] * p2 + cw[1:2] * p1 + cw[2:3] * g
    return (_silu(gc) * u).astype(BF16)


def _ffn_gu_prompt_kernel(h_ref, halo_ref, wg_ref, wu_ref, cw_ref, wd_ref, o_ref, tail_ref, wdb_ref, *, seq):
    wdb_ref[...] = wd_ref[...].astype(wdb_ref.dtype)
    tm = h_ref.shape[0]
    a = h_ref[...]
    wg = wg_ref[...].astype(BF16)
    g = _dg(a, wg)
    u = _dg(a, wu_ref[...].astype(BF16))
    gh = _dg(halo_ref[...], wg)
    seq_start = lax.rem(pl.program_id(0) * tm, seq) == 0
    gh = jnp.where(seq_start, 0.0, gh)
    row = lax.broadcasted_iota(jnp.int32, g.shape, 0)
    p1 = jnp.where(row == 0, gh[7:8], pltpu.roll(g, 1, 0))
    p2 = jnp.where(row == 0, gh[6:7], jnp.where(row == 1, gh[7:8], pltpu.roll(g, 2, 0)))
    o_ref[...] = _ffn_act(g, p1, p2, cw_ref[...], u)
    tail_ref[0] = g[tm - SUBLANES:tm]


def _ffn_gate_up_prompt(hf, wg, wu, cw, wd, layer, seq):
    m, d = hf.shape
    f = wg.shape[2]
    tm = _pick(seq, (1024, 512, 256, 128, 64, 8))
    tf = _pick(f, (256, 128))
    hb = tm // SUBLANES
    nj = f // tf
    steps = (m // tm) * nj
    assert f % steps == 0 and (f // steps) % (2 * SUBLANES) == 0
    wd_rows = f // steps
    wmap = lambda i, j: (layer, 0, j)
    out, tail, wd_bf16 = pl.pallas_call(
        functools.partial(_ffn_gu_prompt_kernel, seq=seq),
        grid=(m // tm, nj),
        in_specs=[pl.BlockSpec((tm, d), lambda i, j: (i, 0)),
                  pl.BlockSpec((SUBLANES, d), lambda i, j: (jnp.maximum(i * hb - 1, 0), 0)),
                  pl.BlockSpec((None, d, tf), wmap),
                  pl.BlockSpec((None, d, tf), wmap),
                  pl.BlockSpec((None, FFN_CONV, tf), wmap),
                  pl.BlockSpec((None, wd_rows, d), lambda i, j: (layer, i * nj + j, 0))],
        out_specs=[pl.BlockSpec((tm, tf), lambda i, j: (i, j)),
                   pl.BlockSpec((1, SUBLANES, tf), lambda i, j: (i, 0, j)),
                   pl.BlockSpec((None, wd_rows, d), lambda i, j: (0, i * nj + j, 0))],
        out_shape=[jax.ShapeDtypeStruct((m, f), BF16),
                   jax.ShapeDtypeStruct((m // tm, SUBLANES, f), F32),
                   jax.ShapeDtypeStruct((1, f, d), BF16)],
        compiler_params=_cparams("parallel", "arbitrary"),
        name="ffn_gate_up_prompt",
    )(hf, hf, wg, wu, cw, wd)
    per_seq = seq // tm
    tail = tail.reshape(m // seq, per_seq, SUBLANES, f)[:, per_seq - 1, SUBLANES - (FFN_CONV - 1):]
    return out, tail, wd_bf16


def _ffn_gu_sample_kernel(h_ref, wg_ref, wu_ref, cw_ref, b1_ref, b2_ref, o_ref, g_ref, *, t):
    a = h_ref[...]
    g = _dg(a, wg_ref[...].astype(BF16))
    u = _dg(a, wu_ref[...].astype(BF16))
    pos = lax.rem(lax.broadcasted_iota(jnp.int32, g.shape, 0), t)
    p1 = jnp.where(pos == 0, b1_ref[...], pltpu.roll(g, 1, 0))
    p2 = jnp.where(pos < 2, b2_ref[...], pltpu.roll(g, 2, 0))
    o_ref[...] = _ffn_act(g, p1, p2, cw_ref[...], u)
    g_ref[...] = g


def _ffn_gate_up_sample(hf, wg, wu, cw, layer, buf, t):
    m, d = hf.shape
    f = wg.shape[2]
    nb = m // t
    tf = _pick(f, (256, 128))
    zeros = jnp.zeros((nb, t - 1, f), F32)
    b1 = jnp.concatenate([buf[:, 1:2], zeros], axis=1).reshape(m, f)
    b2 = jnp.concatenate([buf[:, 0:2], zeros[:, 1:]], axis=1).reshape(m, f)
    wmap = lambda j: (layer, 0, j)
    out, g = pl.pallas_call(
        functools.partial(_ffn_gu_sample_kernel, t=t),
        grid=(f // tf,),
        in_specs=[pl.BlockSpec((m, d), lambda j: (0, 0)),
                  pl.BlockSpec((None, d, tf), wmap),
                  pl.BlockSpec((None, d, tf), wmap),
                  pl.BlockSpec((None, FFN_CONV, tf), wmap),
                  pl.BlockSpec((m, tf), lambda j: (0, j)),
                  pl.BlockSpec((m, tf), lambda j: (0, j))],
        out_specs=[pl.BlockSpec((m, tf), lambda j: (0, j)), pl.BlockSpec((m, tf), lambda j: (0, j))],
        out_shape=[jax.ShapeDtypeStruct((m, f), BF16), jax.ShapeDtypeStruct((m, f), F32)],
        compiler_params=_cparams("arbitrary"),
        name="ffn_gate_up_sample",
    )(hf, wg, wu, cw, b1, b2)
    tail = g.reshape(nb, t, f)[:, t - (FFN_CONV - 1):]
    return out, tail


ATTN_HEADS_PER_STEP = 4


def _band_softmax_many(probs, dist_cf, dist_pf, valid_c):
    scs = [_dot1(pr[0], pr[1], _NT) for pr in probs]
    sps = [None if pr[2] is None else _dot1(pr[0], pr[2], _NT) for pr in probs]
    pcs, pps, ls, lses = [], [], [], []
    for pr, sc, sp in zip(probs, scs, sps):
        slope, prev_ok = pr[5], pr[6]
        sc = jnp.where(valid_c, sc - slope * dist_cf, NEG)
        if sp is None:
            m = jnp.max(sc, axis=-1, keepdims=True)
            pc = jnp.exp(sc - m)
            pp = None
            l = jnp.sum(pc, axis=-1, keepdims=True)
        else:
            sp = jnp.where(prev_ok, sp - slope * dist_pf, NEG)
            m = jnp.max(jnp.maximum(sc, sp), axis=-1, keepdims=True)
            pc = jnp.exp(sc - m)
            pp = jnp.exp(sp - m)
            l = jnp.sum(pc + pp, axis=-1, keepdims=True)
        pcs.append(pc)
        pps.append(pp)
        ls.append(l)
        lses.append(m + jnp.log(l))
    ocs = [_dot1(pc, pr[3]) for pc, pr in zip(pcs, probs)]
    ops = [None if pp is None else _dot1(pp, pr[4]) for pp, pr in zip(pps, probs)]
    outs = [(oc if op is None else oc + op) / l for oc, op, l in zip(ocs, ops, ls)]
    return outs, lses


def _band_kernel(*refs, s):
    nh = ATTN_HEADS_PER_STEP
    q_refs, k_refs, v_refs = refs[0:nh], refs[nh:2 * nh], refs[2 * nh:3 * nh]
    o_ref, o2_ref, o3_ref, l2_ref, l3_ref = refs[3 * nh:]
    nq = A_BLOCK
    hg = pl.program_id(1)
    qi = lax.broadcasted_iota(jnp.int32, (nq, nq), 0)
    kj = lax.broadcasted_iota(jnp.int32, (nq, nq), 1)
    dist_c = qi - kj
    dist_p = dist_c + nq
    valid_c = dist_c >= 0
    dist_cf = dist_c.astype(F32)
    dist_pf = dist_p.astype(F32)
    (w1, d1), (w2, d2), (w3, d3) = A_BRANCHES
    assert d1 == 1 and s % (d2 * nq) == 0 and s == d3 * nq and max(w1 // d1, w2 // d2, w3 // d3) <= nq

    def problem(h, rows_c, rows_p, has_prev, band, dil):
        hv = jnp.full((1, 1), hg * nh + h + 1, jnp.int32).astype(F32)
        slope = float(dil) * jnp.exp2(-8.0 * hv / H_A)
        q = q_refs[h][0, rows_c, :] * HD_A ** -0.5
        if rows_p is None:
            return (q, k_refs[h][0, rows_c, :], None, v_refs[h][0, rows_c, :], None, slope, None)
        return (q, k_refs[h][0, rows_c, :], k_refs[h][0, rows_p, :], v_refs[h][0, rows_c, :],
                v_refs[h][0, rows_p, :], slope, (dist_p <= band) & has_prev)

    def dilated(i, carry):
        rows3 = pl.ds(i, nq, stride=d3)
        r = lax.rem(i, d2)
        b = i // d2
        rows2 = pl.ds(b * (d2 * nq) + r, nq, stride=d2)
        rows2p = pl.ds(jnp.maximum(b - 1, 0) * (d2 * nq) + r, nq, stride=d2)
        probs = [problem(h, rows3, None, False, w3 // d3, d3) for h in range(nh)]
        probs += [problem(h, rows2, rows2p, b > 0, w2 // d2, d2) for h in range(nh)]
        outs, lses = _band_softmax_many(probs, dist_cf, dist_pf, valid_c)
        for h in range(nh):
            o3_ref[h, rows3, :] = outs[h]
            l3_ref[h, rows3, :] = jnp.broadcast_to(lses[h], (nq, HD_A))
            o2_ref[h, rows2, :] = outs[nh + h]
            l2_ref[h, rows2, :] = jnp.broadcast_to(lses[nh + h], (nq, HD_A))
        return carry

    lax.fori_loop(0, s // nq, dilated, 0)

    def dense(i, carry):
        blocks = []
        for j in range(2):
            bi = 2 * i + j
            rows = pl.ds(pl.multiple_of(bi * nq, nq), nq)
            rows_p = pl.ds(pl.multiple_of(jnp.maximum(bi - 1, 0) * nq, nq), nq)
            blocks += [(h, rows, problem(h, rows, rows_p, bi > 0, w1 // d1, d1)) for h in range(nh)]
        outs, lses = _band_softmax_many([blk[2] for blk in blocks], dist_cf, dist_pf, valid_c)
        for (h, rows, _), o1, l1 in zip(blocks, outs, lses):
            l2 = l2_ref[h, rows, :]
            l3 = l3_ref[h, rows, :]
            mx = jnp.maximum(l1, jnp.maximum(l2, l3))
            e1 = jnp.exp(l1 - mx)
            e2 = jnp.exp(l2 - mx)
            e3 = jnp.exp(l3 - mx)
            o = (e1 * o1 + e2 * o2_ref[h, rows, :] + e3 * o3_ref[h, rows, :]) / (e1 + e2 + e3)
            o_ref[0, rows, h * HD_A:(h + 1) * HD_A] = o.astype(o_ref.dtype)
        return carry

    assert (s // nq) % 2 == 0
    lax.fori_loop(0, s // (2 * nq), dense, 0)


def _attention_prompt(za):
    b, s, _ = za.shape
    nh = ATTN_HEADS_PER_STEP
    ng = H_A // nh
    blk = (1, s, HD_A)

    def head_spec(first, h):
        return pl.BlockSpec(blk, lambda bi, g: (bi, 0, first + g * nh + h))

    in_specs = [head_spec(part * H_A, h) for part in range(3) for h in range(nh)]
    return pl.pallas_call(
        functools.partial(_band_kernel, s=s),
        grid=(b, ng),
        in_specs=in_specs,
        out_specs=pl.BlockSpec((1, s, nh * HD_A), lambda bi, g: (bi, 0, g)),
        out_shape=jax.ShapeDtypeStruct((b, s, W_A), BF16),
        scratch_shapes=[pltpu.VMEM((nh, s, HD_A), F32) for _ in range(4)],
        compiler_params=_cparams("parallel", "arbitrary"),
        name="band_attention",
    )(*([za] * (3 * nh)))


def _attn_sample_kernel(q_ref, kn_ref, vn_ref, c_ref, o_ref):
    t = q_ref.shape[1]
    rows_per_pos = 2 * H_A
    n_past = c_ref.shape[0] // rows_per_pos
    tp = 2 * SUBLANES
    shape_c = (tp, n_past)
    shape_n = (tp, tp)
    dist_c = n_past + lax.broadcasted_iota(jnp.int32, shape_c, 0) - lax.broadcasted_iota(jnp.int32, shape_c, 1)
    dist_n = lax.broadcasted_iota(jnp.int32, shape_n, 0) - lax.broadcasted_iota(jnp.int32, shape_n, 1)
    new_ok = lax.broadcasted_iota(jnp.int32, shape_n, 1) < t
    dist_cf = dist_c.astype(F32)
    dist_nf = dist_n.astype(F32)
    oks = []
    for window, dil in A_BRANCHES:
        oks.append(((dist_c <= window) & (lax.rem(dist_c, dil) == 0),
                    (dist_n >= 0) & (dist_n <= window) & (lax.rem(dist_n, dil) == 0) & new_ok))
    hs = range(H_A)
    slopes = [2.0 ** (-8.0 * (h + 1) / H_A) for h in hs]
    sls = [slice(h * HD_A, (h + 1) * HD_A) for h in hs]
    qs = [_pad_rows(q_ref[0, :, sl] * HD_A ** -0.5, tp).astype(BF16) for sl in sls]
    kns = [_pad_rows(kn_ref[0, :, sl], tp).astype(BF16) for sl in sls]
    vns = [_pad_rows(vn_ref[0, :, sl], tp).astype(BF16) for sl in sls]
    kcs = [c_ref[pl.ds(h, n_past, stride=rows_per_pos), :].astype(BF16) for h in hs]
    vcs = [c_ref[pl.ds(H_A + h, n_past, stride=rows_per_pos), :].astype(BF16) for h in hs]
    b_cs = [_dg(qs[h], kcs[h], _NT) - slopes[h] * dist_cf for h in hs]
    b_ns = [_dg(qs[h], kns[h], _NT) - slopes[h] * dist_nf for h in hs]
    pcs, pns, ls, lses = [], [], [], []
    for h in hs:
        for ok_c, ok_n in oks:
            sc = jnp.where(ok_c, b_cs[h], NEG)
            sn = jnp.where(ok_n, b_ns[h], NEG)
            m = jnp.maximum(jnp.max(sc, axis=-1, keepdims=True), jnp.max(sn, axis=-1, keepdims=True))
            pc = jnp.exp(sc - m)
            pn = jnp.exp(sn - m)
            pcs.append(pc.astype(BF16))
            pns.append(pn.astype(BF16))
            ls.append(jnp.sum(pc, axis=-1, keepdims=True) + jnp.sum(pn, axis=-1, keepdims=True))
            lses.append(m + jnp.log(ls[-1]))
    nbr = len(oks)
    outs = [(_dg(pcs[i], vcs[i // nbr]) + _dg(pns[i], vns[i // nbr])) / ls[i] for i in range(len(pcs))]
    for h in hs:
        lse_h = lses[h * nbr:(h + 1) * nbr]
        out_h = outs[h * nbr:(h + 1) * nbr]
        mx = functools.reduce(jnp.maximum, lse_h)
        es = [jnp.exp(x - mx) for x in lse_h]
        tot = functools.reduce(lambda a, b: a + b, es)
        o = functools.reduce(lambda a, b: a + b, [e * x for e, x in zip(es, out_h)]) / tot
        o_ref[0, :, sls[h]] = o[:t].astype(o_ref.dtype)


def _attention_sample(za, cache, layer):
    b, t, _ = za.shape
    depth, _, l = cache.shape[:3]
    rows = l * 2 * H_A
    cache2d = cache.reshape(depth, b, rows, HD_A)
    return pl.pallas_call(
        _attn_sample_kernel,
        grid=(b,),
        in_specs=[pl.BlockSpec((1, t, W_A), lambda bi: (bi, 0, 0)),
                  pl.BlockSpec((1, t, W_A), lambda bi: (bi, 0, 1)),
                  pl.BlockSpec((1, t, W_A), lambda bi: (bi, 0, 2)),
                  pl.BlockSpec((None, None, rows, HD_A), lambda bi: (layer, bi, 0, 0))],
        out_specs=pl.BlockSpec((1, t, W_A), lambda bi: (bi, 0, 0)),
        out_shape=jax.ShapeDtypeStruct((b, t, W_A), BF16),
        compiler_params=_cparams("parallel"),
        name="attention_sample",
    )(za, za, za, cache2d)


def _cast_kernel(x_ref, o_ref):
    o_ref[...] = x_ref[...].astype(o_ref.dtype)


CAST_BLOCK_BYTES = 6 * 1024 * 1024


def _cast_bf16(w, col0=0, ncols=None):
    depth, k, n = w.shape
    ncols = n - col0 if ncols is None else ncols
    tn = next(c for c in (1024, 512, 256, LANES) if ncols % c == 0 and col0 % c == 0)
    packed_rows = 2 * SUBLANES
    tk = max(d for d in range(packed_rows, k + 1, packed_rows) if k % d == 0 and d * tn * 4 <= CAST_BLOCK_BYTES)
    c0 = col0 // tn
    return pl.pallas_call(
        _cast_kernel,
        grid=(depth, k // tk, ncols // tn),
        in_specs=[pl.BlockSpec((1, tk, tn), lambda l, i, j: (l, i, c0 + j))],
        out_specs=pl.BlockSpec((1, tk, tn), lambda l, i, j: (l, i, j)),
        out_shape=jax.ShapeDtypeStruct((depth, k, ncols), BF16),
        compiler_params=_cparams("parallel", "parallel", "parallel"),
        name="cast_bf16",
    )(w)


def _cast_t_kernel(x_ref, o_ref):
    o_ref[0] = x_ref[0].T.astype(o_ref.dtype)


def _cast_bf16_t(wt, row0, nrows, tn=512):
    depth, _, k = wt.shape
    assert row0 % tn == 0 and nrows % tn == 0
    tk = 1024 if k % 1024 == 0 else k
    r0 = row0 // tn
    return pl.pallas_call(
        _cast_t_kernel,
        grid=(depth, k // tk, nrows // tn),
        in_specs=[pl.BlockSpec((1, tn, tk), lambda l, i, j: (l, r0 + j, i))],
        out_specs=pl.BlockSpec((1, tk, tn), lambda l, i, j: (l, i, j)),
        out_shape=jax.ShapeDtypeStruct((depth, k, nrows), BF16),
        compiler_params=_cparams("parallel", "parallel", "parallel"),
        name="cast_bf16_t",
    )(wt)


def _cast_tail_t_kernel(a_ref, b_ref, d_ref, g_ref, *, shift):
    a = a_ref[0]
    tn = a.shape[0]
    x = jnp.concatenate([a, b_ref[0]], axis=0)
    d_ref[0] = x[shift:shift + tn].T.astype(d_ref.dtype)

    @pl.when(pl.program_id(2) == 0)
    def _():
        lane = lax.broadcasted_iota(jnp.int32, (a.shape[1], LANES), 1)
        g_ref[0] = jnp.where(lane < shift, a[0:LANES].T, 0.0).astype(g_ref.dtype)


def _cast_tail_t(wt, row0, shift, tn=512):
    depth, n, k = wt.shape
    nrows = n - row0 - shift
    packed_rows = 2 * SUBLANES
    assert row0 % tn == 0 and nrows % tn == 0 and shift % packed_rows == 0 and 0 < shift <= LANES
    tk = 1024 if k % 1024 == 0 else k
    r0 = row0 // tn
    per = tn // shift
    return pl.pallas_call(
        functools.partial(_cast_tail_t_kernel, shift=shift),
        grid=(depth, k // tk, nrows // tn),
        in_specs=[pl.BlockSpec((1, tn, tk), lambda l, i, j: (l, r0 + j, i)),
                  pl.BlockSpec((1, shift, tk), lambda l, i, j: (l, (r0 + j + 1) * per, i))],
        out_specs=[pl.BlockSpec((1, tk, tn), lambda l, i, j: (l, i, j)),
                   pl.BlockSpec((1, tk, LANES), lambda l, i, j: (l, i, 0))],
        out_shape=[jax.ShapeDtypeStruct((depth, k, nrows), BF16), jax.ShapeDtypeStruct((depth, k, LANES), BF16)],
        compiler_params=_cparams("parallel", "parallel", "arbitrary"),
        name="cast_tail_t",
    )(wt, wt)


def _kv_pack_kernel(*refs):
    o_ref = refs[-1]
    depth = (len(refs) - 1) // 2
    w = refs[0].shape[2]
    for l in range(depth):
        @pl.when(pl.program_id(0) == l)
        def _(l=l):
            o_ref[0, :, 0:w] = refs[2 * l][0]
            o_ref[0, :, w:2 * w] = refs[2 * l + 1][0]


def _kv_pack(zas, rows):
    depth = len(zas)
    b, s, _ = zas[0].shape
    ts = _pick(rows, (1024, 512, 256, 128, 64, 8))
    first = (s - rows) // ts
    assert (s - rows) % ts == 0
    in_specs, args = [], []
    for l in range(depth):
        for part in (1, 2):
            def imap(d, bi, si, l=l, part=part):
                return (bi, jnp.where(d == l, first + si, first), part)
            in_specs.append(pl.BlockSpec((1, ts, W_A), imap))
            args.append(zas[l])
    out = pl.pallas_call(
        _kv_pack_kernel,
        grid=(depth, b, rows // ts),
        in_specs=in_specs,
        out_specs=pl.BlockSpec((None, 1, ts, 2 * W_A), lambda d, bi, si: (d, bi, si, 0)),
        out_shape=jax.ShapeDtypeStruct((depth, b, rows, 2 * W_A), F32),
        compiler_params=_cparams("parallel", "parallel", "parallel"),
        name="kv_pack",
    )(*args)
    return out.reshape(depth, b, rows, 2, H_A, HD_A)


def _kv_shift_kernel(c_ref, n_ref, o_ref):
    l = c_ref.shape[0]
    t = n_ref.shape[0]
    o_ref[0:l - t] = c_ref[t:l]
    o_ref[l - t:l] = n_ref[...]


def _kv_shift(cache, kv_new):
    depth, b, l, two, h, hd = cache.shape
    t = kv_new.shape[2]
    cblk = (None, None, l, None, h, hd)
    cmap = lambda d, bi, s: (d, bi, 0, s, 0, 0)
    return pl.pallas_call(
        _kv_shift_kernel,
        grid=(depth, b, two),
        in_specs=[pl.BlockSpec(cblk, cmap), pl.BlockSpec((None, None, t, None, h, hd), cmap)],
        out_specs=pl.BlockSpec(cblk, cmap),
        out_shape=jax.ShapeDtypeStruct(cache.shape, cache.dtype),
        compiler_params=_cparams("parallel", "parallel", "parallel"),
        name="kv_shift",
    )(cache, kv_new)


def _chunk_plan(t, chunk):
    c = chunk if t >= chunk else max(2 * SUBLANES, t)
    assert t % c == 0 or t < c
    tc = min(t, c)
    return c, tc, max(t // c, 1)


def _run_chunks(nchunks, c, chunk):
    if nchunks == 1:
        chunk(0, 0)
    else:
        def body(ci, carry):
            chunk(ci, pl.multiple_of(ci * c, c))
            return carry
        lax.fori_loop(0, nchunks, body, 0)


def _gla_kernel(q_ref, f_ref, i_ref, g_ref, lb_ref, nrm_ref, s0_ref, o_ref, sfin_ref, st_ref, *, layer, c, tc, nchunks):
    nlev = int(math.log2(c))
    assert 1 << nlev == c
    depth = lb_ref.shape[0]
    nh = st_ref.shape[0]
    lbs = [lb_ref[i] for i in range(depth)]
    mx = functools.reduce(jnp.maximum, lbs)
    es = [jnp.exp(x - mx) for x in lbs]
    tot = functools.reduce(lambda a, b: a + b, es)
    lower = functools.reduce(lambda a, b: a + b, [es[i] / tot for i in range(layer + 1)]) - es[0] / tot
    one_minus_lb = 1.0 - lower

    row = lax.broadcasted_iota(jnp.int32, (c, c), 0)
    col = lax.broadcasted_iota(jnp.int32, (c, c), 1)
    prefix = [_onehot(col <= row)]
    lmask, second = [], []
    row_hd = lax.broadcasted_iota(jnp.int32, (c, HD_B), 0)
    for lev in range(1, nlev + 1):
        s = c >> lev
        grp = ~(2 * s - 1)
        prefix.append(_onehot(col <= (row & grp) + (s - 1)))
        lmask.append((((row ^ col) & grp) == 0) & ((row & s) != 0) & ((col & s) == 0))
        second.append((row_hd & s) != 0)
    mstack = jnp.concatenate(prefix, axis=0)
    eye = row == col
    valid = lax.broadcasted_iota(jnp.int32, (c, 1), 0) < tc
    nrm = nrm_ref[...]

    @pl.when(pl.program_id(2) == 0)
    def _():
        for h in range(nh):
            st_ref[h] = s0_ref[0, h].T

    def chunk(ci, t0):
        rows = pl.ds(t0, tc)
        hs = range(nh)
        sls = [slice(h * HD_B, (h + 1) * HD_B) for h in hs]
        qs = [_silu(_pad_rows(q_ref[0, rows, sl], c)) for sl in sls]
        kbs = [one_minus_lb[:, sl] * _sigmoid(-_pad_rows(f_ref[0, rows, sl], c)) for sl in sls]
        logfs = [jnp.log1p(-kb) for kb in kbs]
        if tc < c:
            kbs = [jnp.where(valid, kb, 0.0) for kb in kbs]
            logfs = [jnp.where(valid, lf, 0.0) for lf in logfs]
        vs = [_pad_rows(i_ref[0, rows, sl], c).astype(BF16) for sl in sls]
        bs_all = _sel_dot(mstack, jnp.concatenate(logfs, axis=1))
        bss = [bs_all[:, sl] for sl in sls]
        b = [bs[0:c] for bs in bss]
        atts = [jnp.where(eye, jnp.sum(q * kb, axis=-1, keepdims=True), 0.0) for q, kb in zip(qs, kbs)]
        for lev in range(1, nlev + 1):
            brs = [bs[lev * c:(lev + 1) * c] for bs in bss]
            ws = [(jnp.where(second[lev - 1], qs[h], kbs[h]) * jnp.exp(-jnp.abs(b[h] - brs[h]))).astype(BF16)
                  for h in hs]
            prods = [_dg(w, w, _NT) for w in ws]
            atts = [att + jnp.where(lmask[lev - 1], pr, 0.0) for att, pr in zip(atts, prods)]
        sts = [st_ref[h] for h in hs]
        o_inter = [_dot1(qs[h] * jnp.exp(b[h]), sts[h], _NT) for h in hs]
        o_intra = [_dg(atts[h].astype(BF16), vs[h]) for h in hs]
        bends = [b[h][c - 1:c] for h in hs]
        upd = [_dg(vs[h], (kbs[h] * jnp.exp(bends[h] - b[h])).astype(BF16), _TN) for h in hs]
        for h in hs:
            st_ref[h] = sts[h] * jnp.exp(bends[h]) + upd[h]
            o = o_inter[h] + o_intra[h]
            on = o * lax.rsqrt(jnp.mean(o * o, axis=-1, keepdims=True) + EPS) * nrm
            out = on * _silu(_pad_rows(g_ref[0, rows, sls[h]], c))
            o_ref[0, rows, sls[h]] = out[0:tc].astype(o_ref.dtype)

    _run_chunks(nchunks, c, chunk)

    @pl.when(pl.program_id(2) == pl.num_programs(2) - 1)
    def _():
        for h in range(nh):
            sfin_ref[0, h] = st_ref[h].T


def _hgrn(zb, lb, nrm, s0, layer, chunk):
    b, t, _ = zb.shape
    tb = min(t, MIXER_ROW_BLOCK)
    assert t % tb == 0
    c, tc, nchunks = _chunk_plan(tb, chunk)
    depth = lb.shape[0]
    nh = GLA_HEADS_PER_STEP
    ng = H_B // nh
    blk = (1, tb, nh * HD_B)
    sblk = (1, nh, HD_B, HD_B)
    return pl.pallas_call(
        functools.partial(_gla_kernel, layer=layer, c=c, tc=tc, nchunks=nchunks),
        grid=(b, ng, t // tb),
        in_specs=[pl.BlockSpec(blk, lambda bi, g, ti: (bi, ti, g)),
                  pl.BlockSpec(blk, lambda bi, g, ti: (bi, ti, ng + g)),
                  pl.BlockSpec(blk, lambda bi, g, ti: (bi, ti, 2 * ng + g)),
                  pl.BlockSpec(blk, lambda bi, g, ti: (bi, ti, 3 * ng + g)),
                  pl.BlockSpec((depth, 1, nh * HD_B), lambda bi, g, ti: (0, 0, g)),
                  pl.BlockSpec((1, HD_B), lambda bi, g, ti: (0, 0)),
                  pl.BlockSpec(sblk, lambda bi, g, ti: (bi, g, 0, 0))],
        out_specs=[pl.BlockSpec(blk, lambda bi, g, ti: (bi, ti, g)),
                   pl.BlockSpec(sblk, lambda bi, g, ti: (bi, g, 0, 0))],
        out_shape=[jax.ShapeDtypeStruct((b, t, W_B), BF16), jax.ShapeDtypeStruct((b, H_B, HD_B, HD_B), F32)],
        scratch_shapes=[pltpu.VMEM((nh, HD_B, HD_B), F32)],
        compiler_params=_cparams("parallel", "parallel", "arbitrary"),
        name="hgrn2",
    )(zb, zb, zb, zb, lb.reshape(depth, 1, W_B), nrm.reshape(1, HD_B), s0)


def _delta_kernel(q_ref, k_ref, v_ref, z_ref, zg_ref, wq_ref, wk_ref, wv_ref, bq_ref, bk_ref, bv_ref,
                  hq_ref, hk_ref, hv_ref, al_ref, dl_ref, nrm_ref, s0_ref, o_ref, sfin_ref, st_ref,
                  *, c, tc, nchunks):
    g = pl.program_id(1)
    first_block = pl.program_id(2) == 0
    nqk = DELTA_QK_HEADS_PER_STEP
    rep = H_C_V // H_C_QK
    sb = min(DELTA_SUB, c)
    row = lax.broadcasted_iota(jnp.int32, (c, c), 0)
    col = lax.broadcasted_iota(jnp.int32, (c, c), 1)
    lower_incl = _onehot(col <= row)
    eye = jnp.where(row == col, 1.0, 0.0)
    same_sub = ((row ^ col) & ~(sb - 1)) == 0
    valid = lax.broadcasted_iota(jnp.int32, (c, 1), 0) < tc
    lane = lax.broadcasted_iota(jnp.int32, (c, LANES), 1)
    nrm = nrm_ref[...]

    @pl.when(first_block)
    def _():
        for h in range(nqk * rep):
            st_ref[h] = s0_ref[0, h].T

    def conv(x_ref, w_ref, buf_ref, halo_ref, sl, ci, t0):
        x = x_ref[0, pl.ds(t0, tc), sl]
        halo = jnp.where(first_block, buf_ref[0, :, sl], halo_ref[0, :, sl])
        if nchunks > 1:
            prev = x_ref[0, pl.ds(pl.multiple_of(jnp.maximum(t0 - SUBLANES, 0), SUBLANES), SUBLANES), sl]
            halo = jnp.where(ci == 0, halo, prev)
        xh = jnp.concatenate([halo, x], axis=0)
        w = w_ref[:, sl]
        y = w[C_CONV - 1:C_CONV] * x
        for s in range(1, C_CONV):
            y = y + w[C_CONV - 1 - s:C_CONV - s] * pltpu.roll(xh, s, 0)[SUBLANES:]
        return _pad_rows(_silu(y), c)

    def l2n(x):
        return x * lax.rsqrt(jnp.sum(x * x, axis=-1, keepdims=True) + EPS)

    def chunk(ci, t0):
        rows = pl.ds(t0, tc)
        zg = _pad_rows(zg_ref[0, rows, :], c)
        beta_all = _sigmoid(zg)
        la_all = -jnp.exp(al_ref[...]) * _softplus(zg + dl_ref[...])
        nv = nqk * rep
        hs = range(nv)
        qsls = [slice(qh * HD_C, (qh + 1) * HD_C) for qh in range(nqk)]
        vsls = [slice(h * HD_C, (h + 1) * HD_C) for h in hs]
        qs = [l2n(conv(q_ref, wq_ref, bq_ref, hq_ref, sl, ci, t0)) * HD_C ** -0.5 for sl in qsls]
        ks = [l2n(conv(k_ref, wk_ref, bk_ref, hk_ref, sl, ci, t0)) for sl in qsls]
        if tc < c:
            ks = [jnp.where(valid, k, 0.0) for k in ks]
        qbs = [q.astype(BF16) for q in qs]
        kbs = [k.astype(BF16) for k in ks]
        kks = [_dg(kb, kb, _NT) for kb in kbs]
        qks = [_dg(qb, kb, _NT) for qb, kb in zip(qbs, kbs)]
        vs = [conv(v_ref, wv_ref, bv_ref, hv_ref, sl, ci, t0) for sl in vsls]
        betas = [jnp.sum(jnp.where(lane == g * nv + h, beta_all, 0.0), axis=-1, keepdims=True) for h in hs]
        las = [jnp.sum(jnp.where(lane == H_C_V + g * nv + h, la_all, 0.0), axis=-1, keepdims=True) for h in hs]
        if tc < c:
            betas = [jnp.where(valid, x, 0.0) for x in betas]
            las = [jnp.where(valid, x, 0.0) for x in las]
        bc_all = _sel_dot(lower_incl, jnp.concatenate([jnp.broadcast_to(la, (c, LANES)) for la in las], axis=1))
        bcols = [bc_all[:, h * LANES:h * LANES + c] for h in hs]
        decs = [jnp.exp(jnp.where(row >= col, bc - bc.T, NEG)) for bc in bcols]
        b1s = [bc[:, 0:1] for bc in bcols]
        ebs = [jnp.exp(b1) for b1 in b1s]
        bends = [b1[c - 1:c] for b1 in b1s]
        ns = [jnp.where(row > col, betas[h] * (kks[h // rep] * decs[h]), 0.0) for h in hs]
        nds = [jnp.where(same_sub, n, 0.0) for n in ns]
        xs = [eye - nd for nd in nds]
        ps = nds
        for _ in range(int(math.log2(sb)) - 1):
            ps = [_dot1(p, p) for p in ps]
            xs = [x + _dot1(x, p) for x, p in zip(xs, ps)]
        nblk = c // sb
        if nblk > 1:
            mms = [_dot1(x, n - nd) for x, n, nd in zip(xs, ns, nds)]
            ys = [eye - mm for mm in mms]
            ps = mms
            for _ in range(int(math.log2(nblk)) - 1):
                ps = [_dot1(p, p) for p in ps]
                ys = [y + _dot1(y, p) for y, p in zip(ys, ps)]
            xs = [_dot1(y, x) for y, x in zip(ys, xs)]
        sts = [st_ref[h] for h in hs]
        stbs = [st.astype(BF16) for st in sts]
        ksts = [_dg(kbs[h // rep], stbs[h], _NT) for h in hs]
        qsts = [_dg(qbs[h // rep], stbs[h], _NT) for h in hs]
        rhss = [betas[h] * (vs[h] - ebs[h] * ksts[h]) for h in hs]
        ubs = [_dot1(x, rhs).astype(BF16) for x, rhs in zip(xs, rhss)]
        o_intra = [_dg((qks[h // rep] * decs[h]).astype(BF16), ubs[h]) for h in hs]
        upd = [_dg(ubs[h], (ks[h // rep] * jnp.exp(bends[h] - b1s[h])).astype(BF16), _TN) for h in hs]
        for h in hs:
            st_ref[h] = jnp.exp(bends[h]) * sts[h] + upd[h]
            o = ebs[h] * qsts[h] + o_intra[h]
            on = o * lax.rsqrt(jnp.mean(o * o, axis=-1, keepdims=True) + EPS) * nrm
            out = on * _silu(_pad_rows(z_ref[0, rows, vsls[h]], c))
            o_ref[0, rows, vsls[h]] = out[0:tc].astype(o_ref.dtype)

    _run_chunks(nchunks, c, chunk)

    @pl.when(pl.program_id(2) == pl.num_programs(2) - 1)
    def _():
        for h in range(nqk * rep):
            sfin_ref[0, h] = st_ref[h].T


def _delta(zc, zg, conv_w, buf, a_log, dt_bias, nrm, s0, chunk):
    b, t, _ = zc.shape
    tb = min(t, MIXER_ROW_BLOCK)
    assert t % tb == 0 and tb % SUBLANES == 0
    c, tc, nchunks = _chunk_plan(tb, chunk)
    rep = H_C_V // H_C_QK
    nqk = DELTA_QK_HEADS_PER_STEP
    nv = nqk * rep
    ng = H_C_QK // nqk
    qw = nqk * HD_C
    vw = nv * HD_C
    assert W_CQK % qw == 0 and (2 * W_CQK) % vw == 0
    kq0 = W_CQK // qw
    v0 = 2 * W_CQK // vw
    z0 = v0 + W_CV // vw
    buf8 = jnp.pad(buf, ((0, 0), (SUBLANES - (C_CONV - 1), 0), (0, 0)))
    pad_l = jnp.zeros((H_C_V,), F32)
    a_lane = jnp.pad(jnp.concatenate([pad_l, a_log]), (0, LANES - 2 * H_C_V)).reshape(1, LANES)
    d_lane = jnp.pad(jnp.concatenate([pad_l, dt_bias]), (0, LANES - 2 * H_C_V)).reshape(1, LANES)
    hb = tb // SUBLANES

    def rows(col0):
        return lambda bi, g, ti: (bi, ti, col0 + g)

    def first(col0):
        return lambda bi, g, ti: (bi, 0, col0 + g)

    def halo(col0):
        return lambda bi, g, ti: (bi, jnp.maximum(ti * hb - 1, 0), col0 + g)

    full = lambda bi, g, ti: (0, 0)
    sblk = (1, nv, HD_C, HD_C)
    smap = lambda bi, g, ti: (bi, g, 0, 0)
    return pl.pallas_call(
        functools.partial(_delta_kernel, c=c, tc=tc, nchunks=nchunks),
        grid=(b, ng, t // tb),
        in_specs=[pl.BlockSpec((1, tb, qw), rows(0)), pl.BlockSpec((1, tb, qw), rows(kq0)),
                  pl.BlockSpec((1, tb, vw), rows(v0)), pl.BlockSpec((1, tb, vw), rows(z0)),
                  pl.BlockSpec((1, tb, W_CG), lambda bi, g, ti: (bi, ti, 0)),
                  pl.BlockSpec((C_CONV, qw), lambda bi, g, ti: (0, g)),
                  pl.BlockSpec((C_CONV, qw), lambda bi, g, ti: (0, kq0 + g)),
                  pl.BlockSpec((C_CONV, vw), lambda bi, g, ti: (0, v0 + g)),
                  pl.BlockSpec((1, SUBLANES, qw), first(0)), pl.BlockSpec((1, SUBLANES, qw), first(kq0)),
                  pl.BlockSpec((1, SUBLANES, vw), first(v0)),
                  pl.BlockSpec((1, SUBLANES, qw), halo(0)), pl.BlockSpec((1, SUBLANES, qw), halo(kq0)),
                  pl.BlockSpec((1, SUBLANES, vw), halo(v0)),
                  pl.BlockSpec((1, LANES), full), pl.BlockSpec((1, LANES), full),
                  pl.BlockSpec((1, HD_C), full),
                  pl.BlockSpec(sblk, smap)],
        out_specs=[pl.BlockSpec((1, tb, vw), rows(0)), pl.BlockSpec(sblk, smap)],
        out_shape=[jax.ShapeDtypeStruct((b, t, W_CV), BF16), jax.ShapeDtypeStruct((b, H_C_V, HD_C, HD_C), F32)],
        scratch_shapes=[pltpu.VMEM((nv, HD_C, HD_C), F32)],
        compiler_params=_cparams("parallel", "parallel", "arbitrary"),
        name="gated_deltanet",
    )(zc, zc, zc, zc, zg, conv_w, conv_w, conv_w, buf8, buf8, buf8, zc, zc, zc,
      a_lane, d_lane, nrm.reshape(1, HD_C), s0)


def _ret_kernel(q_ref, k_ref, v_ref, g_ref, nrm_ref, s0_ref, o_ref, sfin_ref, st_ref, *, c, tc, nchunks):
    nh = st_ref.shape[0]
    row = lax.broadcasted_iota(jnp.int32, (c, c), 0)
    col = lax.broadcasted_iota(jnp.int32, (c, c), 1)
    steps = (jnp.minimum(row + 1, tc) - jnp.minimum(col + 1, tc)).astype(F32)
    r1 = lax.broadcasted_iota(jnp.int32, (c, 1), 0)
    nsteps = jnp.minimum(r1 + 1, tc).astype(F32)
    valid = r1 < tc
    nrm = nrm_ref[...]
    lgs, decs = [], []
    for h in range(nh):
        hf = jnp.full((1, 1), pl.program_id(1) * nh + h, jnp.int32).astype(F32)
        lg = jnp.log1p(-jnp.exp2(-5.0 - hf))
        lgs.append(lg)
        decs.append(jnp.exp(jnp.where(row >= col, steps * lg, NEG)))
        st_ref[h] = s0_ref[0, h].T

    def chunk(ci, t0):
        rows = pl.ds(t0, tc)
        hs = range(nh)
        sls = [slice(h * HD_D, (h + 1) * HD_D) for h in hs]
        b1s = [nsteps * lgs[h] for h in hs]
        bends = [float(tc) * lgs[h] for h in hs]
        qs = [_pad_rows(q_ref[0, rows, sl], c).astype(BF16) for sl in sls]
        ks = [_pad_rows(k_ref[0, rows, sl], c) * HD_D ** -0.5 for sl in sls]
        if tc < c:
            ks = [jnp.where(valid, k, 0.0) for k in ks]
        vs = [_pad_rows(v_ref[0, rows, sl], c).astype(BF16) for sl in sls]
        sts = [st_ref[h] for h in hs]
        atts = [_dg(qs[h], ks[h].astype(BF16), _NT) * decs[h] for h in hs]
        o_inter = [_dg(qs[h], sts[h].astype(BF16), _NT) for h in hs]
        o_intra = [_dg(atts[h].astype(BF16), vs[h]) for h in hs]
        upd = [_dg(vs[h], (ks[h] * jnp.exp(bends[h] - b1s[h])).astype(BF16), _TN) for h in hs]
        for h in hs:
            sl = sls[h]
            st_ref[h] = jnp.exp(bends[h]) * sts[h] + upd[h]
            o = jnp.exp(b1s[h]) * o_inter[h] + o_intra[h]
            mu = jnp.mean(o, axis=-1, keepdims=True)
            oc = o - mu
            var = jnp.mean(oc * oc, axis=-1, keepdims=True)
            on = oc * lax.rsqrt(var + EPS) * nrm
            out = on * _silu(_pad_rows(g_ref[0, rows, sl], c))
            o_ref[0, rows, sl] = out[0:tc].astype(o_ref.dtype)

    _run_chunks(nchunks, c, chunk)
    for h in range(nh):
        sfin_ref[0, h] = st_ref[h].T


def _retention(zd, nrm, s0, chunk):
    b, t, _ = zd.shape
    c, tc, nchunks = _chunk_plan(t, chunk)
    nh = RET_HEADS_PER_STEP
    ng = H_D // nh
    blk = (1, t, nh * HD_D)
    sblk = (1, nh, HD_D, HD_D)
    return pl.pallas_call(
        functools.partial(_ret_kernel, c=c, tc=tc, nchunks=nchunks),
        grid=(b, ng),
        in_specs=[pl.BlockSpec(blk, lambda bi, g: (bi, 0, g)),
                  pl.BlockSpec(blk, lambda bi, g: (bi, 0, ng + g)),
                  pl.BlockSpec(blk, lambda bi, g: (bi, 0, 2 * ng + g)),
                  pl.BlockSpec(blk, lambda bi, g: (bi, 0, 3 * ng + g)),
                  pl.BlockSpec((1, HD_D), lambda bi, g: (0, 0)),
                  pl.BlockSpec(sblk, lambda bi, g: (bi, g, 0, 0))],
        out_specs=[pl.BlockSpec(blk, lambda bi, g: (bi, 0, g)),
                   pl.BlockSpec(sblk, lambda bi, g: (bi, g, 0, 0))],
        out_shape=[jax.ShapeDtypeStruct((b, t, W_D), BF16), jax.ShapeDtypeStruct((b, H_D, HD_D, HD_D), F32)],
        scratch_shapes=[pltpu.VMEM((nh, HD_D, HD_D), F32)],
        compiler_params=_cparams("parallel", "arbitrary"),
        name="retention",
    )(zd, zd, zd, zd, nrm.reshape(1, HD_D), s0)


def _block(x, pe, cache, s_hgrn, s_delta, buf_delta, s_ret, buf_ffn, p, layer):
    bsz, t, d = x.shape
    m = bsz * t
    x2 = x.reshape(m, d)
    hn = _rmsnorm(x2, p['attn_norm'][layer], BF16)
    za = _matmul(hn, p['w_in_a'], layer).reshape(bsz, t, -1)
    zb = _matmul(hn, p['w_in_b'], layer).reshape(bsz, t, -1)
    zc = _matmul(hn, p['w_in_c'], layer).reshape(bsz, t, -1)
    zg = _matmul(hn, p['w_in_g'], layer).reshape(bsz, t, -1)
    zd = _matmul(hn, p['w_in_d'], layer).reshape(bsz, t, -1)

    if cache is None:
        o_a = _attention_prompt(za)
        kv_new = za
    else:
        o_a = _attention_sample(za, cache, layer)
        kv_new = za[:, :, W_A:].reshape(bsz, t, 2, H_A, HD_A)

    o_b, s_hgrn_new = _hgrn(zb, p['hgrn_lb'], p['hgrn_norm'][layer], s_hgrn, layer, GLA_CHUNK)
    o_c, s_delta_new = _delta(zc, zg, p['delta_conv'][layer], buf_delta, p['delta_A_log'][layer],
                              p['delta_dt_bias'][layer], p['delta_norm'][layer], s_delta, DELTA_CHUNK)
    pre = jnp.concatenate([buf_delta, zc[:, :, :2 * W_CQK + W_CV]], axis=1)
    buf_delta_new = pre[:, pre.shape[1] - (C_CONV - 1):]
    o_d, s_ret_new = _retention(zd, p['ret_norm'][layer], s_ret, RET_CHUNK)

    parts = [o.reshape(m, -1) for o in (o_a, o_b, o_c, o_d)]
    x2 = _matmul_residual(parts, p['w_out'], layer, x2, (1024, 512, 256, 64), (512, 256, 128))

    hf = _rmsnorm(x2, p['ffn_norm'][layer], BF16)
    if cache is None:
        act, buf_ffn_new, p['w_down_bf16'][layer] = _ffn_gate_up_prompt(
            hf, p['w_gate'], p['w_up'], p['ffn_conv'], p['w_down'], layer, t)
    else:
        act, buf_ffn_new = _ffn_gate_up_sample(hf, p['w_gate'], p['w_up'], p['ffn_conv'], layer, buf_ffn, t)
    x2 = _matmul_residual([act], p['w_down_bf16'][layer], 0, x2, (512, 256, 64), (256, 128))

    hp = _rmsnorm(x2, p['ple_norm'][layer], BF16)
    x2 = _ple(hp, p['ple_gate'], pe.reshape(m, -1).astype(BF16), p['ple_proj'], layer, x2)
    states = (kv_new, s_hgrn_new, s_delta_new, buf_delta_new, s_ret_new, buf_ffn_new)
    return x2.reshape(bsz, t, d), states


def kernel(x_prompt, x_sample, cache_attn_kv, state_hgrn, state_delta, state_delta_conv, state_ret,
           state_ffn_conv, p_prompt, p_sample, attn_norm, w_in, hgrn_lb, hgrn_norm, delta_conv,
           delta_A_log, delta_dt_bias, delta_norm, ret_norm, w_out, ffn_norm, w_gate, w_up, ffn_conv,
           w_down, ple_norm, ple_gate, ple_proj, final_norm):
    depth = w_in.shape[0]
    bp = x_prompt.shape[0]
    o_b = 3 * W_A
    o_c = o_b + 4 * W_B
    o_g = o_c + 2 * W_CQK + 2 * W_CV
    o_d = o_g + 2 * H_C_V
    w_in_t = jnp.swapaxes(w_in, 1, 2)
    w_in_d, w_in_g = _cast_tail_t(w_in_t, o_g, o_d - o_g)
    p = {'attn_norm': attn_norm,
         'w_in_a': _cast_bf16_t(w_in_t, 0, o_b), 'w_in_b': _cast_bf16_t(w_in_t, o_b, o_c - o_b),
         'w_in_c': _cast_bf16_t(w_in_t, o_c, o_g - o_c), 'w_in_g': w_in_g, 'w_in_d': w_in_d,
         'hgrn_lb': hgrn_lb, 'hgrn_norm': hgrn_norm, 'delta_conv': delta_conv, 'delta_A_log': delta_A_log,
         'delta_dt_bias': delta_dt_bias, 'delta_norm': delta_norm, 'ret_norm': ret_norm,
         'w_out': w_out, 'ffn_norm': ffn_norm, 'w_gate': w_gate, 'w_up': w_up, 'ffn_conv': ffn_conv,
         'w_down': w_down, 'w_down_bf16': {}, 'ple_norm': ple_norm, 'ple_gate': ple_gate, 'ple_proj': ple_proj}
    xp, xs = x_prompt, x_sample
    st_p, st_s = [], []
    for l in range(depth):
        zero = lambda *s: jnp.zeros((bp,) + s, F32)
        xp, sp = _block(xp, p_prompt[l], None, zero(H_B, HD_B, HD_B), zero(H_C_V, HD_C, HD_C),
                        zero(C_CONV - 1, 2 * W_CQK + W_CV), zero(H_D, HD_D, HD_D), None, p, l)
        xs, ss = _block(xs, p_sample[l], cache_attn_kv, state_hgrn[l], state_delta[l], state_delta_conv[l],
                        state_ret[l], state_ffn_conv[l], p, l)
        st_p.append(sp)
        st_s.append(ss)

    def stack(sts, i):
        return jnp.stack([s[i] for s in sts])

    def final(x):
        return _rmsnorm(x.reshape(-1, x.shape[-1]), final_norm, F32).reshape(x.shape)

    kv_sample = _kv_shift(cache_attn_kv, stack(st_s, 0))
    s_p = x_prompt.shape[1]
    kv_prompt = _kv_pack([s[0] for s in st_p], min(A_BRANCHES[-1][0], s_p))
    return (final(xp), final(xs),
            kv_prompt, kv_sample, stack(st_p, 1), stack(st_s, 1), stack(st_p, 2), stack(st_s, 2),
            stack(st_p, 3), stack(st_s, 3), stack(st_p, 4), stack(st_s, 4), stack(st_p, 5), stack(st_s, 5))
```

```python
import functools
import math

import jax
import jax.numpy as jnp
from jax import lax
from jax.experimental import pallas as pl
from jax.experimental.pallas import tpu as pltpu

F32 = jnp.float32
BF16 = jnp.bfloat16
EPS = 1e-6
NEG = -1e30

H_A = 8
HD_A = 128
A_BRANCHES = ((128, 1), (512, 4), (2048, 16))
A_BLOCK = 128
H_B = 8
HD_B = 128
H_C_QK = 4
H_C_V = 8
HD_C = 128
C_CONV = 4
H_D = 4
HD_D = 256
FFN_CONV = 3

W_A = H_A * HD_A
W_B = H_B * HD_B
W_CQK = H_C_QK * HD_C
W_CV = H_C_V * HD_C
W_D = H_D * HD_D
W_CG = 128

VMEM_LIMIT_BYTES = 52 * 1024 * 1024
SUBLANES = 8
LANES = 128

GLA_HEADS_PER_STEP = 8
DELTA_QK_HEADS_PER_STEP = 4
RET_HEADS_PER_STEP = 2
MIXER_ROW_BLOCK = 1024
GLA_CHUNK = 64
DELTA_CHUNK = 128
RET_CHUNK = 256
DELTA_SUB = 16

_NN = (((1,), (0,)), ((), ()))
_NT = (((1,), (1,)), ((), ()))
_TN = (((0,), (0,)), ((), ()))


def _cparams(*sem):
    return pltpu.CompilerParams(dimension_semantics=sem, vmem_limit_bytes=VMEM_LIMIT_BYTES)


def _dg(a, b, dn=_NN):
    return lax.dot_general(a, b, dn, preferred_element_type=F32)


def _dot1(a, b, dn=_NN):
    return _dg(a.astype(BF16), b.astype(BF16), dn)


def _split3(x):
    x1 = x.astype(BF16)
    r1 = x - x1.astype(F32)
    x2 = r1.astype(BF16)
    x3 = (r1 - x2.astype(F32)).astype(BF16)
    return x1, x2, x3


def _sel_dot(m, x):
    x1, x2, x3 = _split3(x)
    return _dg(m, x1) + (_dg(m, x2) + _dg(m, x3))


def _sigmoid(x):
    return 1.0 / (1.0 + jnp.exp(-x))


def _silu(x):
    return x * _sigmoid(x)


def _softplus(x):
    return jnp.maximum(x, 0.0) + jnp.log1p(jnp.exp(-jnp.abs(x)))


def _pad_rows(x, rows):
    if x.shape[0] == rows:
        return x
    return jnp.concatenate([x, jnp.zeros((rows - x.shape[0],) + x.shape[1:], x.dtype)], axis=0)


def _onehot(cond):
    return jnp.where(cond, 1.0, 0.0).astype(BF16)


def _pick(n, prefs):
    for p in prefs:
        if n % p == 0:
            return p
    return n


def _rmsnorm_kernel(x_ref, g_ref, o_ref):
    x = x_ref[...]
    y = x * lax.rsqrt(jnp.mean(x * x, axis=-1, keepdims=True) + EPS)
    o_ref[...] = (y * g_ref[...]).astype(o_ref.dtype)


def _rmsnorm(x, g, out_dtype):
    m, d = x.shape
    tm = _pick(m, (256, 64, 8))
    return pl.pallas_call(
        _rmsnorm_kernel,
        grid=(m // tm,),
        in_specs=[pl.BlockSpec((tm, d), lambda i: (i, 0)), pl.BlockSpec((1, d), lambda i: (0, 0))],
        out_specs=pl.BlockSpec((tm, d), lambda i: (i, 0)),
        out_shape=jax.ShapeDtypeStruct((m, d), out_dtype),
        compiler_params=_cparams("parallel"),
        name="rmsnorm",
    )(x, g.reshape(1, d))


def _rider_map(nj):
    return lambda i, j: (0, jnp.where(i == 0, j, nj - 1))


def _mm_kernel(a_ref, w_ref, as_ref, o_ref, os_ref):
    w = w_ref[...].astype(BF16)
    o_ref[...] = _dg(a_ref[...], w)

    @pl.when(pl.program_id(0) == 0)
    def _():
        os_ref[...] = _dg(as_ref[...], w)


def _matmul(a, a_s, w, layer):
    m, k = a.shape
    ms = a_s.shape[0]
    n = w.shape[2]
    tm = _pick(m, (1024, 512, 256, 64))
    tn = _pick(n, (512, 256, 128))
    nj = n // tn
    return pl.pallas_call(
        _mm_kernel,
        grid=(m // tm, nj),
        in_specs=[pl.BlockSpec((tm, k), lambda i, j: (i, 0)),
                  pl.BlockSpec((None, k, tn), lambda i, j: (layer, 0, j)),
                  pl.BlockSpec((ms, k), lambda i, j: (0, 0))],
        out_specs=[pl.BlockSpec((tm, tn), lambda i, j: (i, j)), pl.BlockSpec((ms, tn), _rider_map(nj))],
        out_shape=[jax.ShapeDtypeStruct((m, n), F32), jax.ShapeDtypeStruct((ms, n), F32)],
        compiler_params=_cparams("arbitrary", "arbitrary"),
        name="matmul",
    )(a, w, a_s)


def _mm_res_kernel(*refs, nparts):
    a_refs = refs[:nparts]
    w_refs = refs[nparts:2 * nparts]
    x_ref = refs[2 * nparts]
    as_refs = refs[2 * nparts + 1:3 * nparts + 1]
    xs_ref, o_ref, os_ref = refs[3 * nparts + 1:]
    ws = [w_ref[...].astype(BF16) for w_ref in w_refs]
    acc = x_ref[...]
    for a_ref, w in zip(a_refs, ws):
        acc = acc + _dg(a_ref[...], w)
    o_ref[...] = acc

    @pl.when(pl.program_id(0) == 0)
    def _():
        acc_s = xs_ref[...]
        for as_ref, w in zip(as_refs, ws):
            acc_s = acc_s + _dg(as_ref[...], w)
        os_ref[...] = acc_s


def _matmul_residual(parts, parts_s, w, layer, x, x_s, tm_prefs, tn_prefs):
    m, n = x.shape
    ms = x_s.shape[0]
    kp = parts[0].shape[1]
    nparts = len(parts)
    tm = _pick(m, tm_prefs)
    tn = _pick(n, tn_prefs)
    nj = n // tn

    def wmap(p):
        return lambda i, j: (layer, p, j)

    in_specs = [pl.BlockSpec((tm, kp), lambda i, j: (i, 0)) for _ in parts]
    in_specs += [pl.BlockSpec((None, kp, tn), wmap(p)) for p in range(nparts)]
    in_specs += [pl.BlockSpec((tm, tn), lambda i, j: (i, j))]
    in_specs += [pl.BlockSpec((ms, kp), lambda i, j: (0, 0)) for _ in parts]
    in_specs += [pl.BlockSpec((ms, tn), _rider_map(nj))]
    return pl.pallas_call(
        functools.partial(_mm_res_kernel, nparts=nparts),
        grid=(m // tm, nj),
        in_specs=in_specs,
        out_specs=[pl.BlockSpec((tm, tn), lambda i, j: (i, j)), pl.BlockSpec((ms, tn), _rider_map(nj))],
        out_shape=[jax.ShapeDtypeStruct((m, n), F32), jax.ShapeDtypeStruct((ms, n), F32)],
        compiler_params=_cparams("arbitrary", "arbitrary"),
        name="matmul_residual",
    )(*parts, *([w] * nparts), x, *parts_s, x_s)


def _ple_kernel(a_ref, wg_ref, pe_ref, wp_ref, x_ref, as_ref, pes_ref, xs_ref, o_ref, os_ref):
    wg = wg_ref[...].astype(BF16)
    wp = wp_ref[...].astype(BF16)
    o_ref[...] = x_ref[...] + _sigmoid(_dg(a_ref[...], wg)) * _dg(pe_ref[...], wp)

    @pl.when(pl.program_id(0) == 0)
    def _():
        os_ref[...] = xs_ref[...] + _sigmoid(_dg(as_ref[...], wg)) * _dg(pes_ref[...], wp)


def _ple(hp, hp_s, wg, pe, pe_s, wp, layer, x, x_s):
    m, d = x.shape
    ms = x_s.shape[0]
    k = hp.shape[1]
    kp = pe.shape[1]
    tm = _pick(m, (1024, 512, 256, 64))
    tn = _pick(d, (512, 256, 128))
    nj = d // tn
    return pl.pallas_call(
        _ple_kernel,
        grid=(m // tm, nj),
        in_specs=[pl.BlockSpec((tm, k), lambda i, j: (i, 0)),
                  pl.BlockSpec((None, k, tn), lambda i, j: (layer, 0, j)),
                  pl.BlockSpec((tm, kp), lambda i, j: (i, 0)),
                  pl.BlockSpec((None, kp, tn), lambda i, j: (layer, 0, j)),
                  pl.BlockSpec((tm, tn), lambda i, j: (i, j)),
                  pl.BlockSpec((ms, k), lambda i, j: (0, 0)),
                  pl.BlockSpec((ms, kp), lambda i, j: (0, 0)),
                  pl.BlockSpec((ms, tn), _rider_map(nj))],
        out_specs=[pl.BlockSpec((tm, tn), lambda i, j: (i, j)), pl.BlockSpec((ms, tn), _rider_map(nj))],
        out_shape=[jax.ShapeDtypeStruct((m, d), F32), jax.ShapeDtypeStruct((ms, d), F32)],
        compiler_params=_cparams("arbitrary", "arbitrary"),
        name="ple",
    )(hp, wg, pe, wp, x, hp_s, pe_s, x_s)


def _ffn_act(g, p1, p2, cw, u):
    gc = cw[0:1] * p2 + cw[1:2] * p1 + cw[2:3] * g
    return (_silu(gc) * u).astype(BF16)


def _ffn_gu_kernel(h_ref, halo_ref, wg_ref, wu_ref, cw_ref, wd_ref, hs_ref, b1_ref, b2_ref,
                   o_ref, tail_ref, wdb_ref, os_ref, gs_ref, *, seq, t_s):
    wdb_ref[...] = wd_ref[...].astype(wdb_ref.dtype)
    tm = h_ref.shape[0]
    a = h_ref[...]
    wg = wg_ref[...].astype(BF16)
    wu = wu_ref[...].astype(BF16)
    cw = cw_ref[...]
    g = _dg(a, wg)
    u = _dg(a, wu)
    gh = _dg(halo_ref[...], wg)
    seq_start = lax.rem(pl.program_id(0) * tm, seq) == 0
    gh = jnp.where(seq_start, 0.0, gh)
    row = lax.broadcasted_iota(jnp.int32, g.shape, 0)
    p1 = jnp.where(row == 0, gh[7:8], pltpu.roll(g, 1, 0))
    p2 = jnp.where(row == 0, gh[6:7], jnp.where(row == 1, gh[7:8], pltpu.roll(g, 2, 0)))
    o_ref[...] = _ffn_act(g, p1, p2, cw, u)
    tail_ref[0] = g[tm - SUBLANES:tm]

    @pl.when(pl.program_id(0) == 0)
    def _():
        hs = hs_ref[...]
        g_s = _dg(hs, wg)
        u_s = _dg(hs, wu)
        pos = lax.rem(lax.broadcasted_iota(jnp.int32, g_s.shape, 0), t_s)
        q1 = jnp.where(pos == 0, b1_ref[...], pltpu.roll(g_s, 1, 0))
        q2 = jnp.where(pos < 2, b2_ref[...], pltpu.roll(g_s, 2, 0))
        os_ref[...] = _ffn_act(g_s, q1, q2, cw, u_s)
        gs_ref[...] = g_s


def _ffn_gate_up(hf, hf_s, wg, wu, cw, wd, layer, seq, buf_s, t_s):
    m, d = hf.shape
    ms = hf_s.shape[0]
    f = wg.shape[2]
    tm = _pick(seq, (1024, 512, 256, 128, 64, 8))
    tf = _pick(f, (256, 128))
    hb = tm // SUBLANES
    nj = f // tf
    steps = (m // tm) * nj
    assert f % steps == 0 and (f // steps) % (2 * SUBLANES) == 0
    wd_rows = f // steps
    nb = ms // t_s
    zeros = jnp.zeros((nb, t_s - 1, f), F32)
    b1 = jnp.concatenate([buf_s[:, 1:2], zeros], axis=1).reshape(ms, f)
    b2 = jnp.concatenate([buf_s[:, 0:2], zeros[:, 1:]], axis=1).reshape(ms, f)
    wmap = lambda i, j: (layer, 0, j)
    rider = _rider_map(nj)
    out, tail, wd_bf16, out_s, g_s = pl.pallas_call(
        functools.partial(_ffn_gu_kernel, seq=seq, t_s=t_s),
        grid=(m // tm, nj),
        in_specs=[pl.BlockSpec((tm, d), lambda i, j: (i, 0)),
                  pl.BlockSpec((SUBLANES, d), lambda i, j: (jnp.maximum(i * hb - 1, 0), 0)),
                  pl.BlockSpec((None, d, tf), wmap),
                  pl.BlockSpec((None, d, tf), wmap),
                  pl.BlockSpec((None, FFN_CONV, tf), wmap),
                  pl.BlockSpec((None, wd_rows, d), lambda i, j: (layer, i * nj + j, 0)),
                  pl.BlockSpec((ms, d), lambda i, j: (0, 0)),
                  pl.BlockSpec((ms, tf), rider),
                  pl.BlockSpec((ms, tf), rider)],
        out_specs=[pl.BlockSpec((tm, tf), lambda i, j: (i, j)),
                   pl.BlockSpec((1, SUBLANES, tf), lambda i, j: (i, 0, j)),
                   pl.BlockSpec((None, wd_rows, d), lambda i, j: (0, i * nj + j, 0)),
                   pl.BlockSpec((ms, tf), rider),
                   pl.BlockSpec((ms, tf), rider)],
        out_shape=[jax.ShapeDtypeStruct((m, f), BF16),
                   jax.ShapeDtypeStruct((m // tm, SUBLANES, f), F32),
                   jax.ShapeDtypeStruct((1, f, d), BF16),
                   jax.ShapeDtypeStruct((ms, f), BF16),
                   jax.ShapeDtypeStruct((ms, f), F32)],
        compiler_params=_cparams("arbitrary", "arbitrary"),
        name="ffn_gate_up",
    )(hf, hf, wg, wu, cw, wd, hf_s, b1, b2)
    per_seq = seq // tm
    tail = tail.reshape(m // seq, per_seq, SUBLANES, f)[:, per_seq - 1, SUBLANES - (FFN_CONV - 1):]
    tail_s = g_s.reshape(nb, t_s, f)[:, t_s - (FFN_CONV - 1):]
    return out, tail, out_s, tail_s, wd_bf16


ATTN_HEADS_PER_STEP = 4


def _band_softmax_many(probs, dist_cf, dist_pf, valid_c):
    scs = [_dot1(pr[0], pr[1], _NT) for pr in probs]
    sps = [None if pr[2] is None else _dot1(pr[0], pr[2], _NT) for pr in probs]
    pcs, pps, ls, lses = [], [], [], []
    for pr, sc, sp in zip(probs, scs, sps):
        slope, prev_ok = pr[5], pr[6]
        sc = jnp.where(valid_c, sc - slope * dist_cf, NEG)
        if sp is None:
            m = jnp.max(sc, axis=-1, keepdims=True)
            pc = jnp.exp(sc - m)
            pp = None
            l = jnp.sum(pc, axis=-1, keepdims=True)
        else:
            sp = jnp.where(prev_ok, sp - slope * dist_pf, NEG)
            m = jnp.max(jnp.maximum(sc, sp), axis=-1, keepdims=True)
            pc = jnp.exp(sc - m)
            pp = jnp.exp(sp - m)
            l = jnp.sum(pc + pp, axis=-1, keepdims=True)
        pcs.append(pc)
        pps.append(pp)
        ls.append(l)
        lses.append(m + jnp.log(l))
    ocs = [_dot1(pc, pr[3]) for pc, pr in zip(pcs, probs)]
    ops = [None if pp is None else _dot1(pp, pr[4]) for pp, pr in zip(pps, probs)]
    outs = [(oc if op is None else oc + op) / l for oc, op, l in zip(ocs, ops, ls)]
    return outs, lses


def _band_kernel(*refs, s):
    nh = ATTN_HEADS_PER_STEP
    q_refs, k_refs, v_refs = refs[0:nh], refs[nh:2 * nh], refs[2 * nh:3 * nh]
    o_ref, o2_ref, o3_ref, l2_ref, l3_ref = refs[3 * nh:]
    nq = A_BLOCK
    hg = pl.program_id(1)
    qi = lax.broadcasted_iota(jnp.int32, (nq, nq), 0)
    kj = lax.broadcasted_iota(jnp.int32, (nq, nq), 1)
    dist_c = qi - kj
    dist_p = dist_c + nq
    valid_c = dist_c >= 0
    dist_cf = dist_c.astype(F32)
    dist_pf = dist_p.astype(F32)
    (w1, d1), (w2, d2), (w3, d3) = A_BRANCHES
    assert d1 == 1 and s % (d2 * nq) == 0 and s == d3 * nq and max(w1 // d1, w2 // d2, w3 // d3) <= nq

    def problem(h, rows_c, rows_p, has_prev, band, dil):
        hv = jnp.full((1, 1), hg * nh + h + 1, jnp.int32).astype(F32)
        slope = float(dil) * jnp.exp2(-8.0 * hv / H_A)
        q = q_refs[h][0, rows_c, :] * HD_A ** -0.5
        if rows_p is None:
            return (q, k_refs[h][0, rows_c, :], None, v_refs[h][0, rows_c, :], None, slope, None)
        return (q, k_refs[h][0, rows_c, :], k_refs[h][0, rows_p, :], v_refs[h][0, rows_c, :],
                v_refs[h][0, rows_p, :], slope, (dist_p <= band) & has_prev)

    def dilated(i, carry):
        rows3 = pl.ds(i, nq, stride=d3)
        r = lax.rem(i, d2)
        b = i // d2
        rows2 = pl.ds(b * (d2 * nq) + r, nq, stride=d2)
        rows2p = pl.ds(jnp.maximum(b - 1, 0) * (d2 * nq) + r, nq, stride=d2)
        probs = [problem(h, rows3, None, False, w3 // d3, d3) for h in range(nh)]
        probs += [problem(h, rows2, rows2p, b > 0, w2 // d2, d2) for h in range(nh)]
        outs, lses = _band_softmax_many(probs, dist_cf, dist_pf, valid_c)
        for h in range(nh):
            o3_ref[h, rows3, :] = outs[h]
            l3_ref[h, rows3, :] = jnp.broadcast_to(lses[h], (nq, HD_A))
            o2_ref[h, rows2, :] = outs[nh + h]
            l2_ref[h, rows2, :] = jnp.broadcast_to(lses[nh + h], (nq, HD_A))
        return carry

    lax.fori_loop(0, s // nq, dilated, 0)

    def dense(i, carry):
        blocks = []
        for j in range(2):
            bi = 2 * i + j
            rows = pl.ds(pl.multiple_of(bi * nq, nq), nq)
            rows_p = pl.ds(pl.multiple_of(jnp.maximum(bi - 1, 0) * nq, nq), nq)
            blocks += [(h, rows, problem(h, rows, rows_p, bi > 0, w1 // d1, d1)) for h in range(nh)]
        outs, lses = _band_softmax_many([blk[2] for blk in blocks], dist_cf, dist_pf, valid_c)
        for (h, rows, _), o1, l1 in zip(blocks, outs, lses):
            l2 = l2_ref[h, rows, :]
            l3 = l3_ref[h, rows, :]
            mx = jnp.maximum(l1, jnp.maximum(l2, l3))
            e1 = jnp.exp(l1 - mx)
            e2 = jnp.exp(l2 - mx)
            e3 = jnp.exp(l3 - mx)
            o = (e1 * o1 + e2 * o2_ref[h, rows, :] + e3 * o3_ref[h, rows, :]) / (e1 + e2 + e3)
            o_ref[0, rows, h * HD_A:(h + 1) * HD_A] = o.astype(o_ref.dtype)
        return carry

    assert (s // nq) % 2 == 0
    lax.fori_loop(0, s // (2 * nq), dense, 0)


def _attention_prompt(za):
    b, s, _ = za.shape
    nh = ATTN_HEADS_PER_STEP
    ng = H_A // nh
    blk = (1, s, HD_A)

    def head_spec(first, h):
        return pl.BlockSpec(blk, lambda bi, g: (bi, 0, first + g * nh + h))

    in_specs = [head_spec(part * H_A, h) for part in range(3) for h in range(nh)]
    return pl.pallas_call(
        functools.partial(_band_kernel, s=s),
        grid=(b, ng),
        in_specs=in_specs,
        out_specs=pl.BlockSpec((1, s, nh * HD_A), lambda bi, g: (bi, 0, g)),
        out_shape=jax.ShapeDtypeStruct((b, s, W_A), BF16),
        scratch_shapes=[pltpu.VMEM((nh, s, HD_A), F32) for _ in range(4)],
        compiler_params=_cparams("parallel", "arbitrary"),
        name="band_attention",
    )(*([za] * (3 * nh)))


def _attn_sample_kernel(q_ref, kn_ref, vn_ref, c_ref, o_ref):
    t = q_ref.shape[1]
    rows_per_pos = 2 * H_A
    n_past = c_ref.shape[0] // rows_per_pos
    tp = 2 * SUBLANES
    shape_c = (tp, n_past)
    shape_n = (tp, tp)
    dist_c = n_past + lax.broadcasted_iota(jnp.int32, shape_c, 0) - lax.broadcasted_iota(jnp.int32, shape_c, 1)
    dist_n = lax.broadcasted_iota(jnp.int32, shape_n, 0) - lax.broadcasted_iota(jnp.int32, shape_n, 1)
    new_ok = lax.broadcasted_iota(jnp.int32, shape_n, 1) < t
    dist_cf = dist_c.astype(F32)
    dist_nf = dist_n.astype(F32)
    oks = []
    for window, dil in A_BRANCHES:
        oks.append(((dist_c <= window) & (lax.rem(dist_c, dil) == 0),
                    (dist_n >= 0) & (dist_n <= window) & (lax.rem(dist_n, dil) == 0) & new_ok))
    hs = range(H_A)
    slopes = [2.0 ** (-8.0 * (h + 1) / H_A) for h in hs]
    sls = [slice(h * HD_A, (h + 1) * HD_A) for h in hs]
    qs = [_pad_rows(q_ref[0, :, sl] * HD_A ** -0.5, tp).astype(BF16) for sl in sls]
    kns = [_pad_rows(kn_ref[0, :, sl], tp).astype(BF16) for sl in sls]
    vns = [_pad_rows(vn_ref[0, :, sl], tp).astype(BF16) for sl in sls]
    kcs = [c_ref[pl.ds(h, n_past, stride=rows_per_pos), :].astype(BF16) for h in hs]
    vcs = [c_ref[pl.ds(H_A + h, n_past, stride=rows_per_pos), :].astype(BF16) for h in hs]
    b_cs = [_dg(qs[h], kcs[h], _NT) - slopes[h] * dist_cf for h in hs]
    b_ns = [_dg(qs[h], kns[h], _NT) - slopes[h] * dist_nf for h in hs]
    pcs, pns, ls, lses = [], [], [], []
    for h in hs:
        for ok_c, ok_n in oks:
            sc = jnp.where(ok_c, b_cs[h], NEG)
            sn = jnp.where(ok_n, b_ns[h], NEG)
            m = jnp.maximum(jnp.max(sc, axis=-1, keepdims=True), jnp.max(sn, axis=-1, keepdims=True))
            pc = jnp.exp(sc - m)
            pn = jnp.exp(sn - m)
            pcs.append(pc.astype(BF16))
            pns.append(pn.astype(BF16))
            ls.append(jnp.sum(pc, axis=-1, keepdims=True) + jnp.sum(pn, axis=-1, keepdims=True))
            lses.append(m + jnp.log(ls[-1]))
    nbr = len(oks)
    outs = [(_dg(pcs[i], vcs[i // nbr]) + _dg(pns[i], vns[i // nbr])) / ls[i] for i in range(len(pcs))]
    for h in hs:
        lse_h = lses[h * nbr:(h + 1) * nbr]
        out_h = outs[h * nbr:(h + 1) * nbr]
        mx = functools.reduce(jnp.maximum, lse_h)
        es = [jnp.exp(x - mx) for x in lse_h]
        tot = functools.reduce(lambda a, b: a + b, es)
        o = functools.reduce(lambda a, b: a + b, [e * x for e, x in zip(es, out_h)]) / tot
        o_ref[0, :, sls[h]] = o[:t].astype(o_ref.dtype)


def _attention_sample(za, cache, layer):
    b, t, _ = za.shape
    depth, _, l = cache.shape[:3]
    rows = l * 2 * H_A
    cache2d = cache.reshape(depth, b, rows, HD_A)
    return pl.pallas_call(
        _attn_sample_kernel,
        grid=(b,),
        in_specs=[pl.BlockSpec((1, t, W_A), lambda bi: (bi, 0, 0)),
                  pl.BlockSpec((1, t, W_A), lambda bi: (bi, 0, 1)),
                  pl.BlockSpec((1, t, W_A), lambda bi: (bi, 0, 2)),
                  pl.BlockSpec((None, None, rows, HD_A), lambda bi: (layer, bi, 0, 0))],
        out_specs=pl.BlockSpec((1, t, W_A), lambda bi: (bi, 0, 0)),
        out_shape=jax.ShapeDtypeStruct((b, t, W_A), BF16),
        compiler_params=_cparams("parallel"),
        name="attention_sample",
    )(za, za, za, cache2d)


def _cast_kernel(x_ref, o_ref):
    o_ref[...] = x_ref[...].astype(o_ref.dtype)


CAST_BLOCK_BYTES = 6 * 1024 * 1024


def _cast_bf16(w, col0=0, ncols=None):
    depth, k, n = w.shape
    ncols = n - col0 if ncols is None else ncols
    tn = next(c for c in (1024, 512, 256, LANES) if ncols % c == 0 and col0 % c == 0)
    packed_rows = 2 * SUBLANES
    tk = max(d for d in range(packed_rows, k + 1, packed_rows) if k % d == 0 and d * tn * 4 <= CAST_BLOCK_BYTES)
    c0 = col0 // tn
    return pl.pallas_call(
        _cast_kernel,
        grid=(depth, k // tk, ncols // tn),
        in_specs=[pl.BlockSpec((1, tk, tn), lambda l, i, j: (l, i, c0 + j))],
        out_specs=pl.BlockSpec((1, tk, tn), lambda l, i, j: (l, i, j)),
        out_shape=jax.ShapeDtypeStruct((depth, k, ncols), BF16),
        compiler_params=_cparams("parallel", "parallel", "parallel"),
        name="cast_bf16",
    )(w)


def _cast_t_kernel(x_ref, o_ref):
    o_ref[0] = x_ref[0].T.astype(o_ref.dtype)


def _cast_bf16_t(wt, row0, nrows, tn=512):
    depth, _, k = wt.shape
    assert row0 % tn == 0 and nrows % tn == 0
    tk = 1024 if k % 1024 == 0 else k
    r0 = row0 // tn
    return pl.pallas_call(
        _cast_t_kernel,
        grid=(depth, k // tk, nrows // tn),
        in_specs=[pl.BlockSpec((1, tn, tk), lambda l, i, j: (l, r0 + j, i))],
        out_specs=pl.BlockSpec((1, tk, tn), lambda l, i, j: (l, i, j)),
        out_shape=jax.ShapeDtypeStruct((depth, k, nrows), BF16),
        compiler_params=_cparams("parallel", "parallel", "parallel"),
        name="cast_bf16_t",
    )(wt)


def _cast_tail_t_kernel(a_ref, b_ref, d_ref, g_ref, *, shift):
    a = a_ref[0]
    tn = a.shape[0]
    x = jnp.concatenate([a, b_ref[0]], axis=0)
    d_ref[0] = x[shift:shift + tn].T.astype(d_ref.dtype)

    @pl.when(pl.program_id(2) == 0)
    def _():
        lane = lax.broadcasted_iota(jnp.int32, (a.shape[1], LANES), 1)
        g_ref[0] = jnp.where(lane < shift, a[0:LANES].T, 0.0).astype(g_ref.dtype)


def _cast_tail_t(wt, row0, shift, tn=512):
    depth, n, k = wt.shape
    nrows = n - row0 - shift
    packed_rows = 2 * SUBLANES
    assert row0 % tn == 0 and nrows % tn == 0 and shift % packed_rows == 0 and 0 < shift <= LANES
    tk = 1024 if k % 1024 == 0 else k
    r0 = row0 // tn
    per = tn // shift
    return pl.pallas_call(
        functools.partial(_cast_tail_t_kernel, shift=shift),
        grid=(depth, k // tk, nrows // tn),
        in_specs=[pl.BlockSpec((1, tn, tk), lambda l, i, j: (l, r0 + j, i)),
                  pl.BlockSpec((1, shift, tk), lambda l, i, j: (l, (r0 + j + 1) * per, i))],
        out_specs=[pl.BlockSpec((1, tk, tn), lambda l, i, j: (l, i, j)),
                   pl.BlockSpec((1, tk, LANES), lambda l, i, j: (l, i, 0))],
        out_shape=[jax.ShapeDtypeStruct((depth, k, nrows), BF16), jax.ShapeDtypeStruct((depth, k, LANES), BF16)],
        compiler_params=_cparams("parallel", "parallel", "arbitrary"),
        name="cast_tail_t",
    )(wt, wt)


def _kv_pack_kernel(*refs):
    o_ref = refs[-1]
    depth = (len(refs) - 1) // 2
    w = refs[0].shape[2]
    for l in range(depth):
        @pl.when(pl.program_id(0) == l)
        def _(l=l):
            o_ref[0, :, 0:w] = refs[2 * l][0]
            o_ref[0, :, w:2 * w] = refs[2 * l + 1][0]


def _kv_pack(zas, rows):
    depth = len(zas)
    b, s, _ = zas[0].shape
    ts = _pick(rows, (1024, 512, 256, 128, 64, 8))
    first = (s - rows) // ts
    assert (s - rows) % ts == 0
    in_specs, args = [], []
    for l in range(depth):
        for part in (1, 2):
            def imap(d, bi, si, l=l, part=part):
                return (bi, jnp.where(d == l, first + si, first), part)
            in_specs.append(pl.BlockSpec((1, ts, W_A), imap))
            args.append(zas[l])
    out = pl.pallas_call(
        _kv_pack_kernel,
        grid=(depth, b, rows // ts),
        in_specs=in_specs,
        out_specs=pl.BlockSpec((None, 1, ts, 2 * W_A), lambda d, bi, si: (d, bi, si, 0)),
        out_shape=jax.ShapeDtypeStruct((depth, b, rows, 2 * W_A), F32),
        compiler_params=_cparams("parallel", "parallel", "parallel"),
        name="kv_pack",
    )(*args)
    return out.reshape(depth, b, rows, 2, H_A, HD_A)


def _kv_shift_kernel(c_ref, n_ref, o_ref):
    l = c_ref.shape[0]
    t = n_ref.shape[0]
    o_ref[0:l - t] = c_ref[t:l]
    o_ref[l - t:l] = n_ref[...]


def _kv_shift(cache, kv_new):
    depth, b, l, two, h, hd = cache.shape
    t = kv_new.shape[2]
    cblk = (None, None, l, None, h, hd)
    cmap = lambda d, bi, s: (d, bi, 0, s, 0, 0)
    return pl.pallas_call(
        _kv_shift_kernel,
        grid=(depth, b, two),
        in_specs=[pl.BlockSpec(cblk, cmap), pl.BlockSpec((None, None, t, None, h, hd), cmap)],
        out_specs=pl.BlockSpec(cblk, cmap),
        out_shape=jax.ShapeDtypeStruct(cache.shape, cache.dtype),
        compiler_params=_cparams("parallel", "parallel", "parallel"),
        name="kv_shift",
    )(cache, kv_new)


def _chunk_plan(t, chunk):
    c = chunk if t >= chunk else max(2 * SUBLANES, t)
    assert t % c == 0 or t < c
    tc = min(t, c)
    return c, tc, max(t // c, 1)


def _run_chunks(nchunks, c, chunk):
    if nchunks == 1:
        chunk(0, 0)
    else:
        def body(ci, carry):
            chunk(ci, pl.multiple_of(ci * c, c))
            return carry
        lax.fori_loop(0, nchunks, body, 0)


def _gla_kernel(q_ref, f_ref, i_ref, g_ref, lb_ref, nrm_ref, s0_ref, o_ref, sfin_ref, st_ref, *, layer, c, tc, nchunks):
    nlev = int(math.log2(c))
    assert 1 << nlev == c
    depth = lb_ref.shape[0]
    nh = st_ref.shape[0]
    lbs = [lb_ref[i] for i in range(depth)]
    mx = functools.reduce(jnp.maximum, lbs)
    es = [jnp.exp(x - mx) for x in lbs]
    tot = functools.reduce(lambda a, b: a + b, es)
    lower = functools.reduce(lambda a, b: a + b, [es[i] / tot for i in range(layer + 1)]) - es[0] / tot
    one_minus_lb = 1.0 - lower

    row = lax.broadcasted_iota(jnp.int32, (c, c), 0)
    col = lax.broadcasted_iota(jnp.int32, (c, c), 1)
    prefix = [_onehot(col <= row)]
    lmask, second = [], []
    row_hd = lax.broadcasted_iota(jnp.int32, (c, HD_B), 0)
    for lev in range(1, nlev + 1):
        s = c >> lev
        grp = ~(2 * s - 1)
        prefix.append(_onehot(col <= (row & grp) + (s - 1)))
        lmask.append((((row ^ col) & grp) == 0) & ((row & s) != 0) & ((col & s) == 0))
        second.append((row_hd & s) != 0)
    mstack = jnp.concatenate(prefix, axis=0)
    eye = row == col
    valid = lax.broadcasted_iota(jnp.int32, (c, 1), 0) < tc
    nrm = nrm_ref[...]

    @pl.when(pl.program_id(2) == 0)
    def _():
        for h in range(nh):
            st_ref[h] = s0_ref[0, h].T

    def chunk(ci, t0):
        rows = pl.ds(t0, tc)
        hs = range(nh)
        sls = [slice(h * HD_B, (h + 1) * HD_B) for h in hs]
        qs = [_silu(_pad_rows(q_ref[0, rows, sl], c)) for sl in sls]
        kbs = [one_minus_lb[:, sl] * _sigmoid(-_pad_rows(f_ref[0, rows, sl], c)) for sl in sls]
        logfs = [jnp.log1p(-kb) for kb in kbs]
        if tc < c:
            kbs = [jnp.where(valid, kb, 0.0) for kb in kbs]
            logfs = [jnp.where(valid, lf, 0.0) for lf in logfs]
        vs = [_pad_rows(i_ref[0, rows, sl], c).astype(BF16) for sl in sls]
        bs_all = _sel_dot(mstack, jnp.concatenate(logfs, axis=1))
        bss = [bs_all[:, sl] for sl in sls]
        b = [bs[0:c] for bs in bss]
        atts = [jnp.where(eye, jnp.sum(q * kb, axis=-1, keepdims=True), 0.0) for q, kb in zip(qs, kbs)]
        for lev in range(1, nlev + 1):
            brs = [bs[lev * c:(lev + 1) * c] for bs in bss]
            ws = [(jnp.where(second[lev - 1], qs[h], kbs[h]) * jnp.exp(-jnp.abs(b[h] - brs[h]))).astype(BF16)
                  for h in hs]
            prods = [_dg(w, w, _NT) for w in ws]
            atts = [att + jnp.where(lmask[lev - 1], pr, 0.0) for att, pr in zip(atts, prods)]
        sts = [st_ref[h] for h in hs]
        o_inter = [_dot1(qs[h] * jnp.exp(b[h]), sts[h], _NT) for h in hs]
        o_intra = [_dg(atts[h].astype(BF16), vs[h]) for h in hs]
        bends = [b[h][c - 1:c] for h in hs]
        upd = [_dg(vs[h], (kbs[h] * jnp.exp(bends[h] - b[h])).astype(BF16), _TN) for h in hs]
        for h in hs:
            st_ref[h] = sts[h] * jnp.exp(bends[h]) + upd[h]
            o = o_inter[h] + o_intra[h]
            on = o * lax.rsqrt(jnp.mean(o * o, axis=-1, keepdims=True) + EPS) * nrm
            out = on * _silu(_pad_rows(g_ref[0, rows, sls[h]], c))
            o_ref[0, rows, sls[h]] = out[0:tc].astype(o_ref.dtype)

    _run_chunks(nchunks, c, chunk)

    @pl.when(pl.program_id(2) == pl.num_programs(2) - 1)
    def _():
        for h in range(nh):
            sfin_ref[0, h] = st_ref[h].T


def _hgrn(zb, lb, nrm, s0, layer, chunk):
    b, t, _ = zb.shape
    tb = min(t, MIXER_ROW_BLOCK)
    assert t % tb == 0
    c, tc, nchunks = _chunk_plan(tb, chunk)
    depth = lb.shape[0]
    nh = GLA_HEADS_PER_STEP
    ng = H_B // nh
    blk = (1, tb, nh * HD_B)
    sblk = (1, nh, HD_B, HD_B)
    return pl.pallas_call(
        functools.partial(_gla_kernel, layer=layer, c=c, tc=tc, nchunks=nchunks),
        grid=(b, ng, t // tb),
        in_specs=[pl.BlockSpec(blk, lambda bi, g, ti: (bi, ti, g)),
                  pl.BlockSpec(blk, lambda bi, g, ti: (bi, ti, ng + g)),
                  pl.BlockSpec(blk, lambda bi, g, ti: (bi, ti, 2 * ng + g)),
                  pl.BlockSpec(blk, lambda bi, g, ti: (bi, ti, 3 * ng + g)),
                  pl.BlockSpec((depth, 1, nh * HD_B), lambda bi, g, ti: (0, 0, g)),
                  pl.BlockSpec((1, HD_B), lambda bi, g, ti: (0, 0)),
                  pl.BlockSpec(sblk, lambda bi, g, ti: (bi, g, 0, 0))],
        out_specs=[pl.BlockSpec(blk, lambda bi, g, ti: (bi, ti, g)),
                   pl.BlockSpec(sblk, lambda bi, g, ti: (bi, g, 0, 0))],
        out_shape=[jax.ShapeDtypeStruct((b, t, W_B), BF16), jax.ShapeDtypeStruct((b, H_B, HD_B, HD_B), F32)],
        scratch_shapes=[pltpu.VMEM((nh, HD_B, HD_B), F32)],
        compiler_params=_cparams("parallel", "parallel", "arbitrary"),
        name="hgrn2",
    )(zb, zb, zb, zb, lb.reshape(depth, 1, W_B), nrm.reshape(1, HD_B), s0)


def _delta_kernel(q_ref, k_ref, v_ref, z_ref, zg_ref, wq_ref, wk_ref, wv_ref, bq_ref, bk_ref, bv_ref,
                  hq_ref, hk_ref, hv_ref, al_ref, dl_ref, nrm_ref, s0_ref, o_ref, sfin_ref, st_ref,
                  *, c, tc, nchunks):
    g = pl.program_id(1)
    first_block = pl.program_id(2) == 0
    nqk = DELTA_QK_HEADS_PER_STEP
    rep = H_C_V // H_C_QK
    sb = min(DELTA_SUB, c)
    row = lax.broadcasted_iota(jnp.int32, (c, c), 0)
    col = lax.broadcasted_iota(jnp.int32, (c, c), 1)
    lower_incl = _onehot(col <= row)
    eye = jnp.where(row == col, 1.0, 0.0)
    same_sub = ((row ^ col) & ~(sb - 1)) == 0
    valid = lax.broadcasted_iota(jnp.int32, (c, 1), 0) < tc
    lane = lax.broadcasted_iota(jnp.int32, (c, LANES), 1)
    nrm = nrm_ref[...]

    @pl.when(first_block)
    def _():
        for h in range(nqk * rep):
            st_ref[h] = s0_ref[0, h].T

    def conv(x_ref, w_ref, buf_ref, halo_ref, sl, ci, t0):
        x = x_ref[0, pl.ds(t0, tc), sl]
        halo = jnp.where(first_block, buf_ref[0, :, sl], halo_ref[0, :, sl])
        if nchunks > 1:
            prev = x_ref[0, pl.ds(pl.multiple_of(jnp.maximum(t0 - SUBLANES, 0), SUBLANES), SUBLANES), sl]
            halo = jnp.where(ci == 0, halo, prev)
        xh = jnp.concatenate([halo, x], axis=0)
        w = w_ref[:, sl]
        y = w[C_CONV - 1:C_CONV] * x
        for s in range(1, C_CONV):
            y = y + w[C_CONV - 1 - s:C_CONV - s] * pltpu.roll(xh, s, 0)[SUBLANES:]
        return _pad_rows(_silu(y), c)

    def l2n(x):
        return x * lax.rsqrt(jnp.sum(x * x, axis=-1, keepdims=True) + EPS)

    def chunk(ci, t0):
        rows = pl.ds(t0, tc)
        zg = _pad_rows(zg_ref[0, rows, :], c)
        beta_all = _sigmoid(zg)
        la_all = -jnp.exp(al_ref[...]) * _softplus(zg + dl_ref[...])
        nv = nqk * rep
        hs = range(nv)
        qsls = [slice(qh * HD_C, (qh + 1) * HD_C) for qh in range(nqk)]
        vsls = [slice(h * HD_C, (h + 1) * HD_C) for h in hs]
        qs = [l2n(conv(q_ref, wq_ref, bq_ref, hq_ref, sl, ci, t0)) * HD_C ** -0.5 for sl in qsls]
        ks = [l2n(conv(k_ref, wk_ref, bk_ref, hk_ref, sl, ci, t0)) for sl in qsls]
        if tc < c:
            ks = [jnp.where(valid, k, 0.0) for k in ks]
        qbs = [q.astype(BF16) for q in qs]
        kbs = [k.astype(BF16) for k in ks]
        kks = [_dg(kb, kb, _NT) for kb in kbs]
        qks = [_dg(qb, kb, _NT) for qb, kb in zip(qbs, kbs)]
        vs = [conv(v_ref, wv_ref, bv_ref, hv_ref, sl, ci, t0) for sl in vsls]
        betas = [jnp.sum(jnp.where(lane == g * nv + h, beta_all, 0.0), axis=-1, keepdims=True) for h in hs]
        las = [jnp.sum(jnp.where(lane == H_C_V + g * nv + h, la_all, 0.0), axis=-1, keepdims=True) for h in hs]
        if tc < c:
            betas = [jnp.where(valid, x, 0.0) for x in betas]
            las = [jnp.where(valid, x, 0.0) for x in las]
        bc_all = _sel_dot(lower_incl, jnp.concatenate([jnp.broadcast_to(la, (c, LANES)) for la in las], axis=1))
        bcols = [bc_all[:, h * LANES:h * LANES + c] for h in hs]
        decs = [jnp.exp(jnp.where(row >= col, bc - bc.T, NEG)) for bc in bcols]
        b1s = [bc[:, 0:1] for bc in bcols]
        ebs = [jnp.exp(b1) for b1 in b1s]
        bends = [b1[c - 1:c] for b1 in b1s]
        ns = [jnp.where(row > col, betas[h] * (kks[h // rep] * decs[h]), 0.0) for h in hs]
        nds = [jnp.where(same_sub, n, 0.0) for n in ns]
        xs = [eye - nd for nd in nds]
        ps = nds
        for _ in range(int(math.log2(sb)) - 1):
            ps = [_dot1(p, p) for p in ps]
            xs = [x + _dot1(x, p) for x, p in zip(xs, ps)]
        nblk = c // sb
        if nblk > 1:
            mms = [_dot1(x, n - nd) for x, n, nd in zip(xs, ns, nds)]
            ys = [eye - mm for mm in mms]
            ps = mms
            for _ in range(int(math.log2(nblk)) - 1):
                ps = [_dot1(p, p) for p in ps]
                ys = [y + _dot1(y, p) for y, p in zip(ys, ps)]
            xs = [_dot1(y, x) for y, x in zip(ys, xs)]
        sts = [st_ref[h] for h in hs]
        stbs = [st.astype(BF16) for st in sts]
        ksts = [_dg(kbs[h // rep], stbs[h], _NT) for h in hs]
        qsts = [_dg(qbs[h // rep], stbs[h], _NT) for h in hs]
        rhss = [betas[h] * (vs[h] - ebs[h] * ksts[h]) for h in hs]
        ubs = [_dot1(x, rhs).astype(BF16) for x, rhs in zip(xs, rhss)]
        o_intra = [_dg((qks[h // rep] * decs[h]).astype(BF16), ubs[h]) for h in hs]
        upd = [_dg(ubs[h], (ks[h // rep] * jnp.exp(bends[h] - b1s[h])).astype(BF16), _TN) for h in hs]
        for h in hs:
            st_ref[h] = jnp.exp(bends[h]) * sts[h] + upd[h]
            o = ebs[h] * qsts[h] + o_intra[h]
            on = o * lax.rsqrt(jnp.mean(o * o, axis=-1, keepdims=True) + EPS) * nrm
            out = on * _silu(_pad_rows(z_ref[0, rows, vsls[h]], c))
            o_ref[0, rows, vsls[h]] = out[0:tc].astype(o_ref.dtype)

    _run_chunks(nchunks, c, chunk)

    @pl.when(pl.program_id(2) == pl.num_programs(2) - 1)
    def _():
        for h in range(nqk * rep):
            sfin_ref[0, h] = st_ref[h].T


def _delta(zc, zg, conv_w, buf, a_log, dt_bias, nrm, s0, chunk):
    b, t, _ = zc.shape
    tb = min(t, MIXER_ROW_BLOCK)
    assert t % tb == 0 and tb % SUBLANES == 0
    c, tc, nchunks = _chunk_plan(tb, chunk)
    rep = H_C_V // H_C_QK
    nqk = DELTA_QK_HEADS_PER_STEP
    nv = nqk * rep
    ng = H_C_QK // nqk
    qw = nqk * HD_C
    vw = nv * HD_C
    assert W_CQK % qw == 0 and (2 * W_CQK) % vw == 0
    kq0 = W_CQK // qw
    v0 = 2 * W_CQK // vw
    z0 = v0 + W_CV // vw
    buf8 = jnp.pad(buf, ((0, 0), (SUBLANES - (C_CONV - 1), 0), (0, 0)))
    pad_l = jnp.zeros((H_C_V,), F32)
    a_lane = jnp.pad(jnp.concatenate([pad_l, a_log]), (0, LANES - 2 * H_C_V)).reshape(1, LANES)
    d_lane = jnp.pad(jnp.concatenate([pad_l, dt_bias]), (0, LANES - 2 * H_C_V)).reshape(1, LANES)
    hb = tb // SUBLANES

    def rows(col0):
        return lambda bi, g, ti: (bi, ti, col0 + g)

    def first(col0):
        return lambda bi, g, ti: (bi, 0, col0 + g)

    def halo(col0):
        return lambda bi, g, ti: (bi, jnp.maximum(ti * hb - 1, 0), col0 + g)

    full = lambda bi, g, ti: (0, 0)
    sblk = (1, nv, HD_C, HD_C)
    smap = lambda bi, g, ti: (bi, g, 0, 0)
    return pl.pallas_call(
        functools.partial(_delta_kernel, c=c, tc=tc, nchunks=nchunks),
        grid=(b, ng, t // tb),
        in_specs=[pl.BlockSpec((1, tb, qw), rows(0)), pl.BlockSpec((1, tb, qw), rows(kq0)),
                  pl.BlockSpec((1, tb, vw), rows(v0)), pl.BlockSpec((1, tb, vw), rows(z0)),
                  pl.BlockSpec((1, tb, W_CG), lambda bi, g, ti: (bi, ti, 0)),
                  pl.BlockSpec((C_CONV, qw), lambda bi, g, ti: (0, g)),
                  pl.BlockSpec((C_CONV, qw), lambda bi, g, ti: (0, kq0 + g)),
                  pl.BlockSpec((C_CONV, vw), lambda bi, g, ti: (0, v0 + g)),
                  pl.BlockSpec((1, SUBLANES, qw), first(0)), pl.BlockSpec((1, SUBLANES, qw), first(kq0)),
                  pl.BlockSpec((1, SUBLANES, vw), first(v0)),
                  pl.BlockSpec((1, SUBLANES, qw), halo(0)), pl.BlockSpec((1, SUBLANES, qw), halo(kq0)),
                  pl.BlockSpec((1, SUBLANES, vw), halo(v0)),
                  pl.BlockSpec((1, LANES), full), pl.BlockSpec((1, LANES), full),
                  pl.BlockSpec((1, HD_C), full),
                  pl.BlockSpec(sblk, smap)],
        out_specs=[pl.BlockSpec((1, tb, vw), rows(0)), pl.BlockSpec(sblk, smap)],
        out_shape=[jax.ShapeDtypeStruct((b, t, W_CV), BF16), jax.ShapeDtypeStruct((b, H_C_V, HD_C, HD_C), F32)],
        scratch_shapes=[pltpu.VMEM((nv, HD_C, HD_C), F32)],
        compiler_params=_cparams("parallel", "parallel", "arbitrary"),
        name="gated_deltanet",
    )(zc, zc, zc, zc, zg, conv_w, conv_w, conv_w, buf8, buf8, buf8, zc, zc, zc,
      a_lane, d_lane, nrm.reshape(1, HD_C), s0)


def _ret_kernel(q_ref, k_ref, v_ref, g_ref, nrm_ref, s0_ref, o_ref, sfin_ref, st_ref, *, c, tc, nchunks):
    nh = st_ref.shape[0]
    row = lax.broadcasted_iota(jnp.int32, (c, c), 0)
    col = lax.broadcasted_iota(jnp.int32, (c, c), 1)
    steps = (jnp.minimum(row + 1, tc) - jnp.minimum(col + 1, tc)).astype(F32)
    r1 = lax.broadcasted_iota(jnp.int32, (c, 1), 0)
    nsteps = jnp.minimum(r1 + 1, tc).astype(F32)
    valid = r1 < tc
    nrm = nrm_ref[...]
    lgs, decs = [], []
    for h in range(nh):
        hf = jnp.full((1, 1), pl.program_id(1) * nh + h, jnp.int32).astype(F32)
        lg = jnp.log1p(-jnp.exp2(-5.0 - hf))
        lgs.append(lg)
        decs.append(jnp.exp(jnp.where(row >= col, steps * lg, NEG)))
        st_ref[h] = s0_ref[0, h].T

    def chunk(ci, t0):
        rows = pl.ds(t0, tc)
        hs = range(nh)
        sls = [slice(h * HD_D, (h + 1) * HD_D) for h in hs]
        b1s = [nsteps * lgs[h] for h in hs]
        bends = [float(tc) * lgs[h] for h in hs]
        qs = [_pad_rows(q_ref[0, rows, sl], c).astype(BF16) for sl in sls]
        ks = [_pad_rows(k_ref[0, rows, sl], c) * HD_D ** -0.5 for sl in sls]
        if tc < c:
            ks = [jnp.where(valid, k, 0.0) for k in ks]
        vs = [_pad_rows(v_ref[0, rows, sl], c).astype(BF16) for sl in sls]
        sts = [st_ref[h] for h in hs]
        atts = [_dg(qs[h], ks[h].astype(BF16), _NT) * decs[h] for h in hs]
        o_inter = [_dg(qs[h], sts[h].astype(BF16), _NT) for h in hs]
        o_intra = [_dg(atts[h].astype(BF16), vs[h]) for h in hs]
        upd = [_dg(vs[h], (ks[h] * jnp.exp(bends[h] - b1s[h])).astype(BF16), _TN) for h in hs]
        for h in hs:
            sl = sls[h]
            st_ref[h] = jnp.exp(bends[h]) * sts[h] + upd[h]
            o = jnp.exp(b1s[h]) * o_inter[h] + o_intra[h]
            mu = jnp.mean(o, axis=-1, keepdims=True)
            oc = o - mu
            var = jnp.mean(oc * oc, axis=-1, keepdims=True)
            on = oc * lax.rsqrt(var + EPS) * nrm
            out = on * _silu(_pad_rows(g_ref[0, rows, sl], c))
            o_ref[0, rows, sl] = out[0:tc].astype(o_ref.dtype)

    _run_chunks(nchunks, c, chunk)
    for h in range(nh):
        sfin_ref[0, h] = st_ref[h].T


def _retention(zd, nrm, s0, chunk):
    b, t, _ = zd.shape
    c, tc, nchunks = _chunk_plan(t, chunk)
    nh = RET_HEADS_PER_STEP
    ng = H_D // nh
    blk = (1, t, nh * HD_D)
    sblk = (1, nh, HD_D, HD_D)
    return pl.pallas_call(
        functools.partial(_ret_kernel, c=c, tc=tc, nchunks=nchunks),
        grid=(b, ng),
        in_specs=[pl.BlockSpec(blk, lambda bi, g: (bi, 0, g)),
                  pl.BlockSpec(blk, lambda bi, g: (bi, 0, ng + g)),
                  pl.BlockSpec(blk, lambda bi, g: (bi, 0, 2 * ng + g)),
                  pl.BlockSpec(blk, lambda bi, g: (bi, 0, 3 * ng + g)),
                  pl.BlockSpec((1, HD_D), lambda bi, g: (0, 0)),
                  pl.BlockSpec(sblk, lambda bi, g: (bi, g, 0, 0))],
        out_specs=[pl.BlockSpec(blk, lambda bi, g: (bi, 0, g)),
                   pl.BlockSpec(sblk, lambda bi, g: (bi, g, 0, 0))],
        out_shape=[jax.ShapeDtypeStruct((b, t, W_D), BF16), jax.ShapeDtypeStruct((b, H_D, HD_D, HD_D), F32)],
        scratch_shapes=[pltpu.VMEM((nh, HD_D, HD_D), F32)],
        compiler_params=_cparams("parallel", "arbitrary"),
        name="retention",
    )(zd, zd, zd, zd, nrm.reshape(1, HD_D), s0)


def _mixers(z, cache, s_hgrn, s_delta, buf_delta, s_ret, p, layer):
    za, zb, zc, zg, zd = z
    bsz, t, _ = za.shape
    if cache is None:
        o_a = _attention_prompt(za)
        kv_new = za
    else:
        o_a = _attention_sample(za, cache, layer)
        kv_new = za[:, :, W_A:].reshape(bsz, t, 2, H_A, HD_A)
    o_b, s_hgrn_new = _hgrn(zb, p['hgrn_lb'], p['hgrn_norm'][layer], s_hgrn, layer, GLA_CHUNK)
    o_c, s_delta_new = _delta(zc, zg, p['delta_conv'][layer], buf_delta, p['delta_A_log'][layer],
                              p['delta_dt_bias'][layer], p['delta_norm'][layer], s_delta, DELTA_CHUNK)
    pre = jnp.concatenate([buf_delta, zc[:, :, :2 * W_CQK + W_CV]], axis=1)
    buf_delta_new = pre[:, pre.shape[1] - (C_CONV - 1):]
    o_d, s_ret_new = _retention(zd, p['ret_norm'][layer], s_ret, RET_CHUNK)
    parts = [o.reshape(bsz * t, -1) for o in (o_a, o_b, o_c, o_d)]
    return parts, (kv_new, s_hgrn_new, s_delta_new, buf_delta_new, s_ret_new)


def _layer(xs, pes, cache, states, p, layer):
    shapes = [x.shape for x in xs]
    d = shapes[0][2]
    x2 = [x.reshape(-1, d) for x in xs]
    hn = [_rmsnorm(x, p['attn_norm'][layer], BF16) for x in x2]
    z = [_matmul(hn[0], hn[1], p[name], layer) for name in ('w_in_a', 'w_in_b', 'w_in_c', 'w_in_g', 'w_in_d')]
    parts, mixed = [], []
    for g in range(2):
        bsz, t, _ = shapes[g]
        zg = [zz[g].reshape(bsz, t, -1) for zz in z]
        s_hgrn, s_delta, buf_delta, s_ret, _ = states[g]
        pg, mg = _mixers(zg, cache if g == 1 else None, s_hgrn, s_delta, buf_delta, s_ret, p, layer)
        parts.append(pg)
        mixed.append(mg)
    x2 = _matmul_residual(parts[0], parts[1], p['w_out'], layer, x2[0], x2[1],
                          (1024, 512, 256, 64), (512, 256, 128))

    hf = [_rmsnorm(x, p['ffn_norm'][layer], BF16) for x in x2]
    act_p, tail_p, act_s, tail_s, w_down_bf16 = _ffn_gate_up(
        hf[0], hf[1], p['w_gate'], p['w_up'], p['ffn_conv'], p['w_down'], layer,
        shapes[0][1], states[1][4], shapes[1][1])
    x2 = _matmul_residual([act_p], [act_s], w_down_bf16, 0, x2[0], x2[1], (512, 256, 64), (256, 128))

    hp = [_rmsnorm(x, p['ple_norm'][layer], BF16) for x in x2]
    pe2 = [pe.reshape(-1, pe.shape[-1]).astype(BF16) for pe in pes]
    x2 = _ple(hp[0], hp[1], p['ple_gate'], pe2[0], pe2[1], p['ple_proj'], layer, x2[0], x2[1])
    new_states = [mixed[0] + (tail_p,), mixed[1] + (tail_s,)]
    return [x.reshape(s) for x, s in zip(x2, shapes)], new_states


def kernel(x_prompt, x_sample, cache_attn_kv, state_hgrn, state_delta, state_delta_conv, state_ret,
           state_ffn_conv, p_prompt, p_sample, attn_norm, w_in, hgrn_lb, hgrn_norm, delta_conv,
           delta_A_log, delta_dt_bias, delta_norm, ret_norm, w_out, ffn_norm, w_gate, w_up, ffn_conv,
           w_down, ple_norm, ple_gate, ple_proj, final_norm):
    depth = w_in.shape[0]
    bp = x_prompt.shape[0]
    o_b = 3 * W_A
    o_c = o_b + 4 * W_B
    o_g = o_c + 2 * W_CQK + 2 * W_CV
    o_d = o_g + 2 * H_C_V
    w_in_t = jnp.swapaxes(w_in, 1, 2)
    w_in_d, w_in_g = _cast_tail_t(w_in_t, o_g, o_d - o_g)
    p = {'attn_norm': attn_norm,
         'w_in_a': _cast_bf16_t(w_in_t, 0, o_b), 'w_in_b': _cast_bf16_t(w_in_t, o_b, o_c - o_b),
         'w_in_c': _cast_bf16_t(w_in_t, o_c, o_g - o_c), 'w_in_g': w_in_g, 'w_in_d': w_in_d,
         'hgrn_lb': hgrn_lb, 'hgrn_norm': hgrn_norm, 'delta_conv': delta_conv, 'delta_A_log': delta_A_log,
         'delta_dt_bias': delta_dt_bias, 'delta_norm': delta_norm, 'ret_norm': ret_norm,
         'w_out': w_out, 'ffn_norm': ffn_norm, 'w_gate': w_gate, 'w_up': w_up, 'ffn_conv': ffn_conv,
         'w_down': w_down, 'ple_norm': ple_norm, 'ple_gate': ple_gate, 'ple_proj': ple_proj}
    xs = [x_prompt, x_sample]
    st_p, st_s = [], []
    for l in range(depth):
        zero = lambda *s: jnp.zeros((bp,) + s, F32)
        states = [(zero(H_B, HD_B, HD_B), zero(H_C_V, HD_C, HD_C), zero(C_CONV - 1, 2 * W_CQK + W_CV),
                   zero(H_D, HD_D, HD_D), None),
                  (state_hgrn[l], state_delta[l], state_delta_conv[l], state_ret[l], state_ffn_conv[l])]
        xs, (sp, ss) = _layer(xs, [p_prompt[l], p_sample[l]], cache_attn_kv, states, p, l)
        st_p.append(sp)
        st_s.append(ss)
    xp, xs = xs

    def stack(sts, i):
        return jnp.stack([s[i] for s in sts])

    def final(x):
        return _rmsnorm(x.reshape(-1, x.shape[-1]), final_norm, F32).reshape(x.shape)

    kv_sample = _kv_shift(cache_attn_kv, stack(st_s, 0))
    s_p = x_prompt.shape[1]
    kv_prompt = _kv_pack([s[0] for s in st_p], min(A_BRANCHES[-1][0], s_p))
    return (final(xp), final(xs),
            kv_prompt, kv_sample, stack(st_p, 1), stack(st_s, 1), stack(st_p, 2), stack(st_s, 2),
            stack(st_p, 3), stack(st_s, 3), stack(st_p, 4), stack(st_s, 4), stack(st_p, 5), stack(st_s, 5))
```

```python
import functools
import math

import jax
import jax.numpy as jnp
from jax import lax
from jax.experimental import pallas as pl
from jax.experimental.pallas import tpu as pltpu

F32 = jnp.float32
BF16 = jnp.bfloat16
EPS = 1e-6
NEG = -1e30

H_A = 8
HD_A = 128
A_BRANCHES = ((128, 1), (512, 4), (2048, 16))
A_BLOCK = 128
H_B = 8
HD_B = 128
H_C_QK = 4
H_C_V = 8
HD_C = 128
C_CONV = 4
H_D = 4
HD_D = 256
FFN_CONV = 3

W_A = H_A * HD_A
W_B = H_B * HD_B
W_CQK = H_C_QK * HD_C
W_CV = H_C_V * HD_C
W_D = H_D * HD_D
W_CG = 128

VMEM_LIMIT_BYTES = 52 * 1024 * 1024
SUBLANES = 8
LANES = 128

GLA_HEADS_PER_STEP = 8
DELTA_QK_HEADS_PER_STEP = 4
RET_HEADS_PER_STEP = 2
MIXER_ROW_BLOCK = 1024
GLA_CHUNK = 64
DELTA_CHUNK = 128
RET_CHUNK = 256
DELTA_SUB = 16

_NN = (((1,), (0,)), ((), ()))
_NT = (((1,), (1,)), ((), ()))
_TN = (((0,), (0,)), ((), ()))


def _cparams(*sem):
    return pltpu.CompilerParams(dimension_semantics=sem, vmem_limit_bytes=VMEM_LIMIT_BYTES)


def _dg(a, b, dn=_NN):
    return lax.dot_general(a, b, dn, preferred_element_type=F32)


def _dot1(a, b, dn=_NN):
    return _dg(a.astype(BF16), b.astype(BF16), dn)


def _split3(x):
    x1 = x.astype(BF16)
    r1 = x - x1.astype(F32)
    x2 = r1.astype(BF16)
    x3 = (r1 - x2.astype(F32)).astype(BF16)
    return x1, x2, x3


def _sel_dot(m, x):
    x1, x2, x3 = _split3(x)
    return _dg(m, x1) + (_dg(m, x2) + _dg(m, x3))


def _sigmoid(x):
    return 1.0 / (1.0 + jnp.exp(-x))


def _silu(x):
    return x * _sigmoid(x)


def _softplus(x):
    return jnp.maximum(x, 0.0) + jnp.log1p(jnp.exp(-jnp.abs(x)))


def _pad_rows(x, rows):
    if x.shape[0] == rows:
        return x
    return jnp.concatenate([x, jnp.zeros((rows - x.shape[0],) + x.shape[1:], x.dtype)], axis=0)


def _onehot(cond):
    return jnp.where(cond, 1.0, 0.0).astype(BF16)


def _pick(n, prefs):
    for p in prefs:
        if n % p == 0:
            return p
    return n


def _rmsnorm_kernel(x_ref, g_ref, o_ref):
    x = x_ref[...]
    y = x * lax.rsqrt(jnp.mean(x * x, axis=-1, keepdims=True) + EPS)
    o_ref[...] = (y * g_ref[...]).astype(o_ref.dtype)


def _rmsnorm(x, g, out_dtype):
    m, d = x.shape
    tm = _pick(m, (256, 64, 8))
    return pl.pallas_call(
        _rmsnorm_kernel,
        grid=(m // tm,),
        in_specs=[pl.BlockSpec((tm, d), lambda i: (i, 0)), pl.BlockSpec((1, d), lambda i: (0, 0))],
        out_specs=pl.BlockSpec((tm, d), lambda i: (i, 0)),
        out_shape=jax.ShapeDtypeStruct((m, d), out_dtype),
        compiler_params=_cparams("parallel"),
        name="rmsnorm",
    )(x, g.reshape(1, d))


def _rider_map(nj):
    return lambda i, j: (0, jnp.where(i == 0, j, nj - 1))


def _both_groups(a_ref, as_ref):
    return jnp.concatenate([a_ref[...], as_ref[...]], axis=0)


def _mm_kernel(a_ref, w_ref, as_ref, o_ref, os_ref):
    tm = a_ref.shape[0]
    first = pl.program_id(0) == 0

    @pl.when(first)
    def _():
        r = _dg(_both_groups(a_ref, as_ref), w_ref[...].astype(BF16))
        o_ref[...] = r[:tm]
        os_ref[...] = r[tm:]

    @pl.when(jnp.logical_not(first))
    def _():
        o_ref[...] = _dg(a_ref[...], w_ref[...].astype(BF16))


def _matmul(a, a_s, w, layer):
    m, k = a.shape
    ms = a_s.shape[0]
    n = w.shape[2]
    tm = _pick(m, (1024, 512, 256, 64))
    tn = _pick(n, (512, 256, 128))
    nj = n // tn
    return pl.pallas_call(
        _mm_kernel,
        grid=(m // tm, nj),
        in_specs=[pl.BlockSpec((tm, k), lambda i, j: (i, 0)),
                  pl.BlockSpec((None, k, tn), lambda i, j: (layer, 0, j)),
                  pl.BlockSpec((ms, k), lambda i, j: (0, 0))],
        out_specs=[pl.BlockSpec((tm, tn), lambda i, j: (i, j)), pl.BlockSpec((ms, tn), _rider_map(nj))],
        out_shape=[jax.ShapeDtypeStruct((m, n), F32), jax.ShapeDtypeStruct((ms, n), F32)],
        compiler_params=_cparams("arbitrary", "arbitrary"),
        name="matmul",
    )(a, w, a_s)


def _mm_res_kernel(*refs, nparts):
    a_refs = refs[:nparts]
    w_refs = refs[nparts:2 * nparts]
    x_ref = refs[2 * nparts]
    as_refs = refs[2 * nparts + 1:3 * nparts + 1]
    xs_ref, o_ref, os_ref = refs[3 * nparts + 1:]
    tm = x_ref.shape[0]
    first = pl.program_id(0) == 0

    @pl.when(first)
    def _():
        acc = jnp.concatenate([x_ref[...], xs_ref[...]], axis=0)
        for a_ref, as_ref, w_ref in zip(a_refs, as_refs, w_refs):
            acc = acc + _dg(_both_groups(a_ref, as_ref), w_ref[...].astype(BF16))
        o_ref[...] = acc[:tm]
        os_ref[...] = acc[tm:]

    @pl.when(jnp.logical_not(first))
    def _():
        acc = x_ref[...]
        for a_ref, w_ref in zip(a_refs, w_refs):
            acc = acc + _dg(a_ref[...], w_ref[...].astype(BF16))
        o_ref[...] = acc


def _matmul_residual(parts, parts_s, w, layer, x, x_s, tm_prefs, tn_prefs):
    m, n = x.shape
    ms = x_s.shape[0]
    kp = parts[0].shape[1]
    nparts = len(parts)
    tm = _pick(m, tm_prefs)
    tn = _pick(n, tn_prefs)
    nj = n // tn

    def wmap(p):
        return lambda i, j: (layer, p, j)

    in_specs = [pl.BlockSpec((tm, kp), lambda i, j: (i, 0)) for _ in parts]
    in_specs += [pl.BlockSpec((None, kp, tn), wmap(p)) for p in range(nparts)]
    in_specs += [pl.BlockSpec((tm, tn), lambda i, j: (i, j))]
    in_specs += [pl.BlockSpec((ms, kp), lambda i, j: (0, 0)) for _ in parts]
    in_specs += [pl.BlockSpec((ms, tn), _rider_map(nj))]
    return pl.pallas_call(
        functools.partial(_mm_res_kernel, nparts=nparts),
        grid=(m // tm, nj),
        in_specs=in_specs,
        out_specs=[pl.BlockSpec((tm, tn), lambda i, j: (i, j)), pl.BlockSpec((ms, tn), _rider_map(nj))],
        out_shape=[jax.ShapeDtypeStruct((m, n), F32), jax.ShapeDtypeStruct((ms, n), F32)],
        compiler_params=_cparams("arbitrary", "arbitrary"),
        name="matmul_residual",
    )(*parts, *([w] * nparts), x, *parts_s, x_s)


def _ple_kernel(a_ref, wg_ref, pe_ref, wp_ref, x_ref, as_ref, pes_ref, xs_ref, o_ref, os_ref):
    tm = x_ref.shape[0]
    first = pl.program_id(0) == 0

    @pl.when(first)
    def _():
        x = jnp.concatenate([x_ref[...], xs_ref[...]], axis=0)
        gate = _sigmoid(_dg(_both_groups(a_ref, as_ref), wg_ref[...].astype(BF16)))
        r = x + gate * _dg(_both_groups(pe_ref, pes_ref), wp_ref[...].astype(BF16))
        o_ref[...] = r[:tm]
        os_ref[...] = r[tm:]

    @pl.when(jnp.logical_not(first))
    def _():
        gate = _sigmoid(_dg(a_ref[...], wg_ref[...].astype(BF16)))
        o_ref[...] = x_ref[...] + gate * _dg(pe_ref[...], wp_ref[...].astype(BF16))


def _ple(hp, hp_s, wg, pe, pe_s, wp, layer, x, x_s):
    m, d = x.shape
    ms = x_s.shape[0]
    k = hp.shape[1]
    kp = pe.shape[1]
    tm = _pick(m, (1024, 512, 256, 64))
    tn = _pick(d, (512, 256, 128))
    nj = d // tn
    return pl.pallas_call(
        _ple_kernel,
        grid=(m // tm, nj),
        in_specs=[pl.BlockSpec((tm, k), lambda i, j: (i, 0)),
                  pl.BlockSpec((None, k, tn), lambda i, j: (layer, 0, j)),
                  pl.BlockSpec((tm, kp), lambda i, j: (i, 0)),
                  pl.BlockSpec((None, kp, tn), lambda i, j: (layer, 0, j)),
                  pl.BlockSpec((tm, tn), lambda i, j: (i, j)),
                  pl.BlockSpec((ms, k), lambda i, j: (0, 0)),
                  pl.BlockSpec((ms, kp), lambda i, j: (0, 0)),
                  pl.BlockSpec((ms, tn), _rider_map(nj))],
        out_specs=[pl.BlockSpec((tm, tn), lambda i, j: (i, j)), pl.BlockSpec((ms, tn), _rider_map(nj))],
        out_shape=[jax.ShapeDtypeStruct((m, d), F32), jax.ShapeDtypeStruct((ms, d), F32)],
        compiler_params=_cparams("arbitrary", "arbitrary"),
        name="ple",
    )(hp, wg, pe, wp, x, hp_s, pe_s, x_s)


def _ffn_act(g, p1, p2, cw, u):
    gc = cw[0:1] * p2 + cw[1:2] * p1 + cw[2:3] * g
    return (_silu(gc) * u).astype(BF16)


def _ffn_gu_kernel(h_ref, halo_ref, wg_ref, wu_ref, cw_ref, wd_ref, hs_ref, b1_ref, b2_ref,
                   o_ref, tail_ref, wdb_ref, os_ref, gs_ref, *, seq, t_s):
    tm = h_ref.shape[0]
    first = pl.program_id(0) == 0

    def prompt_rows(g, u, gh):
        seq_start = lax.rem(pl.program_id(0) * tm, seq) == 0
        gh = jnp.where(seq_start, 0.0, gh)
        row = lax.broadcasted_iota(jnp.int32, g.shape, 0)
        p1 = jnp.where(row == 0, gh[7:8], pltpu.roll(g, 1, 0))
        p2 = jnp.where(row == 0, gh[6:7], jnp.where(row == 1, gh[7:8], pltpu.roll(g, 2, 0)))
        o_ref[...] = _ffn_act(g, p1, p2, cw_ref[...], u)
        tail_ref[0] = g[tm - SUBLANES:tm]

    @pl.when(first)
    def _():
        wdb_ref[...] = wd_ref[...].astype(wdb_ref.dtype)
        wg = wg_ref[...].astype(BF16)
        wu = wu_ref[...].astype(BF16)
        cw = cw_ref[...]
        a = _both_groups(h_ref, hs_ref)
        g = _dg(a, wg)
        u = _dg(a, wu)
        prompt_rows(g[:tm], u[:tm], jnp.zeros((SUBLANES, g.shape[1]), F32))
        g_s = g[tm:]
        pos = lax.rem(lax.broadcasted_iota(jnp.int32, g_s.shape, 0), t_s)
        q1 = jnp.where(pos == 0, b1_ref[...], pltpu.roll(g_s, 1, 0))
        q2 = jnp.where(pos < 2, b2_ref[...], pltpu.roll(g_s, 2, 0))
        os_ref[...] = _ffn_act(g_s, q1, q2, cw, u[tm:])
        gs_ref[...] = g_s

    @pl.when(jnp.logical_not(first))
    def _():
        wdb_ref[...] = wd_ref[...].astype(wdb_ref.dtype)
        wg = wg_ref[...].astype(BF16)
        a = h_ref[...]
        prompt_rows(_dg(a, wg), _dg(a, wu_ref[...].astype(BF16)), _dg(halo_ref[...], wg))


def _ffn_gate_up(hf, hf_s, wg, wu, cw, wd, layer, seq, buf_s, t_s):
    m, d = hf.shape
    ms = hf_s.shape[0]
    f = wg.shape[2]
    tm = _pick(seq, (1024, 512, 256, 128, 64, 8))
    tf = _pick(f, (256, 128))
    hb = tm // SUBLANES
    nj = f // tf
    steps = (m // tm) * nj
    assert f % steps == 0 and (f // steps) % (2 * SUBLANES) == 0
    wd_rows = f // steps
    nb = ms // t_s
    zeros = jnp.zeros((nb, t_s - 1, f), F32)
    b1 = jnp.concatenate([buf_s[:, 1:2], zeros], axis=1).reshape(ms, f)
    b2 = jnp.concatenate([buf_s[:, 0:2], zeros[:, 1:]], axis=1).reshape(ms, f)
    wmap = lambda i, j: (layer, 0, j)
    rider = _rider_map(nj)
    out, tail, wd_bf16, out_s, g_s = pl.pallas_call(
        functools.partial(_ffn_gu_kernel, seq=seq, t_s=t_s),
        grid=(m // tm, nj),
        in_specs=[pl.BlockSpec((tm, d), lambda i, j: (i, 0)),
                  pl.BlockSpec((SUBLANES, d), lambda i, j: (jnp.maximum(i * hb - 1, 0), 0)),
                  pl.BlockSpec((None, d, tf), wmap),
                  pl.BlockSpec((None, d, tf), wmap),
                  pl.BlockSpec((None, FFN_CONV, tf), wmap),
                  pl.BlockSpec((None, wd_rows, d), lambda i, j: (layer, i * nj + j, 0)),
                  pl.BlockSpec((ms, d), lambda i, j: (0, 0)),
                  pl.BlockSpec((ms, tf), rider),
                  pl.BlockSpec((ms, tf), rider)],
        out_specs=[pl.BlockSpec((tm, tf), lambda i, j: (i, j)),
                   pl.BlockSpec((1, SUBLANES, tf), lambda i, j: (i, 0, j)),
                   pl.BlockSpec((None, wd_rows, d), lambda i, j: (0, i * nj + j, 0)),
                   pl.BlockSpec((ms, tf), rider),
                   pl.BlockSpec((ms, tf), rider)],
        out_shape=[jax.ShapeDtypeStruct((m, f), BF16),
                   jax.ShapeDtypeStruct((m // tm, SUBLANES, f), F32),
                   jax.ShapeDtypeStruct((1, f, d), BF16),
                   jax.ShapeDtypeStruct((ms, f), BF16),
                   jax.ShapeDtypeStruct((ms, f), F32)],
        compiler_params=_cparams("arbitrary", "arbitrary"),
        name="ffn_gate_up",
    )(hf, hf, wg, wu, cw, wd, hf_s, b1, b2)
    per_seq = seq // tm
    tail = tail.reshape(m // seq, per_seq, SUBLANES, f)[:, per_seq - 1, SUBLANES - (FFN_CONV - 1):]
    tail_s = g_s.reshape(nb, t_s, f)[:, t_s - (FFN_CONV - 1):]
    return out, tail, out_s, tail_s, wd_bf16


ATTN_HEADS_PER_STEP = 4


def _band_softmax_many(probs, dist_cf, dist_pf, valid_c):
    scs = [_dot1(pr[0], pr[1], _NT) for pr in probs]
    sps = [None if pr[2] is None else _dot1(pr[0], pr[2], _NT) for pr in probs]
    pcs, pps, ls, lses = [], [], [], []
    for pr, sc, sp in zip(probs, scs, sps):
        slope, prev_ok = pr[5], pr[6]
        sc = jnp.where(valid_c, sc - slope * dist_cf, NEG)
        if sp is None:
            m = jnp.max(sc, axis=-1, keepdims=True)
            pc = jnp.exp(sc - m)
            pp = None
            l = jnp.sum(pc, axis=-1, keepdims=True)
        else:
            sp = jnp.where(prev_ok, sp - slope * dist_pf, NEG)
            m = jnp.max(jnp.maximum(sc, sp), axis=-1, keepdims=True)
            pc = jnp.exp(sc - m)
            pp = jnp.exp(sp - m)
            l = jnp.sum(pc + pp, axis=-1, keepdims=True)
        pcs.append(pc)
        pps.append(pp)
        ls.append(l)
        lses.append(m + jnp.log(l))
    ocs = [_dot1(pc, pr[3]) for pc, pr in zip(pcs, probs)]
    ops = [None if pp is None else _dot1(pp, pr[4]) for pp, pr in zip(pps, probs)]
    outs = [(oc if op is None else oc + op) / l for oc, op, l in zip(ocs, ops, ls)]
    return outs, lses


def _band_kernel(*refs, s):
    nh = ATTN_HEADS_PER_STEP
    q_refs, k_refs, v_refs = refs[0:nh], refs[nh:2 * nh], refs[2 * nh:3 * nh]
    o_ref, o2_ref, o3_ref, l2_ref, l3_ref = refs[3 * nh:]
    nq = A_BLOCK
    hg = pl.program_id(1)
    qi = lax.broadcasted_iota(jnp.int32, (nq, nq), 0)
    kj = lax.broadcasted_iota(jnp.int32, (nq, nq), 1)
    dist_c = qi - kj
    dist_p = dist_c + nq
    valid_c = dist_c >= 0
    dist_cf = dist_c.astype(F32)
    dist_pf = dist_p.astype(F32)
    (w1, d1), (w2, d2), (w3, d3) = A_BRANCHES
    assert d1 == 1 and s % (d2 * nq) == 0 and s == d3 * nq and max(w1 // d1, w2 // d2, w3 // d3) <= nq

    def problem(h, rows_c, rows_p, has_prev, band, dil):
        hv = jnp.full((1, 1), hg * nh + h + 1, jnp.int32).astype(F32)
        slope = float(dil) * jnp.exp2(-8.0 * hv / H_A)
        q = q_refs[h][0, rows_c, :] * HD_A ** -0.5
        if rows_p is None:
            return (q, k_refs[h][0, rows_c, :], None, v_refs[h][0, rows_c, :], None, slope, None)
        return (q, k_refs[h][0, rows_c, :], k_refs[h][0, rows_p, :], v_refs[h][0, rows_c, :],
                v_refs[h][0, rows_p, :], slope, (dist_p <= band) & has_prev)

    def dilated(i, carry):
        rows3 = pl.ds(i, nq, stride=d3)
        r = lax.rem(i, d2)
        b = i // d2
        rows2 = pl.ds(b * (d2 * nq) + r, nq, stride=d2)
        rows2p = pl.ds(jnp.maximum(b - 1, 0) * (d2 * nq) + r, nq, stride=d2)
        probs = [problem(h, rows3, None, False, w3 // d3, d3) for h in range(nh)]
        probs += [problem(h, rows2, rows2p, b > 0, w2 // d2, d2) for h in range(nh)]
        outs, lses = _band_softmax_many(probs, dist_cf, dist_pf, valid_c)
        for h in range(nh):
            o3_ref[h, rows3, :] = outs[h]
            l3_ref[h, rows3, :] = jnp.broadcast_to(lses[h], (nq, HD_A))
            o2_ref[h, rows2, :] = outs[nh + h]
            l2_ref[h, rows2, :] = jnp.broadcast_to(lses[nh + h], (nq, HD_A))
        return carry

    lax.fori_loop(0, s // nq, dilated, 0)

    def dense(i, carry):
        blocks = []
        for j in range(2):
            bi = 2 * i + j
            rows = pl.ds(pl.multiple_of(bi * nq, nq), nq)
            rows_p = pl.ds(pl.multiple_of(jnp.maximum(bi - 1, 0) * nq, nq), nq)
            blocks += [(h, rows, problem(h, rows, rows_p, bi > 0, w1 // d1, d1)) for h in range(nh)]
        outs, lses = _band_softmax_many([blk[2] for blk in blocks], dist_cf, dist_pf, valid_c)
        for (h, rows, _), o1, l1 in zip(blocks, outs, lses):
            l2 = l2_ref[h, rows, :]
            l3 = l3_ref[h, rows, :]
            mx = jnp.maximum(l1, jnp.maximum(l2, l3))
            e1 = jnp.exp(l1 - mx)
            e2 = jnp.exp(l2 - mx)
            e3 = jnp.exp(l3 - mx)
            o = (e1 * o1 + e2 * o2_ref[h, rows, :] + e3 * o3_ref[h, rows, :]) / (e1 + e2 + e3)
            o_ref[0, rows, h * HD_A:(h + 1) * HD_A] = o.astype(o_ref.dtype)
        return carry

    assert (s // nq) % 2 == 0
    lax.fori_loop(0, s // (2 * nq), dense, 0)


def _attention_prompt(za):
    b, s, _ = za.shape
    nh = ATTN_HEADS_PER_STEP
    ng = H_A // nh
    blk = (1, s, HD_A)

    def head_spec(first, h):
        return pl.BlockSpec(blk, lambda bi, g: (bi, 0, first + g * nh + h))

    in_specs = [head_spec(part * H_A, h) for part in range(3) for h in range(nh)]
    return pl.pallas_call(
        functools.partial(_band_kernel, s=s),
        grid=(b, ng),
        in_specs=in_specs,
        out_specs=pl.BlockSpec((1, s, nh * HD_A), lambda bi, g: (bi, 0, g)),
        out_shape=jax.ShapeDtypeStruct((b, s, W_A), BF16),
        scratch_shapes=[pltpu.VMEM((nh, s, HD_A), F32) for _ in range(4)],
        compiler_params=_cparams("parallel", "arbitrary"),
        name="band_attention",
    )(*([za] * (3 * nh)))


def _attn_sample_kernel(q_ref, kn_ref, vn_ref, c_ref, o_ref):
    t = q_ref.shape[1]
    rows_per_pos = 2 * H_A
    n_past = c_ref.shape[0] // rows_per_pos
    tp = 2 * SUBLANES
    shape_c = (tp, n_past)
    shape_n = (tp, tp)
    dist_c = n_past + lax.broadcasted_iota(jnp.int32, shape_c, 0) - lax.broadcasted_iota(jnp.int32, shape_c, 1)
    dist_n = lax.broadcasted_iota(jnp.int32, shape_n, 0) - lax.broadcasted_iota(jnp.int32, shape_n, 1)
    new_ok = lax.broadcasted_iota(jnp.int32, shape_n, 1) < t
    dist_cf = dist_c.astype(F32)
    dist_nf = dist_n.astype(F32)
    oks = []
    for window, dil in A_BRANCHES:
        oks.append(((dist_c <= window) & (lax.rem(dist_c, dil) == 0),
                    (dist_n >= 0) & (dist_n <= window) & (lax.rem(dist_n, dil) == 0) & new_ok))
    hs = range(H_A)
    slopes = [2.0 ** (-8.0 * (h + 1) / H_A) for h in hs]
    sls = [slice(h * HD_A, (h + 1) * HD_A) for h in hs]
    qs = [_pad_rows(q_ref[0, :, sl] * HD_A ** -0.5, tp).astype(BF16) for sl in sls]
    kns = [_pad_rows(kn_ref[0, :, sl], tp).astype(BF16) for sl in sls]
    vns = [_pad_rows(vn_ref[0, :, sl], tp).astype(BF16) for sl in sls]
    kcs = [c_ref[pl.ds(h, n_past, stride=rows_per_pos), :].astype(BF16) for h in hs]
    vcs = [c_ref[pl.ds(H_A + h, n_past, stride=rows_per_pos), :].astype(BF16) for h in hs]
    b_cs = [_dg(qs[h], kcs[h], _NT) - slopes[h] * dist_cf for h in hs]
    b_ns = [_dg(qs[h], kns[h], _NT) - slopes[h] * dist_nf for h in hs]
    pcs, pns, ls, lses = [], [], [], []
    for h in hs:
        for ok_c, ok_n in oks:
            sc = jnp.where(ok_c, b_cs[h], NEG)
            sn = jnp.where(ok_n, b_ns[h], NEG)
            m = jnp.maximum(jnp.max(sc, axis=-1, keepdims=True), jnp.max(sn, axis=-1, keepdims=True))
            pc = jnp.exp(sc - m)
            pn = jnp.exp(sn - m)
            pcs.append(pc.astype(BF16))
            pns.append(pn.astype(BF16))
            ls.append(jnp.sum(pc, axis=-1, keepdims=True) + jnp.sum(pn, axis=-1, keepdims=True))
            lses.append(m + jnp.log(ls[-1]))
    nbr = len(oks)
    outs = [(_dg(pcs[i], vcs[i // nbr]) + _dg(pns[i], vns[i // nbr])) / ls[i] for i in range(len(pcs))]
    for h in hs:
        lse_h = lses[h * nbr:(h + 1) * nbr]
        out_h = outs[h * nbr:(h + 1) * nbr]
        mx = functools.reduce(jnp.maximum, lse_h)
        es = [jnp.exp(x - mx) for x in lse_h]
        tot = functools.reduce(lambda a, b: a + b, es)
        o = functools.reduce(lambda a, b: a + b, [e * x for e, x in zip(es, out_h)]) / tot
        o_ref[0, :, sls[h]] = o[:t].astype(o_ref.dtype)


def _attention_sample(za, cache, layer):
    b, t, _ = za.shape
    depth, _, l = cache.shape[:3]
    rows = l * 2 * H_A
    cache2d = cache.reshape(depth, b, rows, HD_A)
    return pl.pallas_call(
        _attn_sample_kernel,
        grid=(b,),
        in_specs=[pl.BlockSpec((1, t, W_A), lambda bi: (bi, 0, 0)),
                  pl.BlockSpec((1, t, W_A), lambda bi: (bi, 0, 1)),
                  pl.BlockSpec((1, t, W_A), lambda bi: (bi, 0, 2)),
                  pl.BlockSpec((None, None, rows, HD_A), lambda bi: (layer, bi, 0, 0))],
        out_specs=pl.BlockSpec((1, t, W_A), lambda bi: (bi, 0, 0)),
        out_shape=jax.ShapeDtypeStruct((b, t, W_A), BF16),
        compiler_params=_cparams("parallel"),
        name="attention_sample",
    )(za, za, za, cache2d)


def _cast_kernel(x_ref, o_ref):
    o_ref[...] = x_ref[...].astype(o_ref.dtype)


CAST_BLOCK_BYTES = 6 * 1024 * 1024


def _cast_bf16(w, col0=0, ncols=None):
    depth, k, n = w.shape
    ncols = n - col0 if ncols is None else ncols
    tn = next(c for c in (1024, 512, 256, LANES) if ncols % c == 0 and col0 % c == 0)
    packed_rows = 2 * SUBLANES
    tk = max(d for d in range(packed_rows, k + 1, packed_rows) if k % d == 0 and d * tn * 4 <= CAST_BLOCK_BYTES)
    c0 = col0 // tn
    return pl.pallas_call(
        _cast_kernel,
        grid=(depth, k // tk, ncols // tn),
        in_specs=[pl.BlockSpec((1, tk, tn), lambda l, i, j: (l, i, c0 + j))],
        out_specs=pl.BlockSpec((1, tk, tn), lambda l, i, j: (l, i, j)),
        out_shape=jax.ShapeDtypeStruct((depth, k, ncols), BF16),
        compiler_params=_cparams("parallel", "parallel", "parallel"),
        name="cast_bf16",
    )(w)


def _cast_t_kernel(x_ref, o_ref):
    o_ref[0] = x_ref[0].T.astype(o_ref.dtype)


def _cast_bf16_t(wt, row0, nrows, tn=512):
    depth, _, k = wt.shape
    assert row0 % tn == 0 and nrows % tn == 0
    tk = 1024 if k % 1024 == 0 else k
    r0 = row0 // tn
    return pl.pallas_call(
        _cast_t_kernel,
        grid=(depth, k // tk, nrows // tn),
        in_specs=[pl.BlockSpec((1, tn, tk), lambda l, i, j: (l, r0 + j, i))],
        out_specs=pl.BlockSpec((1, tk, tn), lambda l, i, j: (l, i, j)),
        out_shape=jax.ShapeDtypeStruct((depth, k, nrows), BF16),
        compiler_params=_cparams("parallel", "parallel", "parallel"),
        name="cast_bf16_t",
    )(wt)


def _cast_tail_t_kernel(a_ref, b_ref, d_ref, g_ref, *, shift):
    a = a_ref[0]
    tn = a.shape[0]
    x = jnp.concatenate([a, b_ref[0]], axis=0)
    d_ref[0] = x[shift:shift + tn].T.astype(d_ref.dtype)

    @pl.when(pl.program_id(2) == 0)
    def _():
        lane = lax.broadcasted_iota(jnp.int32, (a.shape[1], LANES), 1)
        g_ref[0] = jnp.where(lane < shift, a[0:LANES].T, 0.0).astype(g_ref.dtype)


def _cast_tail_t(wt, row0, shift, tn=512):
    depth, n, k = wt.shape
    nrows = n - row0 - shift
    packed_rows = 2 * SUBLANES
    assert row0 % tn == 0 and nrows % tn == 0 and shift % packed_rows == 0 and 0 < shift <= LANES
    tk = 1024 if k % 1024 == 0 else k
    r0 = row0 // tn
    per = tn // shift
    return pl.pallas_call(
        functools.partial(_cast_tail_t_kernel, shift=shift),
        grid=(depth, k // tk, nrows // tn),
        in_specs=[pl.BlockSpec((1, tn, tk), lambda l, i, j: (l, r0 + j, i)),
                  pl.BlockSpec((1, shift, tk), lambda l, i, j: (l, (r0 + j + 1) * per, i))],
        out_specs=[pl.BlockSpec((1, tk, tn), lambda l, i, j: (l, i, j)),
                   pl.BlockSpec((1, tk, LANES), lambda l, i, j: (l, i, 0))],
        out_shape=[jax.ShapeDtypeStruct((depth, k, nrows), BF16), jax.ShapeDtypeStruct((depth, k, LANES), BF16)],
        compiler_params=_cparams("parallel", "parallel", "arbitrary"),
        name="cast_tail_t",
    )(wt, wt)


def _kv_pack_kernel(*refs):
    o_ref = refs[-1]
    depth = (len(refs) - 1) // 2
    w = refs[0].shape[2]
    for l in range(depth):
        @pl.when(pl.program_id(0) == l)
        def _(l=l):
            o_ref[0, :, 0:w] = refs[2 * l][0]
            o_ref[0, :, w:2 * w] = refs[2 * l + 1][0]


def _kv_pack(zas, rows):
    depth = len(zas)
    b, s, _ = zas[0].shape
    ts = _pick(rows, (1024, 512, 256, 128, 64, 8))
    first = (s - rows) // ts
    assert (s - rows) % ts == 0
    in_specs, args = [], []
    for l in range(depth):
        for part in (1, 2):
            def imap(d, bi, si, l=l, part=part):
                return (bi, jnp.where(d == l, first + si, first), part)
            in_specs.append(pl.BlockSpec((1, ts, W_A), imap))
            args.append(zas[l])
    out = pl.pallas_call(
        _kv_pack_kernel,
        grid=(depth, b, rows // ts),
        in_specs=in_specs,
        out_specs=pl.BlockSpec((None, 1, ts, 2 * W_A), lambda d, bi, si: (d, bi, si, 0)),
        out_shape=jax.ShapeDtypeStruct((depth, b, rows, 2 * W_A), F32),
        compiler_params=_cparams("parallel", "parallel", "parallel"),
        name="kv_pack",
    )(*args)
    return out.reshape(depth, b, rows, 2, H_A, HD_A)


def _kv_shift_kernel(c_ref, n_ref, o_ref):
    l = c_ref.shape[0]
    t = n_ref.shape[0]
    o_ref[0:l - t] = c_ref[t:l]
    o_ref[l - t:l] = n_ref[...]


def _kv_shift(cache, kv_new):
    depth, b, l, two, h, hd = cache.shape
    t = kv_new.shape[2]
    cblk = (None, None, l, None, h, hd)
    cmap = lambda d, bi, s: (d, bi, 0, s, 0, 0)
    return pl.pallas_call(
        _kv_shift_kernel,
        grid=(depth, b, two),
        in_specs=[pl.BlockSpec(cblk, cmap), pl.BlockSpec((None, None, t, None, h, hd), cmap)],
        out_specs=pl.BlockSpec(cblk, cmap),
        out_shape=jax.ShapeDtypeStruct(cache.shape, cache.dtype),
        compiler_params=_cparams("parallel", "parallel", "parallel"),
        name="kv_shift",
    )(cache, kv_new)


def _chunk_plan(t, chunk):
    c = chunk if t >= chunk else max(2 * SUBLANES, t)
    assert t % c == 0 or t < c
    tc = min(t, c)
    return c, tc, max(t // c, 1)


def _run_chunks(nchunks, c, chunk):
    if nchunks == 1:
        chunk(0, 0)
    else:
        def body(ci, carry):
            chunk(ci, pl.multiple_of(ci * c, c))
            return carry
        lax.fori_loop(0, nchunks, body, 0)


def _gla_kernel(q_ref, f_ref, i_ref, g_ref, lb_ref, nrm_ref, s0_ref, o_ref, sfin_ref, st_ref, *, layer, c, tc, nchunks):
    nlev = int(math.log2(c))
    assert 1 << nlev == c
    depth = lb_ref.shape[0]
    nh = st_ref.shape[0]
    lbs = [lb_ref[i] for i in range(depth)]
    mx = functools.reduce(jnp.maximum, lbs)
    es = [jnp.exp(x - mx) for x in lbs]
    tot = functools.reduce(lambda a, b: a + b, es)
    lower = functools.reduce(lambda a, b: a + b, [es[i] / tot for i in range(layer + 1)]) - es[0] / tot
    one_minus_lb = 1.0 - lower

    row = lax.broadcasted_iota(jnp.int32, (c, c), 0)
    col = lax.broadcasted_iota(jnp.int32, (c, c), 1)
    prefix = [_onehot(col <= row)]
    lmask, second = [], []
    row_hd = lax.broadcasted_iota(jnp.int32, (c, HD_B), 0)
    for lev in range(1, nlev + 1):
        s = c >> lev
        grp = ~(2 * s - 1)
        prefix.append(_onehot(col <= (row & grp) + (s - 1)))
        lmask.append((((row ^ col) & grp) == 0) & ((row & s) != 0) & ((col & s) == 0))
        second.append((row_hd & s) != 0)
    mstack = jnp.concatenate(prefix, axis=0)
    eye = row == col
    valid = lax.broadcasted_iota(jnp.int32, (c, 1), 0) < tc
    nrm = nrm_ref[...]

    @pl.when(pl.program_id(2) == 0)
    def _():
        for h in range(nh):
            st_ref[h] = s0_ref[0, h].T

    def chunk(ci, t0):
        rows = pl.ds(t0, tc)
        hs = range(nh)
        sls = [slice(h * HD_B, (h + 1) * HD_B) for h in hs]
        qs = [_silu(_pad_rows(q_ref[0, rows, sl], c)) for sl in sls]
        kbs = [one_minus_lb[:, sl] * _sigmoid(-_pad_rows(f_ref[0, rows, sl], c)) for sl in sls]
        logfs = [jnp.log1p(-kb) for kb in kbs]
        if tc < c:
            kbs = [jnp.where(valid, kb, 0.0) for kb in kbs]
            logfs = [jnp.where(valid, lf, 0.0) for lf in logfs]
        vs = [_pad_rows(i_ref[0, rows, sl], c).astype(BF16) for sl in sls]
        bs_all = _sel_dot(mstack, jnp.concatenate(logfs, axis=1))
        bss = [bs_all[:, sl] for sl in sls]
        b = [bs[0:c] for bs in bss]
        atts = [jnp.where(eye, jnp.sum(q * kb, axis=-1, keepdims=True), 0.0) for q, kb in zip(qs, kbs)]
        for lev in range(1, nlev + 1):
            brs = [bs[lev * c:(lev + 1) * c] for bs in bss]
            ws = [(jnp.where(second[lev - 1], qs[h], kbs[h]) * jnp.exp(-jnp.abs(b[h] - brs[h]))).astype(BF16)
                  for h in hs]
            prods = [_dg(w, w, _NT) for w in ws]
            atts = [att + jnp.where(lmask[lev - 1], pr, 0.0) for att, pr in zip(atts, prods)]
        sts = [st_ref[h] for h in hs]
        o_inter = [_dot1(qs[h] * jnp.exp(b[h]), sts[h], _NT) for h in hs]
        o_intra = [_dg(atts[h].astype(BF16), vs[h]) for h in hs]
        bends = [b[h][c - 1:c] for h in hs]
        upd = [_dg(vs[h], (kbs[h] * jnp.exp(bends[h] - b[h])).astype(BF16), _TN) for h in hs]
        for h in hs:
            st_ref[h] = sts[h] * jnp.exp(bends[h]) + upd[h]
            o = o_inter[h] + o_intra[h]
            on = o * lax.rsqrt(jnp.mean(o * o, axis=-1, keepdims=True) + EPS) * nrm
            out = on * _silu(_pad_rows(g_ref[0, rows, sls[h]], c))
            o_ref[0, rows, sls[h]] = out[0:tc].astype(o_ref.dtype)

    _run_chunks(nchunks, c, chunk)

    @pl.when(pl.program_id(2) == pl.num_programs(2) - 1)
    def _():
        for h in range(nh):
            sfin_ref[0, h] = st_ref[h].T


def _hgrn(zb, lb, nrm, s0, layer, chunk):
    b, t, _ = zb.shape
    tb = min(t, MIXER_ROW_BLOCK)
    assert t % tb == 0
    c, tc, nchunks = _chunk_plan(tb, chunk)
    depth = lb.shape[0]
    nh = GLA_HEADS_PER_STEP
    ng = H_B // nh
    blk = (1, tb, nh * HD_B)
    sblk = (1, nh, HD_B, HD_B)
    return pl.pallas_call(
        functools.partial(_gla_kernel, layer=layer, c=c, tc=tc, nchunks=nchunks),
        grid=(b, ng, t // tb),
        in_specs=[pl.BlockSpec(blk, lambda bi, g, ti: (bi, ti, g)),
                  pl.BlockSpec(blk, lambda bi, g, ti: (bi, ti, ng + g)),
                  pl.BlockSpec(blk, lambda bi, g, ti: (bi, ti, 2 * ng + g)),
                  pl.BlockSpec(blk, lambda bi, g, ti: (bi, ti, 3 * ng + g)),
                  pl.BlockSpec((depth, 1, nh * HD_B), lambda bi, g, ti: (0, 0, g)),
                  pl.BlockSpec((1, HD_B), lambda bi, g, ti: (0, 0)),
                  pl.BlockSpec(sblk, lambda bi, g, ti: (bi, g, 0, 0))],
        out_specs=[pl.BlockSpec(blk, lambda bi, g, ti: (bi, ti, g)),
                   pl.BlockSpec(sblk, lambda bi, g, ti: (bi, g, 0, 0))],
        out_shape=[jax.ShapeDtypeStruct((b, t, W_B), BF16), jax.ShapeDtypeStruct((b, H_B, HD_B, HD_B), F32)],
        scratch_shapes=[pltpu.VMEM((nh, HD_B, HD_B), F32)],
        compiler_params=_cparams("parallel", "parallel", "arbitrary"),
        name="hgrn2",
    )(zb, zb, zb, zb, lb.reshape(depth, 1, W_B), nrm.reshape(1, HD_B), s0)


def _delta_kernel(q_ref, k_ref, v_ref, z_ref, zg_ref, wq_ref, wk_ref, wv_ref, bq_ref, bk_ref, bv_ref,
                  hq_ref, hk_ref, hv_ref, al_ref, dl_ref, nrm_ref, s0_ref, o_ref, sfin_ref, st_ref,
                  *, c, tc, nchunks):
    g = pl.program_id(1)
    first_block = pl.program_id(2) == 0
    nqk = DELTA_QK_HEADS_PER_STEP
    rep = H_C_V // H_C_QK
    sb = min(DELTA_SUB, c)
    row = lax.broadcasted_iota(jnp.int32, (c, c), 0)
    col = lax.broadcasted_iota(jnp.int32, (c, c), 1)
    lower_incl = _onehot(col <= row)
    eye = jnp.where(row == col, 1.0, 0.0)
    same_sub = ((row ^ col) & ~(sb - 1)) == 0
    valid = lax.broadcasted_iota(jnp.int32, (c, 1), 0) < tc
    lane = lax.broadcasted_iota(jnp.int32, (c, LANES), 1)
    nrm = nrm_ref[...]

    @pl.when(first_block)
    def _():
        for h in range(nqk * rep):
            st_ref[h] = s0_ref[0, h].T

    def conv(x_ref, w_ref, buf_ref, halo_ref, sl, ci, t0):
        x = x_ref[0, pl.ds(t0, tc), sl]
        halo = jnp.where(first_block, buf_ref[0, :, sl], halo_ref[0, :, sl])
        if nchunks > 1:
            prev = x_ref[0, pl.ds(pl.multiple_of(jnp.maximum(t0 - SUBLANES, 0), SUBLANES), SUBLANES), sl]
            halo = jnp.where(ci == 0, halo, prev)
        xh = jnp.concatenate([halo, x], axis=0)
        w = w_ref[:, sl]
        y = w[C_CONV - 1:C_CONV] * x
        for s in range(1, C_CONV):
            y = y + w[C_CONV - 1 - s:C_CONV - s] * pltpu.roll(xh, s, 0)[SUBLANES:]
        return _pad_rows(_silu(y), c)

    def l2n(x):
        return x * lax.rsqrt(jnp.sum(x * x, axis=-1, keepdims=True) + EPS)

    def chunk(ci, t0):
        rows = pl.ds(t0, tc)
        zg = _pad_rows(zg_ref[0, rows, :], c)
        beta_all = _sigmoid(zg)
        la_all = -jnp.exp(al_ref[...]) * _softplus(zg + dl_ref[...])
        nv = nqk * rep
        hs = range(nv)
        qsls = [slice(qh * HD_C, (qh + 1) * HD_C) for qh in range(nqk)]
        vsls = [slice(h * HD_C, (h + 1) * HD_C) for h in hs]
        qs = [l2n(conv(q_ref, wq_ref, bq_ref, hq_ref, sl, ci, t0)) * HD_C ** -0.5 for sl in qsls]
        ks = [l2n(conv(k_ref, wk_ref, bk_ref, hk_ref, sl, ci, t0)) for sl in qsls]
        if tc < c:
            ks = [jnp.where(valid, k, 0.0) for k in ks]
        qbs = [q.astype(BF16) for q in qs]
        kbs = [k.astype(BF16) for k in ks]
        kks = [_dg(kb, kb, _NT) for kb in kbs]
        qks = [_dg(qb, kb, _NT) for qb, kb in zip(qbs, kbs)]
        vs = [conv(v_ref, wv_ref, bv_ref, hv_ref, sl, ci, t0) for sl in vsls]
        betas = [jnp.sum(jnp.where(lane == g * nv + h, beta_all, 0.0), axis=-1, keepdims=True) for h in hs]
        las = [jnp.sum(jnp.where(lane == H_C_V + g * nv + h, la_all, 0.0), axis=-1, keepdims=True) for h in hs]
        if tc < c:
            betas = [jnp.where(valid, x, 0.0) for x in betas]
            las = [jnp.where(valid, x, 0.0) for x in las]
        bc_all = _sel_dot(lower_incl, jnp.concatenate([jnp.broadcast_to(la, (c, LANES)) for la in las], axis=1))
        bcols = [bc_all[:, h * LANES:h * LANES + c] for h in hs]
        decs = [jnp.exp(jnp.where(row >= col, bc - bc.T, NEG)) for bc in bcols]
        b1s = [bc[:, 0:1] for bc in bcols]
        ebs = [jnp.exp(b1) for b1 in b1s]
        bends = [b1[c - 1:c] for b1 in b1s]
        ns = [jnp.where(row > col, betas[h] * (kks[h // rep] * decs[h]), 0.0) for h in hs]
        nds = [jnp.where(same_sub, n, 0.0) for n in ns]
        xs = [eye - nd for nd in nds]
        ps = nds
        for _ in range(int(math.log2(sb)) - 1):
            ps = [_dot1(p, p) for p in ps]
            xs = [x + _dot1(x, p) for x, p in zip(xs, ps)]
        nblk = c // sb
        if nblk > 1:
            mms = [_dot1(x, n - nd) for x, n, nd in zip(xs, ns, nds)]
            ys = [eye - mm for mm in mms]
            ps = mms
            for _ in range(int(math.log2(nblk)) - 1):
                ps = [_dot1(p, p) for p in ps]
                ys = [y + _dot1(y, p) for y, p in zip(ys, ps)]
            xs = [_dot1(y, x) for y, x in zip(ys, xs)]
        sts = [st_ref[h] for h in hs]
        stbs = [st.astype(BF16) for st in sts]
        ksts = [_dg(kbs[h // rep], stbs[h], _NT) for h in hs]
        qsts = [_dg(qbs[h // rep], stbs[h], _NT) for h in hs]
        rhss = [betas[h] * (vs[h] - ebs[h] * ksts[h]) for h in hs]
        ubs = [_dot1(x, rhs).astype(BF16) for x, rhs in zip(xs, rhss)]
        o_intra = [_dg((qks[h // rep] * decs[h]).astype(BF16), ubs[h]) for h in hs]
        upd = [_dg(ubs[h], (ks[h // rep] * jnp.exp(bends[h] - b1s[h])).astype(BF16), _TN) for h in hs]
        for h in hs:
            st_ref[h] = jnp.exp(bends[h]) * sts[h] + upd[h]
            o = ebs[h] * qsts[h] + o_intra[h]
            on = o * lax.rsqrt(jnp.mean(o * o, axis=-1, keepdims=True) + EPS) * nrm
            out = on * _silu(_pad_rows(z_ref[0, rows, vsls[h]], c))
            o_ref[0, rows, vsls[h]] = out[0:tc].astype(o_ref.dtype)

    _run_chunks(nchunks, c, chunk)

    @pl.when(pl.program_id(2) == pl.num_programs(2) - 1)
    def _():
        for h in range(nqk * rep):
            sfin_ref[0, h] = st_ref[h].T


def _delta(zc, zg, conv_w, buf, a_log, dt_bias, nrm, s0, chunk):
    b, t, _ = zc.shape
    tb = min(t, MIXER_ROW_BLOCK)
    assert t % tb == 0 and tb % SUBLANES == 0
    c, tc, nchunks = _chunk_plan(tb, chunk)
    rep = H_C_V // H_C_QK
    nqk = DELTA_QK_HEADS_PER_STEP
    nv = nqk * rep
    ng = H_C_QK // nqk
    qw = nqk * HD_C
    vw = nv * HD_C
    assert W_CQK % qw == 0 and (2 * W_CQK) % vw == 0
    kq0 = W_CQK // qw
    v0 = 2 * W_CQK // vw
    z0 = v0 + W_CV // vw
    buf8 = jnp.pad(buf, ((0, 0), (SUBLANES - (C_CONV - 1), 0), (0, 0)))
    pad_l = jnp.zeros((H_C_V,), F32)
    a_lane = jnp.pad(jnp.concatenate([pad_l, a_log]), (0, LANES - 2 * H_C_V)).reshape(1, LANES)
    d_lane = jnp.pad(jnp.concatenate([pad_l, dt_bias]), (0, LANES - 2 * H_C_V)).reshape(1, LANES)
    hb = tb // SUBLANES

    def rows(col0):
        return lambda bi, g, ti: (bi, ti, col0 + g)

    def first(col0):
        return lambda bi, g, ti: (bi, 0, col0 + g)

    def halo(col0):
        return lambda bi, g, ti: (bi, jnp.maximum(ti * hb - 1, 0), col0 + g)

    full = lambda bi, g, ti: (0, 0)
    sblk = (1, nv, HD_C, HD_C)
    smap = lambda bi, g, ti: (bi, g, 0, 0)
    return pl.pallas_call(
        functools.partial(_delta_kernel, c=c, tc=tc, nchunks=nchunks),
        grid=(b, ng, t // tb),
        in_specs=[pl.BlockSpec((1, tb, qw), rows(0)), pl.BlockSpec((1, tb, qw), rows(kq0)),
                  pl.BlockSpec((1, tb, vw), rows(v0)), pl.BlockSpec((1, tb, vw), rows(z0)),
                  pl.BlockSpec((1, tb, W_CG), lambda bi, g, ti: (bi, ti, 0)),
                  pl.BlockSpec((C_CONV, qw), lambda bi, g, ti: (0, g)),
                  pl.BlockSpec((C_CONV, qw), lambda bi, g, ti: (0, kq0 + g)),
                  pl.BlockSpec((C_CONV, vw), lambda bi, g, ti: (0, v0 + g)),
                  pl.BlockSpec((1, SUBLANES, qw), first(0)), pl.BlockSpec((1, SUBLANES, qw), first(kq0)),
                  pl.BlockSpec((1, SUBLANES, vw), first(v0)),
                  pl.BlockSpec((1, SUBLANES, qw), halo(0)), pl.BlockSpec((1, SUBLANES, qw), halo(kq0)),
                  pl.BlockSpec((1, SUBLANES, vw), halo(v0)),
                  pl.BlockSpec((1, LANES), full), pl.BlockSpec((1, LANES), full),
                  pl.BlockSpec((1, HD_C), full),
                  pl.BlockSpec(sblk, smap)],
        out_specs=[pl.BlockSpec((1, tb, vw), rows(0)), pl.BlockSpec(sblk, smap)],
        out_shape=[jax.ShapeDtypeStruct((b, t, W_CV), BF16), jax.ShapeDtypeStruct((b, H_C_V, HD_C, HD_C), F32)],
        scratch_shapes=[pltpu.VMEM((nv, HD_C, HD_C), F32)],
        compiler_params=_cparams("parallel", "parallel", "arbitrary"),
        name="gated_deltanet",
    )(zc, zc, zc, zc, zg, conv_w, conv_w, conv_w, buf8, buf8, buf8, zc, zc, zc,
      a_lane, d_lane, nrm.reshape(1, HD_C), s0)


def _ret_kernel(q_ref, k_ref, v_ref, g_ref, nrm_ref, s0_ref, o_ref, sfin_ref, st_ref, *, c, tc, nchunks):
    nh = st_ref.shape[0]
    row = lax.broadcasted_iota(jnp.int32, (c, c), 0)
    col = lax.broadcasted_iota(jnp.int32, (c, c), 1)
    steps = (jnp.minimum(row + 1, tc) - jnp.minimum(col + 1, tc)).astype(F32)
    r1 = lax.broadcasted_iota(jnp.int32, (c, 1), 0)
    nsteps = jnp.minimum(r1 + 1, tc).astype(F32)
    valid = r1 < tc
    nrm = nrm_ref[...]
    lgs, decs = [], []
    for h in range(nh):
        hf = jnp.full((1, 1), pl.program_id(1) * nh + h, jnp.int32).astype(F32)
        lg = jnp.log1p(-jnp.exp2(-5.0 - hf))
        lgs.append(lg)
        decs.append(jnp.exp(jnp.where(row >= col, steps * lg, NEG)))
        st_ref[h] = s0_ref[0, h].T

    def chunk(ci, t0):
        rows = pl.ds(t0, tc)
        hs = range(nh)
        sls = [slice(h * HD_D, (h + 1) * HD_D) for h in hs]
        b1s = [nsteps * lgs[h] for h in hs]
        bends = [float(tc) * lgs[h] for h in hs]
        qs = [_pad_rows(q_ref[0, rows, sl], c).astype(BF16) for sl in sls]
        ks = [_pad_rows(k_ref[0, rows, sl], c) * HD_D ** -0.5 for sl in sls]
        if tc < c:
            ks = [jnp.where(valid, k, 0.0) for k in ks]
        vs = [_pad_rows(v_ref[0, rows, sl], c).astype(BF16) for sl in sls]
        sts = [st_ref[h] for h in hs]
        atts = [_dg(qs[h], ks[h].astype(BF16), _NT) * decs[h] for h in hs]
        o_inter = [_dg(qs[h], sts[h].astype(BF16), _NT) for h in hs]
        o_intra = [_dg(atts[h].astype(BF16), vs[h]) for h in hs]
        upd = [_dg(vs[h], (ks[h] * jnp.exp(bends[h] - b1s[h])).astype(BF16), _TN) for h in hs]
        for h in hs:
            sl = sls[h]
            st_ref[h] = jnp.exp(bends[h]) * sts[h] + upd[h]
            o = jnp.exp(b1s[h]) * o_inter[h] + o_intra[h]
            mu = jnp.mean(o, axis=-1, keepdims=True)
            oc = o - mu
            var = jnp.mean(oc * oc, axis=-1, keepdims=True)
            on = oc * lax.rsqrt(var + EPS) * nrm
            out = on * _silu(_pad_rows(g_ref[0, rows, sl], c))
            o_ref[0, rows, sl] = out[0:tc].astype(o_ref.dtype)

    _run_chunks(nchunks, c, chunk)
    for h in range(nh):
        sfin_ref[0, h] = st_ref[h].T


def _retention(zd, nrm, s0, chunk):
    b, t, _ = zd.shape
    c, tc, nchunks = _chunk_plan(t, chunk)
    nh = RET_HEADS_PER_STEP
    ng = H_D // nh
    blk = (1, t, nh * HD_D)
    sblk = (1, nh, HD_D, HD_D)
    return pl.pallas_call(
        functools.partial(_ret_kernel, c=c, tc=tc, nchunks=nchunks),
        grid=(b, ng),
        in_specs=[pl.BlockSpec(blk, lambda bi, g: (bi, 0, g)),
                  pl.BlockSpec(blk, lambda bi, g: (bi, 0, ng + g)),
                  pl.BlockSpec(blk, lambda bi, g: (bi, 0, 2 * ng + g)),
                  pl.BlockSpec(blk, lambda bi, g: (bi, 0, 3 * ng + g)),
                  pl.BlockSpec((1, HD_D), lambda bi, g: (0, 0)),
                  pl.BlockSpec(sblk, lambda bi, g: (bi, g, 0, 0))],
        out_specs=[pl.BlockSpec(blk, lambda bi, g: (bi, 0, g)),
                   pl.BlockSpec(sblk, lambda bi, g: (bi, g, 0, 0))],
        out_shape=[jax.ShapeDtypeStruct((b, t, W_D), BF16), jax.ShapeDtypeStruct((b, H_D, HD_D, HD_D), F32)],
        scratch_shapes=[pltpu.VMEM((nh, HD_D, HD_D), F32)],
        compiler_params=_cparams("parallel", "arbitrary"),
        name="retention",
    )(zd, zd, zd, zd, nrm.reshape(1, HD_D), s0)


def _mixers(z, cache, s_hgrn, s_delta, buf_delta, s_ret, p, layer):
    za, zb, zc, zg, zd = z
    bsz, t, _ = za.shape
    if cache is None:
        o_a = _attention_prompt(za)
        kv_new = za
    else:
        o_a = _attention_sample(za, cache, layer)
        kv_new = za[:, :, W_A:].reshape(bsz, t, 2, H_A, HD_A)
    o_b, s_hgrn_new = _hgrn(zb, p['hgrn_lb'], p['hgrn_norm'][layer], s_hgrn, layer, GLA_CHUNK)
    o_c, s_delta_new = _delta(zc, zg, p['delta_conv'][layer], buf_delta, p['delta_A_log'][layer],
                              p['delta_dt_bias'][layer], p['delta_norm'][layer], s_delta, DELTA_CHUNK)
    pre = jnp.concatenate([buf_delta, zc[:, :, :2 * W_CQK + W_CV]], axis=1)
    buf_delta_new = pre[:, pre.shape[1] - (C_CONV - 1):]
    o_d, s_ret_new = _retention(zd, p['ret_norm'][layer], s_ret, RET_CHUNK)
    parts = [o.reshape(bsz * t, -1) for o in (o_a, o_b, o_c, o_d)]
    return parts, (kv_new, s_hgrn_new, s_delta_new, buf_delta_new, s_ret_new)


def _layer(xs, pes, cache, states, p, layer):
    shapes = [x.shape for x in xs]
    d = shapes[0][2]
    x2 = [x.reshape(-1, d) for x in xs]
    hn = [_rmsnorm(x, p['attn_norm'][layer], BF16) for x in x2]
    z = [_matmul(hn[0], hn[1], p[name], layer) for name in ('w_in_a', 'w_in_b', 'w_in_c', 'w_in_g', 'w_in_d')]
    parts, mixed = [], []
    for g in range(2):
        bsz, t, _ = shapes[g]
        zg = [zz[g].reshape(bsz, t, -1) for zz in z]
        s_hgrn, s_delta, buf_delta, s_ret, _ = states[g]
        pg, mg = _mixers(zg, cache if g == 1 else None, s_hgrn, s_delta, buf_delta, s_ret, p, layer)
        parts.append(pg)
        mixed.append(mg)
    x2 = _matmul_residual(parts[0], parts[1], p['w_out'], layer, x2[0], x2[1],
                          (1024, 512, 256, 64), (512, 256, 128))

    hf = [_rmsnorm(x, p['ffn_norm'][layer], BF16) for x in x2]
    act_p, tail_p, act_s, tail_s, w_down_bf16 = _ffn_gate_up(
        hf[0], hf[1], p['w_gate'], p['w_up'], p['ffn_conv'], p['w_down'], layer,
        shapes[0][1], states[1][4], shapes[1][1])
    x2 = _matmul_residual([act_p], [act_s], w_down_bf16, 0, x2[0], x2[1], (512, 256, 64), (256, 128))

    hp = [_rmsnorm(x, p['ple_norm'][layer], BF16) for x in x2]
    pe2 = [pe.reshape(-1, pe.shape[-1]).astype(BF16) for pe in pes]
    x2 = _ple(hp[0], hp[1], p['ple_gate'], pe2[0], pe2[1], p['ple_proj'], layer, x2[0], x2[1])
    new_states = [mixed[0] + (tail_p,), mixed[1] + (tail_s,)]
    return [x.reshape(s) for x, s in zip(x2, shapes)], new_states


def kernel(x_prompt, x_sample, cache_attn_kv, state_hgrn, state_delta, state_delta_conv, state_ret,
           state_ffn_conv, p_prompt, p_sample, attn_norm, w_in, hgrn_lb, hgrn_norm, delta_conv,
           delta_A_log, delta_dt_bias, delta_norm, ret_norm, w_out, ffn_norm, w_gate, w_up, ffn_conv,
           w_down, ple_norm, ple_gate, ple_proj, final_norm):
    depth = w_in.shape[0]
    bp = x_prompt.shape[0]
    o_b = 3 * W_A
    o_c = o_b + 4 * W_B
    o_g = o_c + 2 * W_CQK + 2 * W_CV
    o_d = o_g + 2 * H_C_V
    w_in_t = jnp.swapaxes(w_in, 1, 2)
    w_in_d, w_in_g = _cast_tail_t(w_in_t, o_g, o_d - o_g)
    p = {'attn_norm': attn_norm,
         'w_in_a': _cast_bf16_t(w_in_t, 0, o_b), 'w_in_b': _cast_bf16_t(w_in_t, o_b, o_c - o_b),
         'w_in_c': _cast_bf16_t(w_in_t, o_c, o_g - o_c), 'w_in_g': w_in_g, 'w_in_d': w_in_d,
         'hgrn_lb': hgrn_lb, 'hgrn_norm': hgrn_norm, 'delta_conv': delta_conv, 'delta_A_log': delta_A_log,
         'delta_dt_bias': delta_dt_bias, 'delta_norm': delta_norm, 'ret_norm': ret_norm,
         'w_out': w_out, 'ffn_norm': ffn_norm, 'w_gate': w_gate, 'w_up': w_up, 'ffn_conv': ffn_conv,
         'w_down': w_down, 'ple_norm': ple_norm, 'ple_gate': ple_gate, 'ple_proj': ple_proj}
    xs = [x_prompt, x_sample]
    st_p, st_s = [], []
    for l in range(depth):
        zero = lambda *s: jnp.zeros((bp,) + s, F32)
        states = [(zero(H_B, HD_B, HD_B), zero(H_C_V, HD_C, HD_C), zero(C_CONV - 1, 2 * W_CQK + W_CV),
                   zero(H_D, HD_D, HD_D), None),
                  (state_hgrn[l], state_delta[l], state_delta_conv[l], state_ret[l], state_ffn_conv[l])]
        xs, (sp, ss) = _layer(xs, [p_prompt[l], p_sample[l]], cache_attn_kv, states, p, l)
        st_p.append(sp)
        st_s.append(ss)
    xp, xs = xs

    def stack(sts, i):
        return jnp.stack([s[i] for s in sts])

    def final(x):
        return _rmsnorm(x.reshape(-1, x.shape[-1]), final_norm, F32).reshape(x.shape)

    kv_sample = _kv_shift(cache_attn_kv, stack(st_s, 0))
    s_p = x_prompt.shape[1]
    kv_prompt = _kv_pack([s[0] for s in st_p], min(A_BRANCHES[-1][0], s_p))
    return (final(xp), final(xs),
            kv_prompt, kv_sample, stack(st_p, 1), stack(st_s, 1), stack(st_p, 2), stack(st_s, 2),
            stack(st_p, 3), stack(st_s, 3), stack(st_p, 4), stack(st_s, 4), stack(st_p, 5), stack(st_s, 5))
```

```python
import functools
import math

import jax
import jax.numpy as jnp
from jax import lax
from jax.experimental import pallas as pl
from jax.experimental.pallas import tpu as pltpu

F32 = jnp.float32
BF16 = jnp.bfloat16
EPS = 1e-6
NEG = -1e30

H_A = 8
HD_A = 128
A_BRANCHES = ((128, 1), (512, 4), (2048, 16))
A_BLOCK = 128
H_B = 8
HD_B = 128
H_C_QK = 4
H_C_V = 8
HD_C = 128
C_CONV = 4
H_D = 4
HD_D = 256
FFN_CONV = 3

W_A = H_A * HD_A
W_B = H_B * HD_B
W_CQK = H_C_QK * HD_C
W_CV = H_C_V * HD_C
W_D = H_D * HD_D
W_CG = 128

VMEM_LIMIT_BYTES = 52 * 1024 * 1024
SUBLANES = 8
LANES = 128

GLA_HEADS_PER_STEP = 8
DELTA_QK_HEADS_PER_STEP = 4
RET_HEADS_PER_STEP = 2
MIXER_ROW_BLOCK = 1024
GLA_CHUNK = 64
DELTA_CHUNK = 128
RET_CHUNK = 256
DELTA_SUB = 16

_NN = (((1,), (0,)), ((), ()))
_NT = (((1,), (1,)), ((), ()))
_TN = (((0,), (0,)), ((), ()))


def _cparams(*sem):
    return pltpu.CompilerParams(dimension_semantics=sem, vmem_limit_bytes=VMEM_LIMIT_BYTES)


def _dg(a, b, dn=_NN):
    return lax.dot_general(a, b, dn, preferred_element_type=F32)


def _dot1(a, b, dn=_NN):
    return _dg(a.astype(BF16), b.astype(BF16), dn)


def _split3(x):
    x1 = x.astype(BF16)
    r1 = x - x1.astype(F32)
    x2 = r1.astype(BF16)
    x3 = (r1 - x2.astype(F32)).astype(BF16)
    return x1, x2, x3


def _sel_dot(m, x):
    x1, x2, x3 = _split3(x)
    return _dg(m, x1) + (_dg(m, x2) + _dg(m, x3))


def _sigmoid(x):
    return 1.0 / (1.0 + jnp.exp(-x))


def _silu(x):
    return x * _sigmoid(x)


def _softplus(x):
    return jnp.maximum(x, 0.0) + jnp.log1p(jnp.exp(-jnp.abs(x)))


def _pad_rows(x, rows):
    if x.shape[0] == rows:
        return x
    return jnp.concatenate([x, jnp.zeros((rows - x.shape[0],) + x.shape[1:], x.dtype)], axis=0)


def _onehot(cond):
    return jnp.where(cond, 1.0, 0.0).astype(BF16)


def _pick(n, prefs):
    for p in prefs:
        if n % p == 0:
            return p
    return n


def _rmsnorm_kernel(x_ref, g_ref, o_ref):
    x = x_ref[...]
    y = x * lax.rsqrt(jnp.mean(x * x, axis=-1, keepdims=True) + EPS)
    o_ref[...] = (y * g_ref[...]).astype(o_ref.dtype)


def _rmsnorm(x, g, out_dtype):
    m, d = x.shape
    tm = _pick(m, (256, 64, 8))
    return pl.pallas_call(
        _rmsnorm_kernel,
        grid=(m // tm,),
        in_specs=[pl.BlockSpec((tm, d), lambda i: (i, 0)), pl.BlockSpec((1, d), lambda i: (0, 0))],
        out_specs=pl.BlockSpec((tm, d), lambda i: (i, 0)),
        out_shape=jax.ShapeDtypeStruct((m, d), out_dtype),
        compiler_params=_cparams("parallel"),
        name="rmsnorm",
    )(x, g.reshape(1, d))


def _rider_map(nj):
    return lambda i, j: (0, jnp.where(i == 0, j, nj - 1))


def _both_groups(a_ref, as_ref):
    return jnp.concatenate([a_ref[...], as_ref[...]], axis=0)


def _mm_kernel(a_ref, w_ref, as_ref, o_ref, os_ref):
    tm = a_ref.shape[0]
    first = pl.program_id(0) == 0

    @pl.when(first)
    def _():
        r = _dg(_both_groups(a_ref, as_ref), w_ref[...].astype(BF16))
        o_ref[...] = r[:tm]
        os_ref[...] = r[tm:]

    @pl.when(jnp.logical_not(first))
    def _():
        o_ref[...] = _dg(a_ref[...], w_ref[...].astype(BF16))


def _matmul(a, a_s, w, layer):
    m, k = a.shape
    ms = a_s.shape[0]
    n = w.shape[2]
    tm = _pick(m, (1024, 512, 256, 64))
    tn = _pick(n, (512, 256, 128))
    nj = n // tn
    return pl.pallas_call(
        _mm_kernel,
        grid=(m // tm, nj),
        in_specs=[pl.BlockSpec((tm, k), lambda i, j: (i, 0)),
                  pl.BlockSpec((None, k, tn), lambda i, j: (layer, 0, j)),
                  pl.BlockSpec((ms, k), lambda i, j: (0, 0))],
        out_specs=[pl.BlockSpec((tm, tn), lambda i, j: (i, j)), pl.BlockSpec((ms, tn), _rider_map(nj))],
        out_shape=[jax.ShapeDtypeStruct((m, n), F32), jax.ShapeDtypeStruct((ms, n), F32)],
        compiler_params=_cparams("arbitrary", "arbitrary"),
        name="matmul",
    )(a, w, a_s)


def _mm_res_kernel(*refs, nparts):
    a_refs = refs[:nparts]
    w_refs = refs[nparts:2 * nparts]
    x_ref = refs[2 * nparts]
    as_refs = refs[2 * nparts + 1:3 * nparts + 1]
    xs_ref, o_ref, os_ref = refs[3 * nparts + 1:]
    tm = x_ref.shape[0]
    first = pl.program_id(0) == 0

    @pl.when(first)
    def _():
        acc = jnp.concatenate([x_ref[...], xs_ref[...]], axis=0)
        for a_ref, as_ref, w_ref in zip(a_refs, as_refs, w_refs):
            acc = acc + _dg(_both_groups(a_ref, as_ref), w_ref[...].astype(BF16))
        o_ref[...] = acc[:tm]
        os_ref[...] = acc[tm:]

    @pl.when(jnp.logical_not(first))
    def _():
        acc = x_ref[...]
        for a_ref, w_ref in zip(a_refs, w_refs):
            acc = acc + _dg(a_ref[...], w_ref[...].astype(BF16))
        o_ref[...] = acc


def _matmul_residual(parts, parts_s, w, layer, x, x_s, tm_prefs, tn_prefs):
    m, n = x.shape
    ms = x_s.shape[0]
    kp = parts[0].shape[1]
    nparts = len(parts)
    tm = _pick(m, tm_prefs)
    tn = _pick(n, tn_prefs)
    nj = n // tn

    def wmap(p):
        return lambda i, j: (layer, p, j)

    in_specs = [pl.BlockSpec((tm, kp), lambda i, j: (i, 0)) for _ in parts]
    in_specs += [pl.BlockSpec((None, kp, tn), wmap(p)) for p in range(nparts)]
    in_specs += [pl.BlockSpec((tm, tn), lambda i, j: (i, j))]
    in_specs += [pl.BlockSpec((ms, kp), lambda i, j: (0, 0)) for _ in parts]
    in_specs += [pl.BlockSpec((ms, tn), _rider_map(nj))]
    return pl.pallas_call(
        functools.partial(_mm_res_kernel, nparts=nparts),
        grid=(m // tm, nj),
        in_specs=in_specs,
        out_specs=[pl.BlockSpec((tm, tn), lambda i, j: (i, j)), pl.BlockSpec((ms, tn), _rider_map(nj))],
        out_shape=[jax.ShapeDtypeStruct((m, n), F32), jax.ShapeDtypeStruct((ms, n), F32)],
        compiler_params=_cparams("arbitrary", "arbitrary"),
        name="matmul_residual",
    )(*parts, *([w] * nparts), x, *parts_s, x_s)


def _ple_kernel(a_ref, wg_ref, pe_ref, wp_ref, x_ref, as_ref, pes_ref, xs_ref, o_ref, os_ref):
    tm = x_ref.shape[0]
    first = pl.program_id(0) == 0

    @pl.when(first)
    def _():
        x = jnp.concatenate([x_ref[...], xs_ref[...]], axis=0)
        gate = _sigmoid(_dg(_both_groups(a_ref, as_ref), wg_ref[...].astype(BF16)))
        r = x + gate * _dg(_both_groups(pe_ref, pes_ref), wp_ref[...].astype(BF16))
        o_ref[...] = r[:tm]
        os_ref[...] = r[tm:]

    @pl.when(jnp.logical_not(first))
    def _():
        gate = _sigmoid(_dg(a_ref[...], wg_ref[...].astype(BF16)))
        o_ref[...] = x_ref[...] + gate * _dg(pe_ref[...], wp_ref[...].astype(BF16))


def _ple(hp, hp_s, wg, pe, pe_s, wp, layer, x, x_s):
    m, d = x.shape
    ms = x_s.shape[0]
    k = hp.shape[1]
    kp = pe.shape[1]
    tm = _pick(m, (1024, 512, 256, 64))
    tn = _pick(d, (512, 256, 128))
    nj = d // tn
    return pl.pallas_call(
        _ple_kernel,
        grid=(m // tm, nj),
        in_specs=[pl.BlockSpec((tm, k), lambda i, j: (i, 0)),
                  pl.BlockSpec((None, k, tn), lambda i, j: (layer, 0, j)),
                  pl.BlockSpec((tm, kp), lambda i, j: (i, 0)),
                  pl.BlockSpec((None, kp, tn), lambda i, j: (layer, 0, j)),
                  pl.BlockSpec((tm, tn), lambda i, j: (i, j)),
                  pl.BlockSpec((ms, k), lambda i, j: (0, 0)),
                  pl.BlockSpec((ms, kp), lambda i, j: (0, 0)),
                  pl.BlockSpec((ms, tn), _rider_map(nj))],
        out_specs=[pl.BlockSpec((tm, tn), lambda i, j: (i, j)), pl.BlockSpec((ms, tn), _rider_map(nj))],
        out_shape=[jax.ShapeDtypeStruct((m, d), F32), jax.ShapeDtypeStruct((ms, d), F32)],
        compiler_params=_cparams("arbitrary", "arbitrary"),
        name="ple",
    )(hp, wg, pe, wp, x, hp_s, pe_s, x_s)


def _ffn_act(g, p1, p2, cw, u):
    gc = cw[0:1] * p2 + cw[1:2] * p1 + cw[2:3] * g
    return (_silu(gc) * u).astype(BF16)


def _ffn_gu_kernel(h_ref, halo_ref, wg_ref, wu_ref, cw_ref, wd_ref, hs_ref, b1_ref, b2_ref,
                   o_ref, tail_ref, wdb_ref, os_ref, gs_ref, *, seq, t_s):
    tm = h_ref.shape[0]
    first = pl.program_id(0) == 0

    def prompt_rows(g, u, gh):
        seq_start = lax.rem(pl.program_id(0) * tm, seq) == 0
        gh = jnp.where(seq_start, 0.0, gh)
        row = lax.broadcasted_iota(jnp.int32, g.shape, 0)
        p1 = jnp.where(row == 0, gh[7:8], pltpu.roll(g, 1, 0))
        p2 = jnp.where(row == 0, gh[6:7], jnp.where(row == 1, gh[7:8], pltpu.roll(g, 2, 0)))
        o_ref[...] = _ffn_act(g, p1, p2, cw_ref[...], u)
        tail_ref[0] = g[tm - SUBLANES:tm]

    @pl.when(first)
    def _():
        wdb_ref[...] = wd_ref[...].astype(wdb_ref.dtype)
        wg = wg_ref[...].astype(BF16)
        wu = wu_ref[...].astype(BF16)
        cw = cw_ref[...]
        a = _both_groups(h_ref, hs_ref)
        g = _dg(a, wg)
        u = _dg(a, wu)
        prompt_rows(g[:tm], u[:tm], jnp.zeros((SUBLANES, g.shape[1]), F32))
        g_s = g[tm:]
        pos = lax.rem(lax.broadcasted_iota(jnp.int32, g_s.shape, 0), t_s)
        q1 = jnp.where(pos == 0, b1_ref[...], pltpu.roll(g_s, 1, 0))
        q2 = jnp.where(pos < 2, b2_ref[...], pltpu.roll(g_s, 2, 0))
        os_ref[...] = _ffn_act(g_s, q1, q2, cw, u[tm:])
        gs_ref[...] = g_s

    @pl.when(jnp.logical_not(first))
    def _():
        wdb_ref[...] = wd_ref[...].astype(wdb_ref.dtype)
        wg = wg_ref[...].astype(BF16)
        a = h_ref[...]
        prompt_rows(_dg(a, wg), _dg(a, wu_ref[...].astype(BF16)), _dg(halo_ref[...], wg))


def _ffn_gate_up(hf, hf_s, wg, wu, cw, wd, layer, seq, buf_s, t_s):
    m, d = hf.shape
    ms = hf_s.shape[0]
    f = wg.shape[2]
    tm = _pick(seq, (1024, 512, 256, 128, 64, 8))
    tf = _pick(f, (256, 128))
    hb = tm // SUBLANES
    nj = f // tf
    steps = (m // tm) * nj
    assert f % steps == 0 and (f // steps) % (2 * SUBLANES) == 0
    wd_rows = f // steps
    nb = ms // t_s
    zeros = jnp.zeros((nb, t_s - 1, f), F32)
    b1 = jnp.concatenate([buf_s[:, 1:2], zeros], axis=1).reshape(ms, f)
    b2 = jnp.concatenate([buf_s[:, 0:2], zeros[:, 1:]], axis=1).reshape(ms, f)
    wmap = lambda i, j: (layer, 0, j)
    rider = _rider_map(nj)
    out, tail, wd_bf16, out_s, g_s = pl.pallas_call(
        functools.partial(_ffn_gu_kernel, seq=seq, t_s=t_s),
        grid=(m // tm, nj),
        in_specs=[pl.BlockSpec((tm, d), lambda i, j: (i, 0)),
                  pl.BlockSpec((SUBLANES, d), lambda i, j: (jnp.maximum(i * hb - 1, 0), 0)),
                  pl.BlockSpec((None, d, tf), wmap),
                  pl.BlockSpec((None, d, tf), wmap),
                  pl.BlockSpec((None, FFN_CONV, tf), wmap),
                  pl.BlockSpec((None, wd_rows, d), lambda i, j: (layer, i * nj + j, 0)),
                  pl.BlockSpec((ms, d), lambda i, j: (0, 0)),
                  pl.BlockSpec((ms, tf), rider),
                  pl.BlockSpec((ms, tf), rider)],
        out_specs=[pl.BlockSpec((tm, tf), lambda i, j: (i, j)),
                   pl.BlockSpec((1, SUBLANES, tf), lambda i, j: (i, 0, j)),
                   pl.BlockSpec((None, wd_rows, d), lambda i, j: (0, i * nj + j, 0)),
                   pl.BlockSpec((ms, tf), rider),
                   pl.BlockSpec((ms, tf), rider)],
        out_shape=[jax.ShapeDtypeStruct((m, f), BF16),
                   jax.ShapeDtypeStruct((m // tm, SUBLANES, f), F32),
                   jax.ShapeDtypeStruct((1, f, d), BF16),
                   jax.ShapeDtypeStruct((ms, f), BF16),
                   jax.ShapeDtypeStruct((ms, f), F32)],
        compiler_params=_cparams("arbitrary", "arbitrary"),
        name="ffn_gate_up",
    )(hf, hf, wg, wu, cw, wd, hf_s, b1, b2)
    per_seq = seq // tm
    tail = tail.reshape(m // seq, per_seq, SUBLANES, f)[:, per_seq - 1, SUBLANES - (FFN_CONV - 1):]
    tail_s = g_s.reshape(nb, t_s, f)[:, t_s - (FFN_CONV - 1):]
    return out, tail, out_s, tail_s, wd_bf16


ATTN_HEADS_PER_STEP = 4


def _band_softmax_many(probs, dist_cf, dist_pf, valid_c):
    scs = [_dot1(pr[0], pr[1], _NT) for pr in probs]
    sps = [None if pr[2] is None else _dot1(pr[0], pr[2], _NT) for pr in probs]
    pcs, pps, ls, lses = [], [], [], []
    for pr, sc, sp in zip(probs, scs, sps):
        slope, prev_ok = pr[5], pr[6]
        sc = jnp.where(valid_c, sc - slope * dist_cf, NEG)
        if sp is None:
            m = jnp.max(sc, axis=-1, keepdims=True)
            pc = jnp.exp(sc - m)
            pp = None
            l = jnp.sum(pc, axis=-1, keepdims=True)
        else:
            sp = jnp.where(prev_ok, sp - slope * dist_pf, NEG)
            m = jnp.max(jnp.maximum(sc, sp), axis=-1, keepdims=True)
            pc = jnp.exp(sc - m)
            pp = jnp.exp(sp - m)
            l = jnp.sum(pc + pp, axis=-1, keepdims=True)
        pcs.append(pc)
        pps.append(pp)
        ls.append(l)
        lses.append(m + jnp.log(l))
    ocs = [_dot1(pc, pr[3]) for pc, pr in zip(pcs, probs)]
    ops = [None if pp is None else _dot1(pp, pr[4]) for pp, pr in zip(pps, probs)]
    outs = [(oc if op is None else oc + op) / l for oc, op, l in zip(ocs, ops, ls)]
    return outs, lses


def _band_kernel(*refs, s):
    nh = ATTN_HEADS_PER_STEP
    q_refs, k_refs, v_refs = refs[0:nh], refs[nh:2 * nh], refs[2 * nh:3 * nh]
    o_ref, o2_ref, o3_ref, l2_ref, l3_ref = refs[3 * nh:]
    nq = A_BLOCK
    hg = pl.program_id(1)
    qi = lax.broadcasted_iota(jnp.int32, (nq, nq), 0)
    kj = lax.broadcasted_iota(jnp.int32, (nq, nq), 1)
    dist_c = qi - kj
    dist_p = dist_c + nq
    valid_c = dist_c >= 0
    dist_cf = dist_c.astype(F32)
    dist_pf = dist_p.astype(F32)
    (w1, d1), (w2, d2), (w3, d3) = A_BRANCHES
    assert d1 == 1 and s % (d2 * nq) == 0 and s == d3 * nq and max(w1 // d1, w2 // d2, w3 // d3) <= nq

    def problem(h, rows_c, rows_p, has_prev, band, dil):
        hv = jnp.full((1, 1), hg * nh + h + 1, jnp.int32).astype(F32)
        slope = float(dil) * jnp.exp2(-8.0 * hv / H_A)
        q = q_refs[h][0, rows_c, :] * HD_A ** -0.5
        if rows_p is None:
            return (q, k_refs[h][0, rows_c, :], None, v_refs[h][0, rows_c, :], None, slope, None)
        return (q, k_refs[h][0, rows_c, :], k_refs[h][0, rows_p, :], v_refs[h][0, rows_c, :],
                v_refs[h][0, rows_p, :], slope, (dist_p <= band) & has_prev)

    def dilated(i, carry):
        rows3 = pl.ds(i, nq, stride=d3)
        r = lax.rem(i, d2)
        b = i // d2
        rows2 = pl.ds(b * (d2 * nq) + r, nq, stride=d2)
        rows2p = pl.ds(jnp.maximum(b - 1, 0) * (d2 * nq) + r, nq, stride=d2)
        probs = [problem(h, rows3, None, False, w3 // d3, d3) for h in range(nh)]
        probs += [problem(h, rows2, rows2p, b > 0, w2 // d2, d2) for h in range(nh)]
        outs, lses = _band_softmax_many(probs, dist_cf, dist_pf, valid_c)
        for h in range(nh):
            o3_ref[h, rows3, :] = outs[h]
            l3_ref[h, rows3, :] = jnp.broadcast_to(lses[h], (nq, HD_A))
            o2_ref[h, rows2, :] = outs[nh + h]
            l2_ref[h, rows2, :] = jnp.broadcast_to(lses[nh + h], (nq, HD_A))
        return carry

    lax.fori_loop(0, s // nq, dilated, 0)

    def dense(i, carry):
        blocks = []
        for j in range(2):
            bi = 2 * i + j
            rows = pl.ds(pl.multiple_of(bi * nq, nq), nq)
            rows_p = pl.ds(pl.multiple_of(jnp.maximum(bi - 1, 0) * nq, nq), nq)
            blocks += [(h, rows, problem(h, rows, rows_p, bi > 0, w1 // d1, d1)) for h in range(nh)]
        outs, lses = _band_softmax_many([blk[2] for blk in blocks], dist_cf, dist_pf, valid_c)
        for (h, rows, _), o1, l1 in zip(blocks, outs, lses):
            l2 = l2_ref[h, rows, :]
            l3 = l3_ref[h, rows, :]
            mx = jnp.maximum(l1, jnp.maximum(l2, l3))
            e1 = jnp.exp(l1 - mx)
            e2 = jnp.exp(l2 - mx)
            e3 = jnp.exp(l3 - mx)
            o = (e1 * o1 + e2 * o2_ref[h, rows, :] + e3 * o3_ref[h, rows, :]) / (e1 + e2 + e3)
            o_ref[0, rows, h * HD_A:(h + 1) * HD_A] = o.astype(o_ref.dtype)
        return carry

    assert (s // nq) % 2 == 0
    lax.fori_loop(0, s // (2 * nq), dense, 0)


def _attention_prompt(za):
    b, s, _ = za.shape
    nh = ATTN_HEADS_PER_STEP
    ng = H_A // nh
    blk = (1, s, HD_A)

    def head_spec(first, h):
        return pl.BlockSpec(blk, lambda bi, g: (bi, 0, first + g * nh + h))

    in_specs = [head_spec(part * H_A, h) for part in range(3) for h in range(nh)]
    return pl.pallas_call(
        functools.partial(_band_kernel, s=s),
        grid=(b, ng),
        in_specs=in_specs,
        out_specs=pl.BlockSpec((1, s, nh * HD_A), lambda bi, g: (bi, 0, g)),
        out_shape=jax.ShapeDtypeStruct((b, s, W_A), BF16),
        scratch_shapes=[pltpu.VMEM((nh, s, HD_A), F32) for _ in range(4)],
        compiler_params=_cparams("parallel", "arbitrary"),
        name="band_attention",
    )(*([za] * (3 * nh)))


def _attn_sample_kernel(q_ref, kn_ref, vn_ref, c_ref, o_ref):
    t = q_ref.shape[1]
    rows_per_pos = 2 * H_A
    n_past = c_ref.shape[0] // rows_per_pos
    tp = 2 * SUBLANES
    shape_c = (tp, n_past)
    shape_n = (tp, tp)
    dist_c = n_past + lax.broadcasted_iota(jnp.int32, shape_c, 0) - lax.broadcasted_iota(jnp.int32, shape_c, 1)
    dist_n = lax.broadcasted_iota(jnp.int32, shape_n, 0) - lax.broadcasted_iota(jnp.int32, shape_n, 1)
    new_ok = lax.broadcasted_iota(jnp.int32, shape_n, 1) < t
    dist_cf = dist_c.astype(F32)
    dist_nf = dist_n.astype(F32)
    oks = []
    for window, dil in A_BRANCHES:
        oks.append(((dist_c <= window) & (lax.rem(dist_c, dil) == 0),
                    (dist_n >= 0) & (dist_n <= window) & (lax.rem(dist_n, dil) == 0) & new_ok))
    hs = range(H_A)
    slopes = [2.0 ** (-8.0 * (h + 1) / H_A) for h in hs]
    sls = [slice(h * HD_A, (h + 1) * HD_A) for h in hs]
    qs = [_pad_rows(q_ref[0, :, sl] * HD_A ** -0.5, tp).astype(BF16) for sl in sls]
    kns = [_pad_rows(kn_ref[0, :, sl], tp).astype(BF16) for sl in sls]
    vns = [_pad_rows(vn_ref[0, :, sl], tp).astype(BF16) for sl in sls]
    kcs = [c_ref[pl.ds(h, n_past, stride=rows_per_pos), :].astype(BF16) for h in hs]
    vcs = [c_ref[pl.ds(H_A + h, n_past, stride=rows_per_pos), :].astype(BF16) for h in hs]
    b_cs = [_dg(qs[h], kcs[h], _NT) - slopes[h] * dist_cf for h in hs]
    b_ns = [_dg(qs[h], kns[h], _NT) - slopes[h] * dist_nf for h in hs]
    pcs, pns, ls, lses = [], [], [], []
    for h in hs:
        for ok_c, ok_n in oks:
            sc = jnp.where(ok_c, b_cs[h], NEG)
            sn = jnp.where(ok_n, b_ns[h], NEG)
            m = jnp.maximum(jnp.max(sc, axis=-1, keepdims=True), jnp.max(sn, axis=-1, keepdims=True))
            pc = jnp.exp(sc - m)
            pn = jnp.exp(sn - m)
            pcs.append(pc.astype(BF16))
            pns.append(pn.astype(BF16))
            ls.append(jnp.sum(pc, axis=-1, keepdims=True) + jnp.sum(pn, axis=-1, keepdims=True))
            lses.append(m + jnp.log(ls[-1]))
    nbr = len(oks)
    outs = [(_dg(pcs[i], vcs[i // nbr]) + _dg(pns[i], vns[i // nbr])) / ls[i] for i in range(len(pcs))]
    for h in hs:
        lse_h = lses[h * nbr:(h + 1) * nbr]
        out_h = outs[h * nbr:(h + 1) * nbr]
        mx = functools.reduce(jnp.maximum, lse_h)
        es = [jnp.exp(x - mx) for x in lse_h]
        tot = functools.reduce(lambda a, b: a + b, es)
        o = functools.reduce(lambda a, b: a + b, [e * x for e, x in zip(es, out_h)]) / tot
        o_ref[0, :, sls[h]] = o[:t].astype(o_ref.dtype)


def _attention_sample(za, cache, layer):
    b, t, _ = za.shape
    depth, _, l = cache.shape[:3]
    rows = l * 2 * H_A
    cache2d = cache.reshape(depth, b, rows, HD_A)
    return pl.pallas_call(
        _attn_sample_kernel,
        grid=(b,),
        in_specs=[pl.BlockSpec((1, t, W_A), lambda bi: (bi, 0, 0)),
                  pl.BlockSpec((1, t, W_A), lambda bi: (bi, 0, 1)),
                  pl.BlockSpec((1, t, W_A), lambda bi: (bi, 0, 2)),
                  pl.BlockSpec((None, None, rows, HD_A), lambda bi: (layer, bi, 0, 0))],
        out_specs=pl.BlockSpec((1, t, W_A), lambda bi: (bi, 0, 0)),
        out_shape=jax.ShapeDtypeStruct((b, t, W_A), BF16),
        compiler_params=_cparams("parallel"),
        name="attention_sample",
    )(za, za, za, cache2d)


def _cast_t_kernel(x_ref, o_ref):
    o_ref[0] = x_ref[0].T.astype(o_ref.dtype)


def _cast_bf16_t(wt, row0, nrows, tn=512):
    depth, _, k = wt.shape
    assert row0 % tn == 0 and nrows % tn == 0
    tk = 1024 if k % 1024 == 0 else k
    r0 = row0 // tn
    return pl.pallas_call(
        _cast_t_kernel,
        grid=(depth, k // tk, nrows // tn),
        in_specs=[pl.BlockSpec((1, tn, tk), lambda l, i, j: (l, r0 + j, i))],
        out_specs=pl.BlockSpec((1, tk, tn), lambda l, i, j: (l, i, j)),
        out_shape=jax.ShapeDtypeStruct((depth, k, nrows), BF16),
        compiler_params=_cparams("parallel", "parallel", "parallel"),
        name="cast_bf16_t",
    )(wt)


def _cast_tail_t_kernel(a_ref, b_ref, d_ref, g_ref, *, shift):
    a = a_ref[0]
    tn = a.shape[0]
    x = jnp.concatenate([a, b_ref[0]], axis=0)
    d_ref[0] = x[shift:shift + tn].T.astype(d_ref.dtype)

    @pl.when(pl.program_id(2) == 0)
    def _():
        lane = lax.broadcasted_iota(jnp.int32, (a.shape[1], LANES), 1)
        g_ref[0] = jnp.where(lane < shift, a[0:LANES].T, 0.0).astype(g_ref.dtype)


def _cast_tail_t(wt, row0, shift, tn=512):
    depth, n, k = wt.shape
    nrows = n - row0 - shift
    packed_rows = 2 * SUBLANES
    assert row0 % tn == 0 and nrows % tn == 0 and shift % packed_rows == 0 and 0 < shift <= LANES
    tk = 1024 if k % 1024 == 0 else k
    r0 = row0 // tn
    per = tn // shift
    return pl.pallas_call(
        functools.partial(_cast_tail_t_kernel, shift=shift),
        grid=(depth, k // tk, nrows // tn),
        in_specs=[pl.BlockSpec((1, tn, tk), lambda l, i, j: (l, r0 + j, i)),
                  pl.BlockSpec((1, shift, tk), lambda l, i, j: (l, (r0 + j + 1) * per, i))],
        out_specs=[pl.BlockSpec((1, tk, tn), lambda l, i, j: (l, i, j)),
                   pl.BlockSpec((1, tk, LANES), lambda l, i, j: (l, i, 0))],
        out_shape=[jax.ShapeDtypeStruct((depth, k, nrows), BF16), jax.ShapeDtypeStruct((depth, k, LANES), BF16)],
        compiler_params=_cparams("parallel", "parallel", "arbitrary"),
        name="cast_tail_t",
    )(wt, wt)


def _kv_pack_kernel(*refs):
    o_ref = refs[-1]
    depth = (len(refs) - 1) // 2
    w = refs[0].shape[2]
    for l in range(depth):
        @pl.when(pl.program_id(0) == l)
        def _(l=l):
            o_ref[0, :, 0:w] = refs[2 * l][0]
            o_ref[0, :, w:2 * w] = refs[2 * l + 1][0]


def _kv_pack(zas, rows):
    depth = len(zas)
    b, s, _ = zas[0].shape
    ts = _pick(rows, (1024, 512, 256, 128, 64, 8))
    first = (s - rows) // ts
    assert (s - rows) % ts == 0
    in_specs, args = [], []
    for l in range(depth):
        for part in (1, 2):
            def imap(d, bi, si, l=l, part=part):
                return (bi, jnp.where(d == l, first + si, first), part)
            in_specs.append(pl.BlockSpec((1, ts, W_A), imap))
            args.append(zas[l])
    out = pl.pallas_call(
        _kv_pack_kernel,
        grid=(depth, b, rows // ts),
        in_specs=in_specs,
        out_specs=pl.BlockSpec((None, 1, ts, 2 * W_A), lambda d, bi, si: (d, bi, si, 0)),
        out_shape=jax.ShapeDtypeStruct((depth, b, rows, 2 * W_A), F32),
        compiler_params=_cparams("parallel", "parallel", "parallel"),
        name="kv_pack",
    )(*args)
    return out.reshape(depth, b, rows, 2, H_A, HD_A)


def _kv_shift_kernel(c_ref, n_ref, o_ref):
    l = c_ref.shape[0]
    t = n_ref.shape[0]
    o_ref[0:l - t] = c_ref[t:l]
    o_ref[l - t:l] = n_ref[...]


def _kv_shift(cache, kv_new):
    depth, b, l, two, h, hd = cache.shape
    t = kv_new.shape[2]
    cblk = (None, None, l, None, h, hd)
    cmap = lambda d, bi, s: (d, bi, 0, s, 0, 0)
    return pl.pallas_call(
        _kv_shift_kernel,
        grid=(depth, b, two),
        in_specs=[pl.BlockSpec(cblk, cmap), pl.BlockSpec((None, None, t, None, h, hd), cmap)],
        out_specs=pl.BlockSpec(cblk, cmap),
        out_shape=jax.ShapeDtypeStruct(cache.shape, cache.dtype),
        compiler_params=_cparams("parallel", "parallel", "parallel"),
        name="kv_shift",
    )(cache, kv_new)


def _chunk_plan(t, chunk):
    c = chunk if t >= chunk else max(2 * SUBLANES, t)
    assert t % c == 0 or t < c
    tc = min(t, c)
    return c, tc, max(t // c, 1)


def _run_chunks(nchunks, c, chunk):
    if nchunks == 1:
        chunk(0, 0)
    else:
        def body(ci, carry):
            chunk(ci, pl.multiple_of(ci * c, c))
            return carry
        lax.fori_loop(0, nchunks, body, 0)


def _gla_kernel(q_ref, f_ref, i_ref, g_ref, lb_ref, nrm_ref, s0_ref, o_ref, sfin_ref, st_ref, *, layer, c, tc, nchunks):
    nlev = int(math.log2(c))
    assert 1 << nlev == c
    depth = lb_ref.shape[0]
    nh = st_ref.shape[0]
    lbs = [lb_ref[i] for i in range(depth)]
    mx = functools.reduce(jnp.maximum, lbs)
    es = [jnp.exp(x - mx) for x in lbs]
    tot = functools.reduce(lambda a, b: a + b, es)
    lower = functools.reduce(lambda a, b: a + b, [es[i] / tot for i in range(layer + 1)]) - es[0] / tot
    one_minus_lb = 1.0 - lower

    row = lax.broadcasted_iota(jnp.int32, (c, c), 0)
    col = lax.broadcasted_iota(jnp.int32, (c, c), 1)
    prefix = [_onehot(col <= row)]
    lmask, second = [], []
    row_hd = lax.broadcasted_iota(jnp.int32, (c, HD_B), 0)
    for lev in range(1, nlev + 1):
        s = c >> lev
        grp = ~(2 * s - 1)
        prefix.append(_onehot(col <= (row & grp) + (s - 1)))
        lmask.append((((row ^ col) & grp) == 0) & ((row & s) != 0) & ((col & s) == 0))
        second.append((row_hd & s) != 0)
    mstack = jnp.concatenate(prefix, axis=0)
    eye = row == col
    valid = lax.broadcasted_iota(jnp.int32, (c, 1), 0) < tc
    nrm = nrm_ref[...]

    @pl.when(pl.program_id(2) == 0)
    def _():
        for h in range(nh):
            st_ref[h] = s0_ref[0, h].T

    def chunk(ci, t0):
        rows = pl.ds(t0, tc)
        hs = range(nh)
        sls = [slice(h * HD_B, (h + 1) * HD_B) for h in hs]
        qs = [_silu(_pad_rows(q_ref[0, rows, sl], c)) for sl in sls]
        kbs = [one_minus_lb[:, sl] * _sigmoid(-_pad_rows(f_ref[0, rows, sl], c)) for sl in sls]
        logfs = [jnp.log1p(-kb) for kb in kbs]
        if tc < c:
            kbs = [jnp.where(valid, kb, 0.0) for kb in kbs]
            logfs = [jnp.where(valid, lf, 0.0) for lf in logfs]
        vs = [_pad_rows(i_ref[0, rows, sl], c).astype(BF16) for sl in sls]
        bs_all = _sel_dot(mstack, jnp.concatenate(logfs, axis=1))
        bss = [bs_all[:, sl] for sl in sls]
        b = [bs[0:c] for bs in bss]
        atts = [jnp.where(eye, jnp.sum(q * kb, axis=-1, keepdims=True), 0.0) for q, kb in zip(qs, kbs)]
        for lev in range(1, nlev + 1):
            brs = [bs[lev * c:(lev + 1) * c] for bs in bss]
            ws = [(jnp.where(second[lev - 1], qs[h], kbs[h]) * jnp.exp(-jnp.abs(b[h] - brs[h]))).astype(BF16)
                  for h in hs]
            prods = [_dg(w, w, _NT) for w in ws]
            atts = [att + jnp.where(lmask[lev - 1], pr, 0.0) for att, pr in zip(atts, prods)]
        sts = [st_ref[h] for h in hs]
        o_inter = [_dot1(qs[h] * jnp.exp(b[h]), sts[h], _NT) for h in hs]
        o_intra = [_dg(atts[h].astype(BF16), vs[h]) for h in hs]
        bends = [b[h][c - 1:c] for h in hs]
        upd = [_dg(vs[h], (kbs[h] * jnp.exp(bends[h] - b[h])).astype(BF16), _TN) for h in hs]
        for h in hs:
            st_ref[h] = sts[h] * jnp.exp(bends[h]) + upd[h]
            o = o_inter[h] + o_intra[h]
            on = o * lax.rsqrt(jnp.mean(o * o, axis=-1, keepdims=True) + EPS) * nrm
            out = on * _silu(_pad_rows(g_ref[0, rows, sls[h]], c))
            o_ref[0, rows, sls[h]] = out[0:tc].astype(o_ref.dtype)

    _run_chunks(nchunks, c, chunk)

    @pl.when(pl.program_id(2) == pl.num_programs(2) - 1)
    def _():
        for h in range(nh):
            sfin_ref[0, h] = st_ref[h].T


def _hgrn(zb, lb, nrm, s0, layer, chunk):
    b, t, _ = zb.shape
    tb = min(t, MIXER_ROW_BLOCK)
    assert t % tb == 0
    c, tc, nchunks = _chunk_plan(tb, chunk)
    depth = lb.shape[0]
    nh = GLA_HEADS_PER_STEP
    ng = H_B // nh
    blk = (1, tb, nh * HD_B)
    sblk = (1, nh, HD_B, HD_B)
    return pl.pallas_call(
        functools.partial(_gla_kernel, layer=layer, c=c, tc=tc, nchunks=nchunks),
        grid=(b, ng, t // tb),
        in_specs=[pl.BlockSpec(blk, lambda bi, g, ti: (bi, ti, g)),
                  pl.BlockSpec(blk, lambda bi, g, ti: (bi, ti, ng + g)),
                  pl.BlockSpec(blk, lambda bi, g, ti: (bi, ti, 2 * ng + g)),
                  pl.BlockSpec(blk, lambda bi, g, ti: (bi, ti, 3 * ng + g)),
                  pl.BlockSpec((depth, 1, nh * HD_B), lambda bi, g, ti: (0, 0, g)),
                  pl.BlockSpec((1, HD_B), lambda bi, g, ti: (0, 0)),
                  pl.BlockSpec(sblk, lambda bi, g, ti: (bi, g, 0, 0))],
        out_specs=[pl.BlockSpec(blk, lambda bi, g, ti: (bi, ti, g)),
                   pl.BlockSpec(sblk, lambda bi, g, ti: (bi, g, 0, 0))],
        out_shape=[jax.ShapeDtypeStruct((b, t, W_B), BF16), jax.ShapeDtypeStruct((b, H_B, HD_B, HD_B), F32)],
        scratch_shapes=[pltpu.VMEM((nh, HD_B, HD_B), F32)],
        compiler_params=_cparams("parallel", "parallel", "arbitrary"),
        name="hgrn2",
    )(zb, zb, zb, zb, lb.reshape(depth, 1, W_B), nrm.reshape(1, HD_B), s0)


def _delta_kernel(q_ref, k_ref, v_ref, z_ref, zg_ref, wq_ref, wk_ref, wv_ref, bq_ref, bk_ref, bv_ref,
                  hq_ref, hk_ref, hv_ref, al_ref, dl_ref, nrm_ref, s0_ref, o_ref, sfin_ref, st_ref,
                  *, c, tc, nchunks):
    g = pl.program_id(1)
    first_block = pl.program_id(2) == 0
    nqk = DELTA_QK_HEADS_PER_STEP
    rep = H_C_V // H_C_QK
    sb = min(DELTA_SUB, c)
    row = lax.broadcasted_iota(jnp.int32, (c, c), 0)
    col = lax.broadcasted_iota(jnp.int32, (c, c), 1)
    lower_incl = _onehot(col <= row)
    eye = jnp.where(row == col, 1.0, 0.0)
    same_sub = ((row ^ col) & ~(sb - 1)) == 0
    valid = lax.broadcasted_iota(jnp.int32, (c, 1), 0) < tc
    lane = lax.broadcasted_iota(jnp.int32, (c, LANES), 1)
    nrm = nrm_ref[...]

    @pl.when(first_block)
    def _():
        for h in range(nqk * rep):
            st_ref[h] = s0_ref[0, h].T

    def conv(x_ref, w_ref, buf_ref, halo_ref, sl, ci, t0):
        x = x_ref[0, pl.ds(t0, tc), sl]
        halo = jnp.where(first_block, buf_ref[0, :, sl], halo_ref[0, :, sl])
        if nchunks > 1:
            prev = x_ref[0, pl.ds(pl.multiple_of(jnp.maximum(t0 - SUBLANES, 0), SUBLANES), SUBLANES), sl]
            halo = jnp.where(ci == 0, halo, prev)
        xh = jnp.concatenate([halo, x], axis=0)
        w = w_ref[:, sl]
        y = w[C_CONV - 1:C_CONV] * x
        for s in range(1, C_CONV):
            y = y + w[C_CONV - 1 - s:C_CONV - s] * pltpu.roll(xh, s, 0)[SUBLANES:]
        return _pad_rows(_silu(y), c)

    def l2n(x):
        return x * lax.rsqrt(jnp.sum(x * x, axis=-1, keepdims=True) + EPS)

    def chunk(ci, t0):
        rows = pl.ds(t0, tc)
        zg = _pad_rows(zg_ref[0, rows, :], c)
        beta_all = _sigmoid(zg)
        la_all = -jnp.exp(al_ref[...]) * _softplus(zg + dl_ref[...])
        nv = nqk * rep
        hs = range(nv)
        qsls = [slice(qh * HD_C, (qh + 1) * HD_C) for qh in range(nqk)]
        vsls = [slice(h * HD_C, (h + 1) * HD_C) for h in hs]
        qs = [l2n(conv(q_ref, wq_ref, bq_ref, hq_ref, sl, ci, t0)) * HD_C ** -0.5 for sl in qsls]
        ks = [l2n(conv(k_ref, wk_ref, bk_ref, hk_ref, sl, ci, t0)) for sl in qsls]
        if tc < c:
            ks = [jnp.where(valid, k, 0.0) for k in ks]
        qbs = [q.astype(BF16) for q in qs]
        kbs = [k.astype(BF16) for k in ks]
        kks = [_dg(kb, kb, _NT) for kb in kbs]
        qks = [_dg(qb, kb, _NT) for qb, kb in zip(qbs, kbs)]
        vs = [conv(v_ref, wv_ref, bv_ref, hv_ref, sl, ci, t0) for sl in vsls]
        betas = [jnp.sum(jnp.where(lane == g * nv + h, beta_all, 0.0), axis=-1, keepdims=True) for h in hs]
        las = [jnp.sum(jnp.where(lane == H_C_V + g * nv + h, la_all, 0.0), axis=-1, keepdims=True) for h in hs]
        if tc < c:
            betas = [jnp.where(valid, x, 0.0) for x in betas]
            las = [jnp.where(valid, x, 0.0) for x in las]
        bc_all = _sel_dot(lower_incl, jnp.concatenate([jnp.broadcast_to(la, (c, LANES)) for la in las], axis=1))
        bcols = [bc_all[:, h * LANES:h * LANES + c] for h in hs]
        decs = [jnp.exp(jnp.where(row >= col, bc - bc.T, NEG)) for bc in bcols]
        b1s = [bc[:, 0:1] for bc in bcols]
        ebs = [jnp.exp(b1) for b1 in b1s]
        bends = [b1[c - 1:c] for b1 in b1s]
        ns = [jnp.where(row > col, betas[h] * (kks[h // rep] * decs[h]), 0.0) for h in hs]
        nds = [jnp.where(same_sub, n, 0.0) for n in ns]
        xs = [eye - nd for nd in nds]
        ps = nds
        for _ in range(int(math.log2(sb)) - 1):
            ps = [_dot1(p, p) for p in ps]
            xs = [x + _dot1(x, p) for x, p in zip(xs, ps)]
        nblk = c // sb
        if nblk > 1:
            mms = [_dot1(x, n - nd) for x, n, nd in zip(xs, ns, nds)]
            ys = [eye - mm for mm in mms]
            ps = mms
            for _ in range(int(math.log2(nblk)) - 1):
                ps = [_dot1(p, p) for p in ps]
                ys = [y + _dot1(y, p) for y, p in zip(ys, ps)]
            xs = [_dot1(y, x) for y, x in zip(ys, xs)]
        sts = [st_ref[h] for h in hs]
        stbs = [st.astype(BF16) for st in sts]
        ksts = [_dg(kbs[h // rep], stbs[h], _NT) for h in hs]
        qsts = [_dg(qbs[h // rep], stbs[h], _NT) for h in hs]
        rhss = [betas[h] * (vs[h] - ebs[h] * ksts[h]) for h in hs]
        ubs = [_dot1(x, rhs).astype(BF16) for x, rhs in zip(xs, rhss)]
        o_intra = [_dg((qks[h // rep] * decs[h]).astype(BF16), ubs[h]) for h in hs]
        upd = [_dg(ubs[h], (ks[h // rep] * jnp.exp(bends[h] - b1s[h])).astype(BF16), _TN) for h in hs]
        for h in hs:
            st_ref[h] = jnp.exp(bends[h]) * sts[h] + upd[h]
            o = ebs[h] * qsts[h] + o_intra[h]
            on = o * lax.rsqrt(jnp.mean(o * o, axis=-1, keepdims=True) + EPS) * nrm
            out = on * _silu(_pad_rows(z_ref[0, rows, vsls[h]], c))
            o_ref[0, rows, vsls[h]] = out[0:tc].astype(o_ref.dtype)

    _run_chunks(nchunks, c, chunk)

    @pl.when(pl.program_id(2) == pl.num_programs(2) - 1)
    def _():
        for h in range(nqk * rep):
            sfin_ref[0, h] = st_ref[h].T


def _delta(zc, zg, conv_w, buf, a_log, dt_bias, nrm, s0, chunk):
    b, t, _ = zc.shape
    tb = min(t, MIXER_ROW_BLOCK)
    assert t % tb == 0 and tb % SUBLANES == 0
    c, tc, nchunks = _chunk_plan(tb, chunk)
    rep = H_C_V // H_C_QK
    nqk = DELTA_QK_HEADS_PER_STEP
    nv = nqk * rep
    ng = H_C_QK // nqk
    qw = nqk * HD_C
    vw = nv * HD_C
    assert W_CQK % qw == 0 and (2 * W_CQK) % vw == 0
    kq0 = W_CQK // qw
    v0 = 2 * W_CQK // vw
    z0 = v0 + W_CV // vw
    buf8 = jnp.pad(buf, ((0, 0), (SUBLANES - (C_CONV - 1), 0), (0, 0)))
    pad_l = jnp.zeros((H_C_V,), F32)
    a_lane = jnp.pad(jnp.concatenate([pad_l, a_log]), (0, LANES - 2 * H_C_V)).reshape(1, LANES)
    d_lane = jnp.pad(jnp.concatenate([pad_l, dt_bias]), (0, LANES - 2 * H_C_V)).reshape(1, LANES)
    hb = tb // SUBLANES

    def rows(col0):
        return lambda bi, g, ti: (bi, ti, col0 + g)

    def first(col0):
        return lambda bi, g, ti: (bi, 0, col0 + g)

    def halo(col0):
        return lambda bi, g, ti: (bi, jnp.maximum(ti * hb - 1, 0), col0 + g)

    full = lambda bi, g, ti: (0, 0)
    sblk = (1, nv, HD_C, HD_C)
    smap = lambda bi, g, ti: (bi, g, 0, 0)
    return pl.pallas_call(
        functools.partial(_delta_kernel, c=c, tc=tc, nchunks=nchunks),
        grid=(b, ng, t // tb),
        in_specs=[pl.BlockSpec((1, tb, qw), rows(0)), pl.BlockSpec((1, tb, qw), rows(kq0)),
                  pl.BlockSpec((1, tb, vw), rows(v0)), pl.BlockSpec((1, tb, vw), rows(z0)),
                  pl.BlockSpec((1, tb, W_CG), lambda bi, g, ti: (bi, ti, 0)),
                  pl.BlockSpec((C_CONV, qw), lambda bi, g, ti: (0, g)),
                  pl.BlockSpec((C_CONV, qw), lambda bi, g, ti: (0, kq0 + g)),
                  pl.BlockSpec((C_CONV, vw), lambda bi, g, ti: (0, v0 + g)),
                  pl.BlockSpec((1, SUBLANES, qw), first(0)), pl.BlockSpec((1, SUBLANES, qw), first(kq0)),
                  pl.BlockSpec((1, SUBLANES, vw), first(v0)),
                  pl.BlockSpec((1, SUBLANES, qw), halo(0)), pl.BlockSpec((1, SUBLANES, qw), halo(kq0)),
                  pl.BlockSpec((1, SUBLANES, vw), halo(v0)),
                  pl.BlockSpec((1, LANES), full), pl.BlockSpec((1, LANES), full),
                  pl.BlockSpec((1, HD_C), full),
                  pl.BlockSpec(sblk, smap)],
        out_specs=[pl.BlockSpec((1, tb, vw), rows(0)), pl.BlockSpec(sblk, smap)],
        out_shape=[jax.ShapeDtypeStruct((b, t, W_CV), BF16), jax.ShapeDtypeStruct((b, H_C_V, HD_C, HD_C), F32)],
        scratch_shapes=[pltpu.VMEM((nv, HD_C, HD_C), F32)],
        compiler_params=_cparams("parallel", "parallel", "arbitrary"),
        name="gated_deltanet",
    )(zc, zc, zc, zc, zg, conv_w, conv_w, conv_w, buf8, buf8, buf8, zc, zc, zc,
      a_lane, d_lane, nrm.reshape(1, HD_C), s0)


def _ret_kernel(q_ref, k_ref, v_ref, g_ref, nrm_ref, s0_ref, o_ref, sfin_ref, st_ref, *, c, tc, nchunks):
    nh = st_ref.shape[0]
    row = lax.broadcasted_iota(jnp.int32, (c, c), 0)
    col = lax.broadcasted_iota(jnp.int32, (c, c), 1)
    steps = (jnp.minimum(row + 1, tc) - jnp.minimum(col + 1, tc)).astype(F32)
    r1 = lax.broadcasted_iota(jnp.int32, (c, 1), 0)
    nsteps = jnp.minimum(r1 + 1, tc).astype(F32)
    valid = r1 < tc
    nrm = nrm_ref[...]
    lgs, decs = [], []
    for h in range(nh):
        hf = jnp.full((1, 1), pl.program_id(1) * nh + h, jnp.int32).astype(F32)
        lg = jnp.log1p(-jnp.exp2(-5.0 - hf))
        lgs.append(lg)
        decs.append(jnp.exp(jnp.where(row >= col, steps * lg, NEG)))
        st_ref[h] = s0_ref[0, h].T

    def chunk(ci, t0):
        rows = pl.ds(t0, tc)
        hs = range(nh)
        sls = [slice(h * HD_D, (h + 1) * HD_D) for h in hs]
        b1s = [nsteps * lgs[h] for h in hs]
        bends = [float(tc) * lgs[h] for h in hs]
        qs = [_pad_rows(q_ref[0, rows, sl], c).astype(BF16) for sl in sls]
        ks = [_pad_rows(k_ref[0, rows, sl], c) * HD_D ** -0.5 for sl in sls]
        if tc < c:
            ks = [jnp.where(valid, k, 0.0) for k in ks]
        vs = [_pad_rows(v_ref[0, rows, sl], c).astype(BF16) for sl in sls]
        sts = [st_ref[h] for h in hs]
        atts = [_dg(qs[h], ks[h].astype(BF16), _NT) * decs[h] for h in hs]
        o_inter = [_dg(qs[h], sts[h].astype(BF16), _NT) for h in hs]
        o_intra = [_dg(atts[h].astype(BF16), vs[h]) for h in hs]
        upd = [_dg(vs[h], (ks[h] * jnp.exp(bends[h] - b1s[h])).astype(BF16), _TN) for h in hs]
        for h in hs:
            sl = sls[h]
            st_ref[h] = jnp.exp(bends[h]) * sts[h] + upd[h]
            o = jnp.exp(b1s[h]) * o_inter[h] + o_intra[h]
            mu = jnp.mean(o, axis=-1, keepdims=True)
            oc = o - mu
            var = jnp.mean(oc * oc, axis=-1, keepdims=True)
            on = oc * lax.rsqrt(var + EPS) * nrm
            out = on * _silu(_pad_rows(g_ref[0, rows, sl], c))
            o_ref[0, rows, sl] = out[0:tc].astype(o_ref.dtype)

    _run_chunks(nchunks, c, chunk)
    for h in range(nh):
        sfin_ref[0, h] = st_ref[h].T


def _retention(zd, nrm, s0, chunk):
    b, t, _ = zd.shape
    c, tc, nchunks = _chunk_plan(t, chunk)
    nh = RET_HEADS_PER_STEP
    ng = H_D // nh
    blk = (1, t, nh * HD_D)
    sblk = (1, nh, HD_D, HD_D)
    return pl.pallas_call(
        functools.partial(_ret_kernel, c=c, tc=tc, nchunks=nchunks),
        grid=(b, ng),
        in_specs=[pl.BlockSpec(blk, lambda bi, g: (bi, 0, g)),
                  pl.BlockSpec(blk, lambda bi, g: (bi, 0, ng + g)),
                  pl.BlockSpec(blk, lambda bi, g: (bi, 0, 2 * ng + g)),
                  pl.BlockSpec(blk, lambda bi, g: (bi, 0, 3 * ng + g)),
                  pl.BlockSpec((1, HD_D), lambda bi, g: (0, 0)),
                  pl.BlockSpec(sblk, lambda bi, g: (bi, g, 0, 0))],
        out_specs=[pl.BlockSpec(blk, lambda bi, g: (bi, 0, g)),
                   pl.BlockSpec(sblk, lambda bi, g: (bi, g, 0, 0))],
        out_shape=[jax.ShapeDtypeStruct((b, t, W_D), BF16), jax.ShapeDtypeStruct((b, H_D, HD_D, HD_D), F32)],
        scratch_shapes=[pltpu.VMEM((nh, HD_D, HD_D), F32)],
        compiler_params=_cparams("parallel", "arbitrary"),
        name="retention",
    )(zd, zd, zd, zd, nrm.reshape(1, HD_D), s0)


def _mixers(z, cache, s_hgrn, s_delta, buf_delta, s_ret, p, layer):
    za, zb, zc, zg, zd = z
    bsz, t, _ = za.shape
    if cache is None:
        o_a = _attention_prompt(za)
        kv_new = za
    else:
        o_a = _attention_sample(za, cache, layer)
        kv_new = za[:, :, W_A:].reshape(bsz, t, 2, H_A, HD_A)
    o_b, s_hgrn_new = _hgrn(zb, p['hgrn_lb'], p['hgrn_norm'][layer], s_hgrn, layer, GLA_CHUNK)
    o_c, s_delta_new = _delta(zc, zg, p['delta_conv'][layer], buf_delta, p['delta_A_log'][layer],
                              p['delta_dt_bias'][layer], p['delta_norm'][layer], s_delta, DELTA_CHUNK)
    pre = jnp.concatenate([buf_delta, zc[:, :, :2 * W_CQK + W_CV]], axis=1)
    buf_delta_new = pre[:, pre.shape[1] - (C_CONV - 1):]
    o_d, s_ret_new = _retention(zd, p['ret_norm'][layer], s_ret, RET_CHUNK)
    parts = [o.reshape(bsz * t, -1) for o in (o_a, o_b, o_c, o_d)]
    return parts, (kv_new, s_hgrn_new, s_delta_new, buf_delta_new, s_ret_new)


def _layer(xs, pes, cache, states, p, layer):
    shapes = [x.shape for x in xs]
    d = shapes[0][2]
    x2 = [x.reshape(-1, d) for x in xs]
    hn = [_rmsnorm(x, p['attn_norm'][layer], BF16) for x in x2]
    z = [_matmul(hn[0], hn[1], p[name], layer) for name in ('w_in_a', 'w_in_b', 'w_in_c', 'w_in_g', 'w_in_d')]
    parts, mixed = [], []
    for g in range(2):
        bsz, t, _ = shapes[g]
        zg = [zz[g].reshape(bsz, t, -1) for zz in z]
        s_hgrn, s_delta, buf_delta, s_ret, _ = states[g]
        pg, mg = _mixers(zg, cache if g == 1 else None, s_hgrn, s_delta, buf_delta, s_ret, p, layer)
        parts.append(pg)
        mixed.append(mg)
    x2 = _matmul_residual(parts[0], parts[1], p['w_out'], layer, x2[0], x2[1],
                          (1024, 512, 256, 64), (512, 256, 128))

    hf = [_rmsnorm(x, p['ffn_norm'][layer], BF16) for x in x2]
    act_p, tail_p, act_s, tail_s, w_down_bf16 = _ffn_gate_up(
        hf[0], hf[1], p['w_gate'], p['w_up'], p['ffn_conv'], p['w_down'], layer,
        shapes[0][1], states[1][4], shapes[1][1])
    x2 = _matmul_residual([act_p], [act_s], w_down_bf16, 0, x2[0], x2[1], (512, 256, 64), (256, 128))

    hp = [_rmsnorm(x, p['ple_norm'][layer], BF16) for x in x2]
    pe2 = [pe.reshape(-1, pe.shape[-1]).astype(BF16) for pe in pes]
    x2 = _ple(hp[0], hp[1], p['ple_gate'], pe2[0], pe2[1], p['ple_proj'], layer, x2[0], x2[1])
    new_states = [mixed[0] + (tail_p,), mixed[1] + (tail_s,)]
    return [x.reshape(s) for x, s in zip(x2, shapes)], new_states


def kernel(x_prompt, x_sample, cache_attn_kv, state_hgrn, state_delta, state_delta_conv, state_ret,
           state_ffn_conv, p_prompt, p_sample, attn_norm, w_in, hgrn_lb, hgrn_norm, delta_conv,
           delta_A_log, delta_dt_bias, delta_norm, ret_norm, w_out, ffn_norm, w_gate, w_up, ffn_conv,
           w_down, ple_norm, ple_gate, ple_proj, final_norm):
    depth = w_in.shape[0]
    bp = x_prompt.shape[0]
    o_b = 3 * W_A
    o_c = o_b + 4 * W_B
    o_g = o_c + 2 * W_CQK + 2 * W_CV
    o_d = o_g + 2 * H_C_V
    w_in_t = jnp.swapaxes(w_in, 1, 2)
    w_in_d, w_in_g = _cast_tail_t(w_in_t, o_g, o_d - o_g)
    p = {'attn_norm': attn_norm,
         'w_in_a': _cast_bf16_t(w_in_t, 0, o_b), 'w_in_b': _cast_bf16_t(w_in_t, o_b, o_c - o_b),
         'w_in_c': _cast_bf16_t(w_in_t, o_c, o_g - o_c), 'w_in_g': w_in_g, 'w_in_d': w_in_d,
         'hgrn_lb': hgrn_lb, 'hgrn_norm': hgrn_norm, 'delta_conv': delta_conv, 'delta_A_log': delta_A_log,
         'delta_dt_bias': delta_dt_bias, 'delta_norm': delta_norm, 'ret_norm': ret_norm,
         'w_out': w_out, 'ffn_norm': ffn_norm, 'w_gate': w_gate, 'w_up': w_up, 'ffn_conv': ffn_conv,
         'w_down': w_down, 'ple_norm': ple_norm, 'ple_gate': ple_gate, 'ple_proj': ple_proj}
    xs = [x_prompt, x_sample]
    st_p, st_s = [], []
    for l in range(depth):
        zero = lambda *s: jnp.zeros((bp,) + s, F32)
        states = [(zero(H_B, HD_B, HD_B), zero(H_C_V, HD_C, HD_C), zero(C_CONV - 1, 2 * W_CQK + W_CV),
                   zero(H_D, HD_D, HD_D), None),
                  (state_hgrn[l], state_delta[l], state_delta_conv[l], state_ret[l], state_ffn_conv[l])]
        xs, (sp, ss) = _layer(xs, [p_prompt[l], p_sample[l]], cache_attn_kv, states, p, l)
        st_p.append(sp)
        st_s.append(ss)
    xp, xs = xs

    def stack(sts, i):
        return jnp.stack([s[i] for s in sts])

    def final(x):
        return _rmsnorm(x.reshape(-1, x.shape[-1]), final_norm, F32).reshape(x.shape)

    kv_sample = _kv_shift(cache_attn_kv, stack(st_s, 0))
    s_p = x_prompt.shape[1]
    kv_prompt = _kv_pack([s[0] for s in st_p], min(A_BRANCHES[-1][0], s_p))
    return (final(xp), final(xs),
            kv_prompt, kv_sample, stack(st_p, 1), stack(st_s, 1), stack(st_p, 2), stack(st_s, 2),
            stack(st_p, 3), stack(st_s, 3), stack(st_p, 4), stack(st_s, 4), stack(st_p, 5), stack(st_s, 5))
```

```python
import functools
import math

import jax
import jax.numpy as jnp
from jax import lax
from jax.experimental import pallas as pl
from jax.experimental.pallas import tpu as pltpu

F32 = jnp.float32
BF16 = jnp.bfloat16
EPS = 1e-6
NEG = -1e30

H_A = 8
HD_A = 128
A_BRANCHES = ((128, 1), (512, 4), (2048, 16))
A_BLOCK = 128
H_B = 8
HD_B = 128
H_C_QK = 4
H_C_V = 8
HD_C = 128
C_CONV = 4
H_D = 4
HD_D = 256
FFN_CONV = 3

W_A = H_A * HD_A
W_B = H_B * HD_B
W_CQK = H_C_QK * HD_C
W_CV = H_C_V * HD_C
W_D = H_D * HD_D
W_CG = 128

VMEM_LIMIT_BYTES = 52 * 1024 * 1024
SUBLANES = 8
LANES = 128

GLA_HEADS_PER_STEP = 8
DELTA_QK_HEADS_PER_STEP = 4
RET_HEADS_PER_STEP = 2
MIXER_ROW_BLOCK = 1024
GLA_CHUNK = 64
DELTA_CHUNK = 128
RET_CHUNK = 256
DELTA_SUB = 16

_NN = (((1,), (0,)), ((), ()))
_NT = (((1,), (1,)), ((), ()))
_TN = (((0,), (0,)), ((), ()))


def _cparams(*sem):
    return pltpu.CompilerParams(dimension_semantics=sem, vmem_limit_bytes=VMEM_LIMIT_BYTES)


def _dg(a, b, dn=_NN):
    return lax.dot_general(a, b, dn, preferred_element_type=F32)


def _dot1(a, b, dn=_NN):
    return _dg(a.astype(BF16), b.astype(BF16), dn)


def _split3(x):
    x1 = x.astype(BF16)
    r1 = x - x1.astype(F32)
    x2 = r1.astype(BF16)
    x3 = (r1 - x2.astype(F32)).astype(BF16)
    return x1, x2, x3


def _sel_dot(m, x):
    x1, x2, x3 = _split3(x)
    return _dg(m, x1) + (_dg(m, x2) + _dg(m, x3))


def _sigmoid(x):
    return 1.0 / (1.0 + jnp.exp(-x))


def _silu(x):
    return x * _sigmoid(x)


def _softplus(x):
    return jnp.maximum(x, 0.0) + jnp.log1p(jnp.exp(-jnp.abs(x)))


def _pad_rows(x, rows):
    if x.shape[0] == rows:
        return x
    return jnp.concatenate([x, jnp.zeros((rows - x.shape[0],) + x.shape[1:], x.dtype)], axis=0)


def _onehot(cond):
    return jnp.where(cond, 1.0, 0.0).astype(BF16)


def _pick(n, prefs):
    for p in prefs:
        if n % p == 0:
            return p
    return n


def _rmsnorm_kernel(x_ref, g_ref, o_ref):
    x = x_ref[...]
    y = x * lax.rsqrt(jnp.mean(x * x, axis=-1, keepdims=True) + EPS)
    o_ref[...] = (y * g_ref[...]).astype(o_ref.dtype)


def _rmsnorm(x, g, out_dtype):
    m, d = x.shape
    tm = _pick(m, (256, 64, 8))
    return pl.pallas_call(
        _rmsnorm_kernel,
        grid=(m // tm,),
        in_specs=[pl.BlockSpec((tm, d), lambda i: (i, 0)), pl.BlockSpec((1, d), lambda i: (0, 0))],
        out_specs=pl.BlockSpec((tm, d), lambda i: (i, 0)),
        out_shape=jax.ShapeDtypeStruct((m, d), out_dtype),
        compiler_params=_cparams("parallel"),
        name="rmsnorm",
    )(x, g.reshape(1, d))


def _rider_map(nj):
    return lambda i, j: (0, jnp.where(i == 0, j, nj - 1))


def _both_groups(a_ref, as_ref):
    return jnp.concatenate([a_ref[...], as_ref[...]], axis=0)


def _mm_kernel(a_ref, w_ref, as_ref, o_ref, os_ref):
    tm = a_ref.shape[0]
    first = pl.program_id(0) == 0

    @pl.when(first)
    def _():
        r = _dg(_both_groups(a_ref, as_ref), w_ref[...].astype(BF16))
        o_ref[...] = r[:tm]
        os_ref[...] = r[tm:]

    @pl.when(jnp.logical_not(first))
    def _():
        o_ref[...] = _dg(a_ref[...], w_ref[...].astype(BF16))


def _matmul(a, a_s, w, layer):
    m, k = a.shape
    ms = a_s.shape[0]
    n = w.shape[2]
    tm = _pick(m, (1024, 512, 256, 64))
    tn = _pick(n, (512, 256, 128))
    nj = n // tn
    return pl.pallas_call(
        _mm_kernel,
        grid=(m // tm, nj),
        in_specs=[pl.BlockSpec((tm, k), lambda i, j: (i, 0)),
                  pl.BlockSpec((None, k, tn), lambda i, j: (layer, 0, j)),
                  pl.BlockSpec((ms, k), lambda i, j: (0, 0))],
        out_specs=[pl.BlockSpec((tm, tn), lambda i, j: (i, j)), pl.BlockSpec((ms, tn), _rider_map(nj))],
        out_shape=[jax.ShapeDtypeStruct((m, n), F32), jax.ShapeDtypeStruct((ms, n), F32)],
        compiler_params=_cparams("arbitrary", "arbitrary"),
        name="matmul",
    )(a, w, a_s)


def _mm_res_kernel(*refs, nparts):
    a_refs = refs[:nparts]
    w_refs = refs[nparts:2 * nparts]
    x_ref = refs[2 * nparts]
    as_refs = refs[2 * nparts + 1:3 * nparts + 1]
    xs_ref, o_ref, os_ref = refs[3 * nparts + 1:]
    tm = x_ref.shape[0]
    first = pl.program_id(0) == 0

    @pl.when(first)
    def _():
        acc = jnp.concatenate([x_ref[...], xs_ref[...]], axis=0)
        for a_ref, as_ref, w_ref in zip(a_refs, as_refs, w_refs):
            acc = acc + _dg(_both_groups(a_ref, as_ref), w_ref[...].astype(BF16))
        o_ref[...] = acc[:tm]
        os_ref[...] = acc[tm:]

    @pl.when(jnp.logical_not(first))
    def _():
        acc = x_ref[...]
        for a_ref, w_ref in zip(a_refs, w_refs):
            acc = acc + _dg(a_ref[...], w_ref[...].astype(BF16))
        o_ref[...] = acc


def _matmul_residual(parts, parts_s, w, layer, x, x_s, tm_prefs, tn_prefs):
    m, n = x.shape
    ms = x_s.shape[0]
    kp = parts[0].shape[1]
    nparts = len(parts)
    tm = _pick(m, tm_prefs)
    tn = _pick(n, tn_prefs)
    nj = n // tn

    def wmap(p):
        return lambda i, j: (layer, p, j)

    in_specs = [pl.BlockSpec((tm, kp), lambda i, j: (i, 0)) for _ in parts]
    in_specs += [pl.BlockSpec((None, kp, tn), wmap(p)) for p in range(nparts)]
    in_specs += [pl.BlockSpec((tm, tn), lambda i, j: (i, j))]
    in_specs += [pl.BlockSpec((ms, kp), lambda i, j: (0, 0)) for _ in parts]
    in_specs += [pl.BlockSpec((ms, tn), _rider_map(nj))]
    return pl.pallas_call(
        functools.partial(_mm_res_kernel, nparts=nparts),
        grid=(m // tm, nj),
        in_specs=in_specs,
        out_specs=[pl.BlockSpec((tm, tn), lambda i, j: (i, j)), pl.BlockSpec((ms, tn), _rider_map(nj))],
        out_shape=[jax.ShapeDtypeStruct((m, n), F32), jax.ShapeDtypeStruct((ms, n), F32)],
        compiler_params=_cparams("arbitrary", "arbitrary"),
        name="matmul_residual",
    )(*parts, *([w] * nparts), x, *parts_s, x_s)


def _ple_kernel(a_ref, wg_ref, pe_ref, wp_ref, x_ref, as_ref, pes_ref, xs_ref, o_ref, os_ref):
    tm = x_ref.shape[0]
    first = pl.program_id(0) == 0

    @pl.when(first)
    def _():
        x = jnp.concatenate([x_ref[...], xs_ref[...]], axis=0)
        gate = _sigmoid(_dg(_both_groups(a_ref, as_ref), wg_ref[...].astype(BF16)))
        r = x + gate * _dg(_both_groups(pe_ref, pes_ref), wp_ref[...].astype(BF16))
        o_ref[...] = r[:tm]
        os_ref[...] = r[tm:]

    @pl.when(jnp.logical_not(first))
    def _():
        gate = _sigmoid(_dg(a_ref[...], wg_ref[...].astype(BF16)))
        o_ref[...] = x_ref[...] + gate * _dg(pe_ref[...], wp_ref[...].astype(BF16))


def _ple(hp, hp_s, wg, pe, pe_s, wp, layer, x, x_s):
    m, d = x.shape
    ms = x_s.shape[0]
    k = hp.shape[1]
    kp = pe.shape[1]
    tm = _pick(m, (1024, 512, 256, 64))
    tn = _pick(d, (512, 256, 128))
    nj = d // tn
    return pl.pallas_call(
        _ple_kernel,
        grid=(m // tm, nj),
        in_specs=[pl.BlockSpec((tm, k), lambda i, j: (i, 0)),
                  pl.BlockSpec((None, k, tn), lambda i, j: (layer, 0, j)),
                  pl.BlockSpec((tm, kp), lambda i, j: (i, 0)),
                  pl.BlockSpec((None, kp, tn), lambda i, j: (layer, 0, j)),
                  pl.BlockSpec((tm, tn), lambda i, j: (i, j)),
                  pl.BlockSpec((ms, k), lambda i, j: (0, 0)),
                  pl.BlockSpec((ms, kp), lambda i, j: (0, 0)),
                  pl.BlockSpec((ms, tn), _rider_map(nj))],
        out_specs=[pl.BlockSpec((tm, tn), lambda i, j: (i, j)), pl.BlockSpec((ms, tn), _rider_map(nj))],
        out_shape=[jax.ShapeDtypeStruct((m, d), F32), jax.ShapeDtypeStruct((ms, d), F32)],
        compiler_params=_cparams("arbitrary", "arbitrary"),
        name="ple",
    )(hp, wg, pe, wp, x, hp_s, pe_s, x_s)


def _ffn_act(g, p1, p2, cw, u):
    gc = cw[0:1] * p2 + cw[1:2] * p1 + cw[2:3] * g
    return (_silu(gc) * u).astype(BF16)


def _ffn_gu_kernel(h_ref, halo_ref, wg_ref, wu_ref, cw_ref, wd_ref, hs_ref, b1_ref, b2_ref,
                   o_ref, tail_ref, wdb_ref, os_ref, gs_ref, *, seq, t_s):
    tm = h_ref.shape[0]
    first = pl.program_id(0) == 0

    def prompt_rows(g, u, gh):
        seq_start = lax.rem(pl.program_id(0) * tm, seq) == 0
        gh = jnp.where(seq_start, 0.0, gh)
        row = lax.broadcasted_iota(jnp.int32, g.shape, 0)
        p1 = jnp.where(row == 0, gh[7:8], pltpu.roll(g, 1, 0))
        p2 = jnp.where(row == 0, gh[6:7], jnp.where(row == 1, gh[7:8], pltpu.roll(g, 2, 0)))
        o_ref[...] = _ffn_act(g, p1, p2, cw_ref[...], u)
        tail_ref[0] = g[tm - SUBLANES:tm]

    @pl.when(first)
    def _():
        wdb_ref[...] = wd_ref[...].astype(wdb_ref.dtype)
        wg = wg_ref[...].astype(BF16)
        wu = wu_ref[...].astype(BF16)
        cw = cw_ref[...]
        a = _both_groups(h_ref, hs_ref)
        g = _dg(a, wg)
        u = _dg(a, wu)
        prompt_rows(g[:tm], u[:tm], jnp.zeros((SUBLANES, g.shape[1]), F32))
        g_s = g[tm:]
        pos = lax.rem(lax.broadcasted_iota(jnp.int32, g_s.shape, 0), t_s)
        q1 = jnp.where(pos == 0, b1_ref[...], pltpu.roll(g_s, 1, 0))
        q2 = jnp.where(pos < 2, b2_ref[...], pltpu.roll(g_s, 2, 0))
        os_ref[...] = _ffn_act(g_s, q1, q2, cw, u[tm:])
        gs_ref[...] = g_s

    @pl.when(jnp.logical_not(first))
    def _():
        wdb_ref[...] = wd_ref[...].astype(wdb_ref.dtype)
        wg = wg_ref[...].astype(BF16)
        a = h_ref[...]
        prompt_rows(_dg(a, wg), _dg(a, wu_ref[...].astype(BF16)), _dg(halo_ref[...], wg))


def _ffn_gate_up(hf, hf_s, wg, wu, cw, wd, layer, seq, buf_s, t_s):
    m, d = hf.shape
    ms = hf_s.shape[0]
    f = wg.shape[2]
    tm = _pick(seq, (1024, 512, 256, 128, 64, 8))
    tf = _pick(f, (256, 128))
    hb = tm // SUBLANES
    nj = f // tf
    steps = (m // tm) * nj
    assert f % steps == 0 and (f // steps) % (2 * SUBLANES) == 0
    wd_rows = f // steps
    nb = ms // t_s
    zeros = jnp.zeros((nb, t_s - 1, f), F32)
    b1 = jnp.concatenate([buf_s[:, 1:2], zeros], axis=1).reshape(ms, f)
    b2 = jnp.concatenate([buf_s[:, 0:2], zeros[:, 1:]], axis=1).reshape(ms, f)
    wmap = lambda i, j: (layer, 0, j)
    rider = _rider_map(nj)
    out, tail, wd_bf16, out_s, g_s = pl.pallas_call(
        functools.partial(_ffn_gu_kernel, seq=seq, t_s=t_s),
        grid=(m // tm, nj),
        in_specs=[pl.BlockSpec((tm, d), lambda i, j: (i, 0)),
                  pl.BlockSpec((SUBLANES, d), lambda i, j: (jnp.maximum(i * hb - 1, 0), 0)),
                  pl.BlockSpec((None, d, tf), wmap),
                  pl.BlockSpec((None, d, tf), wmap),
                  pl.BlockSpec((None, FFN_CONV, tf), wmap),
                  pl.BlockSpec((None, wd_rows, d), lambda i, j: (layer, i * nj + j, 0)),
                  pl.BlockSpec((ms, d), lambda i, j: (0, 0)),
                  pl.BlockSpec((ms, tf), rider),
                  pl.BlockSpec((ms, tf), rider)],
        out_specs=[pl.BlockSpec((tm, tf), lambda i, j: (i, j)),
                   pl.BlockSpec((1, SUBLANES, tf), lambda i, j: (i, 0, j)),
                   pl.BlockSpec((None, wd_rows, d), lambda i, j: (0, i * nj + j, 0)),
                   pl.BlockSpec((ms, tf), rider),
                   pl.BlockSpec((ms, tf), rider)],
        out_shape=[jax.ShapeDtypeStruct((m, f), BF16),
                   jax.ShapeDtypeStruct((m // tm, SUBLANES, f), F32),
                   jax.ShapeDtypeStruct((1, f, d), BF16),
                   jax.ShapeDtypeStruct((ms, f), BF16),
                   jax.ShapeDtypeStruct((ms, f), F32)],
        compiler_params=_cparams("arbitrary", "arbitrary"),
        name="ffn_gate_up",
    )(hf, hf, wg, wu, cw, wd, hf_s, b1, b2)
    per_seq = seq // tm
    tail = tail.reshape(m // seq, per_seq, SUBLANES, f)[:, per_seq - 1, SUBLANES - (FFN_CONV - 1):]
    tail_s = g_s.reshape(nb, t_s, f)[:, t_s - (FFN_CONV - 1):]
    return out, tail, out_s, tail_s, wd_bf16


ATTN_HEADS_PER_STEP = 4


def _band_softmax_many(probs, dist_cf, dist_pf, valid_c):
    scs = [_dot1(pr[0], pr[1], _NT) for pr in probs]
    sps = [None if pr[2] is None else _dot1(pr[0], pr[2], _NT) for pr in probs]
    pcs, pps, ls, lses = [], [], [], []
    for pr, sc, sp in zip(probs, scs, sps):
        slope, prev_ok = pr[5], pr[6]
        sc = jnp.where(valid_c, sc - slope * dist_cf, NEG)
        if sp is None:
            m = jnp.max(sc, axis=-1, keepdims=True)
            pc = jnp.exp(sc - m)
            pp = None
            l = jnp.sum(pc, axis=-1, keepdims=True)
        else:
            sp = jnp.where(prev_ok, sp - slope * dist_pf, NEG)
            m = jnp.max(jnp.maximum(sc, sp), axis=-1, keepdims=True)
            pc = jnp.exp(sc - m)
            pp = jnp.exp(sp - m)
            l = jnp.sum(pc + pp, axis=-1, keepdims=True)
        pcs.append(pc)
        pps.append(pp)
        ls.append(l)
        lses.append(m + jnp.log(l))
    ocs = [_dot1(pc, pr[3]) for pc, pr in zip(pcs, probs)]
    ops = [None if pp is None else _dot1(pp, pr[4]) for pp, pr in zip(pps, probs)]
    outs = [(oc if op is None else oc + op) / l for oc, op, l in zip(ocs, ops, ls)]
    return outs, lses


def _band_kernel(*refs, s):
    nh = ATTN_HEADS_PER_STEP
    q_refs, k_refs, v_refs = refs[0:nh], refs[nh:2 * nh], refs[2 * nh:3 * nh]
    o_ref, o2_ref, o3_ref, l2_ref, l3_ref = refs[3 * nh:]
    nq = A_BLOCK
    hg = pl.program_id(1)
    qi = lax.broadcasted_iota(jnp.int32, (nq, nq), 0)
    kj = lax.broadcasted_iota(jnp.int32, (nq, nq), 1)
    dist_c = qi - kj
    dist_p = dist_c + nq
    valid_c = dist_c >= 0
    dist_cf = dist_c.astype(F32)
    dist_pf = dist_p.astype(F32)
    (w1, d1), (w2, d2), (w3, d3) = A_BRANCHES
    assert d1 == 1 and s % (d2 * nq) == 0 and s == d3 * nq and max(w1 // d1, w2 // d2, w3 // d3) <= nq

    def problem(h, rows_c, rows_p, has_prev, band, dil):
        hv = jnp.full((1, 1), hg * nh + h + 1, jnp.int32).astype(F32)
        slope = float(dil) * jnp.exp2(-8.0 * hv / H_A)
        q = q_refs[h][0, rows_c, :] * HD_A ** -0.5
        if rows_p is None:
            return (q, k_refs[h][0, rows_c, :], None, v_refs[h][0, rows_c, :], None, slope, None)
        return (q, k_refs[h][0, rows_c, :], k_refs[h][0, rows_p, :], v_refs[h][0, rows_c, :],
                v_refs[h][0, rows_p, :], slope, (dist_p <= band) & has_prev)

    def dilated(i, carry):
        rows3 = pl.ds(i, nq, stride=d3)
        r = lax.rem(i, d2)
        b = i // d2
        rows2 = pl.ds(b * (d2 * nq) + r, nq, stride=d2)
        rows2p = pl.ds(jnp.maximum(b - 1, 0) * (d2 * nq) + r, nq, stride=d2)
        probs = [problem(h, rows3, None, False, w3 // d3, d3) for h in range(nh)]
        probs += [problem(h, rows2, rows2p, b > 0, w2 // d2, d2) for h in range(nh)]
        outs, lses = _band_softmax_many(probs, dist_cf, dist_pf, valid_c)
        for h in range(nh):
            o3_ref[h, rows3, :] = outs[h]
            l3_ref[h, rows3, :] = jnp.broadcast_to(lses[h], (nq, HD_A))
            o2_ref[h, rows2, :] = outs[nh + h]
            l2_ref[h, rows2, :] = jnp.broadcast_to(lses[nh + h], (nq, HD_A))
        return carry

    lax.fori_loop(0, s // nq, dilated, 0)

    def dense(i, carry):
        blocks = []
        for j in range(2):
            bi = 2 * i + j
            rows = pl.ds(pl.multiple_of(bi * nq, nq), nq)
            rows_p = pl.ds(pl.multiple_of(jnp.maximum(bi - 1, 0) * nq, nq), nq)
            blocks += [(h, rows, problem(h, rows, rows_p, bi > 0, w1 // d1, d1)) for h in range(nh)]
        outs, lses = _band_softmax_many([blk[2] for blk in blocks], dist_cf, dist_pf, valid_c)
        for (h, rows, _), o1, l1 in zip(blocks, outs, lses):
            l2 = l2_ref[h, rows, :]
            l3 = l3_ref[h, rows, :]
            mx = jnp.maximum(l1, jnp.maximum(l2, l3))
            e1 = jnp.exp(l1 - mx)
            e2 = jnp.exp(l2 - mx)
            e3 = jnp.exp(l3 - mx)
            o = (e1 * o1 + e2 * o2_ref[h, rows, :] + e3 * o3_ref[h, rows, :]) / (e1 + e2 + e3)
            o_ref[0, rows, h * HD_A:(h + 1) * HD_A] = o.astype(o_ref.dtype)
        return carry

    assert (s // nq) % 2 == 0
    lax.fori_loop(0, s // (2 * nq), dense, 0)


def _attention_prompt(za):
    b, s, _ = za.shape
    nh = ATTN_HEADS_PER_STEP
    ng = H_A // nh
    blk = (1, s, HD_A)

    def head_spec(first, h):
        return pl.BlockSpec(blk, lambda bi, g: (bi, 0, first + g * nh + h))

    in_specs = [head_spec(part * H_A, h) for part in range(3) for h in range(nh)]
    return pl.pallas_call(
        functools.partial(_band_kernel, s=s),
        grid=(b, ng),
        in_specs=in_specs,
        out_specs=pl.BlockSpec((1, s, nh * HD_A), lambda bi, g: (bi, 0, g)),
        out_shape=jax.ShapeDtypeStruct((b, s, W_A), BF16),
        scratch_shapes=[pltpu.VMEM((nh, s, HD_A), F32) for _ in range(4)],
        compiler_params=_cparams("parallel", "arbitrary"),
        name="band_attention",
    )(*([za] * (3 * nh)))


def _attn_sample_kernel(q_ref, kn_ref, vn_ref, c_ref, o_ref):
    t = q_ref.shape[1]
    rows_per_pos = 2 * H_A
    n_past = c_ref.shape[0] // rows_per_pos
    tp = 2 * SUBLANES
    shape_c = (tp, n_past)
    shape_n = (tp, tp)
    dist_c = n_past + lax.broadcasted_iota(jnp.int32, shape_c, 0) - lax.broadcasted_iota(jnp.int32, shape_c, 1)
    dist_n = lax.broadcasted_iota(jnp.int32, shape_n, 0) - lax.broadcasted_iota(jnp.int32, shape_n, 1)
    new_ok = lax.broadcasted_iota(jnp.int32, shape_n, 1) < t
    dist_cf = dist_c.astype(F32)
    dist_nf = dist_n.astype(F32)
    oks = []
    for window, dil in A_BRANCHES:
        oks.append(((dist_c <= window) & (lax.rem(dist_c, dil) == 0),
                    (dist_n >= 0) & (dist_n <= window) & (lax.rem(dist_n, dil) == 0) & new_ok))
    hs = range(H_A)
    slopes = [2.0 ** (-8.0 * (h + 1) / H_A) for h in hs]
    sls = [slice(h * HD_A, (h + 1) * HD_A) for h in hs]
    qs = [_pad_rows(q_ref[0, :, sl] * HD_A ** -0.5, tp).astype(BF16) for sl in sls]
    kns = [_pad_rows(kn_ref[0, :, sl], tp).astype(BF16) for sl in sls]
    vns = [_pad_rows(vn_ref[0, :, sl], tp).astype(BF16) for sl in sls]
    kcs = [c_ref[pl.ds(h, n_past, stride=rows_per_pos), :].astype(BF16) for h in hs]
    vcs = [c_ref[pl.ds(H_A + h, n_past, stride=rows_per_pos), :].astype(BF16) for h in hs]
    b_cs = [_dg(qs[h], kcs[h], _NT) - slopes[h] * dist_cf for h in hs]
    b_ns = [_dg(qs[h], kns[h], _NT) - slopes[h] * dist_nf for h in hs]
    pcs, pns, ls, lses = [], [], [], []
    for h in hs:
        for ok_c, ok_n in oks:
            sc = jnp.where(ok_c, b_cs[h], NEG)
            sn = jnp.where(ok_n, b_ns[h], NEG)
            m = jnp.maximum(jnp.max(sc, axis=-1, keepdims=True), jnp.max(sn, axis=-1, keepdims=True))
            pc = jnp.exp(sc - m)
            pn = jnp.exp(sn - m)
            pcs.append(pc.astype(BF16))
            pns.append(pn.astype(BF16))
            ls.append(jnp.sum(pc, axis=-1, keepdims=True) + jnp.sum(pn, axis=-1, keepdims=True))
            lses.append(m + jnp.log(ls[-1]))
    nbr = len(oks)
    outs = [(_dg(pcs[i], vcs[i // nbr]) + _dg(pns[i], vns[i // nbr])) / ls[i] for i in range(len(pcs))]
    for h in hs:
        lse_h = lses[h * nbr:(h + 1) * nbr]
        out_h = outs[h * nbr:(h + 1) * nbr]
        mx = functools.reduce(jnp.maximum, lse_h)
        es = [jnp.exp(x - mx) for x in lse_h]
        tot = functools.reduce(lambda a, b: a + b, es)
        o = functools.reduce(lambda a, b: a + b, [e * x for e, x in zip(es, out_h)]) / tot
        o_ref[0, :, sls[h]] = o[:t].astype(o_ref.dtype)


def _attention_sample(za, cache, layer):
    b, t, _ = za.shape
    depth, _, l = cache.shape[:3]
    rows = l * 2 * H_A
    cache2d = cache.reshape(depth, b, rows, HD_A)
    return pl.pallas_call(
        _attn_sample_kernel,
        grid=(b,),
        in_specs=[pl.BlockSpec((1, t, W_A), lambda bi: (bi, 0, 0)),
                  pl.BlockSpec((1, t, W_A), lambda bi: (bi, 0, 1)),
                  pl.BlockSpec((1, t, W_A), lambda bi: (bi, 0, 2)),
                  pl.BlockSpec((None, None, rows, HD_A), lambda bi: (layer, bi, 0, 0))],
        out_specs=pl.BlockSpec((1, t, W_A), lambda bi: (bi, 0, 0)),
        out_shape=jax.ShapeDtypeStruct((b, t, W_A), BF16),
        compiler_params=_cparams("parallel"),
        name="attention_sample",
    )(za, za, za, cache2d)


def _cast_t_kernel(x_ref, o_ref):
    o_ref[0] = x_ref[0].T.astype(o_ref.dtype)


def _cast_bf16_t(wt, row0, nrows, tn=512):
    depth, _, k = wt.shape
    assert row0 % tn == 0 and nrows % tn == 0
    tk = 1024 if k % 1024 == 0 else k
    r0 = row0 // tn
    return pl.pallas_call(
        _cast_t_kernel,
        grid=(depth, k // tk, nrows // tn),
        in_specs=[pl.BlockSpec((1, tn, tk), lambda l, i, j: (l, r0 + j, i))],
        out_specs=pl.BlockSpec((1, tk, tn), lambda l, i, j: (l, i, j)),
        out_shape=jax.ShapeDtypeStruct((depth, k, nrows), BF16),
        compiler_params=_cparams("parallel", "parallel", "parallel"),
        name="cast_bf16_t",
    )(wt)


def _cast_tail_t_kernel(a_ref, b_ref, d_ref, g_ref, *, shift):
    a = a_ref[0]
    tn = a.shape[0]
    x = jnp.concatenate([a, b_ref[0]], axis=0)
    d_ref[0] = x[shift:shift + tn].T.astype(d_ref.dtype)

    @pl.when(pl.program_id(2) == 0)
    def _():
        lane = lax.broadcasted_iota(jnp.int32, (a.shape[1], LANES), 1)
        g_ref[0] = jnp.where(lane < shift, a[0:LANES].T, 0.0).astype(g_ref.dtype)


def _cast_tail_t(wt, row0, shift, tn=512):
    depth, n, k = wt.shape
    nrows = n - row0 - shift
    packed_rows = 2 * SUBLANES
    assert row0 % tn == 0 and nrows % tn == 0 and shift % packed_rows == 0 and 0 < shift <= LANES
    tk = 1024 if k % 1024 == 0 else k
    r0 = row0 // tn
    per = tn // shift
    return pl.pallas_call(
        functools.partial(_cast_tail_t_kernel, shift=shift),
        grid=(depth, k // tk, nrows // tn),
        in_specs=[pl.BlockSpec((1, tn, tk), lambda l, i, j: (l, r0 + j, i)),
                  pl.BlockSpec((1, shift, tk), lambda l, i, j: (l, (r0 + j + 1) * per, i))],
        out_specs=[pl.BlockSpec((1, tk, tn), lambda l, i, j: (l, i, j)),
                   pl.BlockSpec((1, tk, LANES), lambda l, i, j: (l, i, 0))],
        out_shape=[jax.ShapeDtypeStruct((depth, k, nrows), BF16), jax.ShapeDtypeStruct((depth, k, LANES), BF16)],
        compiler_params=_cparams("parallel", "parallel", "arbitrary"),
        name="cast_tail_t",
    )(wt, wt)


def _kv_pack_kernel(*refs):
    o_ref = refs[-1]
    depth = (len(refs) - 1) // 2
    w = refs[0].shape[2]
    for l in range(depth):
        @pl.when(pl.program_id(0) == l)
        def _(l=l):
            o_ref[0, :, 0:w] = refs[2 * l][0]
            o_ref[0, :, w:2 * w] = refs[2 * l + 1][0]


def _kv_pack(zas, rows):
    depth = len(zas)
    b, s, _ = zas[0].shape
    ts = _pick(rows, (1024, 512, 256, 128, 64, 8))
    first = (s - rows) // ts
    assert (s - rows) % ts == 0
    in_specs, args = [], []
    for l in range(depth):
        for part in (1, 2):
            def imap(d, bi, si, l=l, part=part):
                return (bi, jnp.where(d == l, first + si, first), part)
            in_specs.append(pl.BlockSpec((1, ts, W_A), imap))
            args.append(zas[l])
    out = pl.pallas_call(
        _kv_pack_kernel,
        grid=(depth, b, rows // ts),
        in_specs=in_specs,
        out_specs=pl.BlockSpec((None, 1, ts, 2 * W_A), lambda d, bi, si: (d, bi, si, 0)),
        out_shape=jax.ShapeDtypeStruct((depth, b, rows, 2 * W_A), F32),
        compiler_params=_cparams("parallel", "parallel", "parallel"),
        name="kv_pack",
    )(*args)
    return out.reshape(depth, b, rows, 2, H_A, HD_A)


def _kv_shift_kernel(c_ref, n_ref, o_ref):
    l = c_ref.shape[0]
    t = n_ref.shape[0]
    o_ref[0:l - t] = c_ref[t:l]
    o_ref[l - t:l] = n_ref[...]


def _kv_shift(cache, kv_new):
    depth, b, l, two, h, hd = cache.shape
    t = kv_new.shape[2]
    cblk = (None, None, l, None, h, hd)
    cmap = lambda d, bi, s: (d, bi, 0, s, 0, 0)
    return pl.pallas_call(
        _kv_shift_kernel,
        grid=(depth, b, two),
        in_specs=[pl.BlockSpec(cblk, cmap), pl.BlockSpec((None, None, t, None, h, hd), cmap)],
        out_specs=pl.BlockSpec(cblk, cmap),
        out_shape=jax.ShapeDtypeStruct(cache.shape, cache.dtype),
        compiler_params=_cparams("parallel", "parallel", "parallel"),
        name="kv_shift",
    )(cache, kv_new)


def _chunk_plan(t, chunk):
    c = chunk if t >= chunk else max(2 * SUBLANES, t)
    assert t % c == 0 or t < c
    tc = min(t, c)
    return c, tc, max(t // c, 1)


def _run_chunks(nchunks, c, chunk):
    if nchunks == 1:
        chunk(0, 0)
    else:
        def body(ci, carry):
            chunk(ci, pl.multiple_of(ci * c, c))
            return carry
        lax.fori_loop(0, nchunks, body, 0)


def _gla_kernel(q_ref, f_ref, i_ref, g_ref, lb_ref, nrm_ref, s0_ref, o_ref, sfin_ref, st_ref, *, layer, c, tc, nchunks):
    nlev = int(math.log2(c))
    assert 1 << nlev == c
    depth = lb_ref.shape[0]
    nh = st_ref.shape[0]
    lbs = [lb_ref[i] for i in range(depth)]
    mx = functools.reduce(jnp.maximum, lbs)
    es = [jnp.exp(x - mx) for x in lbs]
    tot = functools.reduce(lambda a, b: a + b, es)
    lower = functools.reduce(lambda a, b: a + b, [es[i] / tot for i in range(layer + 1)]) - es[0] / tot
    one_minus_lb = 1.0 - lower

    row = lax.broadcasted_iota(jnp.int32, (c, c), 0)
    col = lax.broadcasted_iota(jnp.int32, (c, c), 1)
    prefix = [_onehot(col <= row)]
    lmask, second = [], []
    row_hd = lax.broadcasted_iota(jnp.int32, (c, HD_B), 0)
    for lev in range(1, nlev + 1):
        s = c >> lev
        grp = ~(2 * s - 1)
        prefix.append(_onehot(col <= (row & grp) + (s - 1)))
        lmask.append((((row ^ col) & grp) == 0) & ((row & s) != 0) & ((col & s) == 0))
        second.append((row_hd & s) != 0)
    mstack = jnp.concatenate(prefix, axis=0)
    eye = row == col
    valid = lax.broadcasted_iota(jnp.int32, (c, 1), 0) < tc
    nrm = nrm_ref[...]

    @pl.when(pl.program_id(2) == 0)
    def _():
        for h in range(nh):
            st_ref[h] = s0_ref[0, h].T

    def chunk(ci, t0):
        rows = pl.ds(t0, tc)
        hs = range(nh)
        sls = [slice(h * HD_B, (h + 1) * HD_B) for h in hs]
        qs = [_silu(_pad_rows(q_ref[0, rows, sl], c)) for sl in sls]
        kbs = [one_minus_lb[:, sl] * _sigmoid(-_pad_rows(f_ref[0, rows, sl], c)) for sl in sls]
        logfs = [jnp.log1p(-kb) for kb in kbs]
        if tc < c:
            kbs = [jnp.where(valid, kb, 0.0) for kb in kbs]
            logfs = [jnp.where(valid, lf, 0.0) for lf in logfs]
        vs = [_pad_rows(i_ref[0, rows, sl], c).astype(BF16) for sl in sls]
        bs_all = _sel_dot(mstack, jnp.concatenate(logfs, axis=1))
        bss = [bs_all[:, sl] for sl in sls]
        b = [bs[0:c] for bs in bss]
        atts = [jnp.where(eye, jnp.sum(q * kb, axis=-1, keepdims=True), 0.0) for q, kb in zip(qs, kbs)]
        for lev in range(1, nlev + 1):
            brs = [bs[lev * c:(lev + 1) * c] for bs in bss]
            ws = [(jnp.where(second[lev - 1], qs[h], kbs[h]) * jnp.exp(-jnp.abs(b[h] - brs[h]))).astype(BF16)
                  for h in hs]
            prods = [_dg(w, w, _NT) for w in ws]
            atts = [att + jnp.where(lmask[lev - 1], pr, 0.0) for att, pr in zip(atts, prods)]
        sts = [st_ref[h] for h in hs]
        o_inter = [_dot1(qs[h] * jnp.exp(b[h]), sts[h], _NT) for h in hs]
        o_intra = [_dg(atts[h].astype(BF16), vs[h]) for h in hs]
        bends = [b[h][c - 1:c] for h in hs]
        upd = [_dg(vs[h], (kbs[h] * jnp.exp(bends[h] - b[h])).astype(BF16), _TN) for h in hs]
        for h in hs:
            st_ref[h] = sts[h] * jnp.exp(bends[h]) + upd[h]
            o = o_inter[h] + o_intra[h]
            on = o * lax.rsqrt(jnp.mean(o * o, axis=-1, keepdims=True) + EPS) * nrm
            out = on * _silu(_pad_rows(g_ref[0, rows, sls[h]], c))
            o_ref[0, rows, sls[h]] = out[0:tc].astype(o_ref.dtype)

    _run_chunks(nchunks, c, chunk)

    @pl.when(pl.program_id(2) == pl.num_programs(2) - 1)
    def _():
        for h in range(nh):
            sfin_ref[0, h] = st_ref[h].T


def _hgrn(zb, col0, lb, nrm, s0, layer, chunk):
    b, t, _ = zb.shape
    assert col0 % (GLA_HEADS_PER_STEP * HD_B) == 0
    c0 = col0 // (GLA_HEADS_PER_STEP * HD_B)
    tb = min(t, MIXER_ROW_BLOCK)
    assert t % tb == 0
    c, tc, nchunks = _chunk_plan(tb, chunk)
    depth = lb.shape[0]
    nh = GLA_HEADS_PER_STEP
    ng = H_B // nh
    blk = (1, tb, nh * HD_B)
    sblk = (1, nh, HD_B, HD_B)
    return pl.pallas_call(
        functools.partial(_gla_kernel, layer=layer, c=c, tc=tc, nchunks=nchunks),
        grid=(b, ng, t // tb),
        in_specs=[pl.BlockSpec(blk, lambda bi, g, ti: (bi, ti, c0 + g)),
                  pl.BlockSpec(blk, lambda bi, g, ti: (bi, ti, c0 + ng + g)),
                  pl.BlockSpec(blk, lambda bi, g, ti: (bi, ti, c0 + 2 * ng + g)),
                  pl.BlockSpec(blk, lambda bi, g, ti: (bi, ti, c0 + 3 * ng + g)),
                  pl.BlockSpec((depth, 1, nh * HD_B), lambda bi, g, ti: (0, 0, g)),
                  pl.BlockSpec((1, HD_B), lambda bi, g, ti: (0, 0)),
                  pl.BlockSpec(sblk, lambda bi, g, ti: (bi, g, 0, 0))],
        out_specs=[pl.BlockSpec(blk, lambda bi, g, ti: (bi, ti, g)),
                   pl.BlockSpec(sblk, lambda bi, g, ti: (bi, g, 0, 0))],
        out_shape=[jax.ShapeDtypeStruct((b, t, W_B), BF16), jax.ShapeDtypeStruct((b, H_B, HD_B, HD_B), F32)],
        scratch_shapes=[pltpu.VMEM((nh, HD_B, HD_B), F32)],
        compiler_params=_cparams("parallel", "parallel", "arbitrary"),
        name="hgrn2",
    )(zb, zb, zb, zb, lb.reshape(depth, 1, W_B), nrm.reshape(1, HD_B), s0)


def _delta_kernel(q_ref, k_ref, v_ref, z_ref, zg_ref, wq_ref, wk_ref, wv_ref, bq_ref, bk_ref, bv_ref,
                  hq_ref, hk_ref, hv_ref, al_ref, dl_ref, nrm_ref, s0_ref, o_ref, sfin_ref, st_ref,
                  *, c, tc, nchunks):
    g = pl.program_id(1)
    first_block = pl.program_id(2) == 0
    nqk = DELTA_QK_HEADS_PER_STEP
    rep = H_C_V // H_C_QK
    sb = min(DELTA_SUB, c)
    row = lax.broadcasted_iota(jnp.int32, (c, c), 0)
    col = lax.broadcasted_iota(jnp.int32, (c, c), 1)
    lower_incl = _onehot(col <= row)
    eye = jnp.where(row == col, 1.0, 0.0)
    same_sub = ((row ^ col) & ~(sb - 1)) == 0
    valid = lax.broadcasted_iota(jnp.int32, (c, 1), 0) < tc
    lane = lax.broadcasted_iota(jnp.int32, (c, LANES), 1)
    nrm = nrm_ref[...]

    @pl.when(first_block)
    def _():
        for h in range(nqk * rep):
            st_ref[h] = s0_ref[0, h].T

    def conv(x_ref, w_ref, buf_ref, halo_ref, sl, ci, t0):
        x = x_ref[0, pl.ds(t0, tc), sl]
        halo = jnp.where(first_block, buf_ref[0, :, sl], halo_ref[0, :, sl])
        if nchunks > 1:
            prev = x_ref[0, pl.ds(pl.multiple_of(jnp.maximum(t0 - SUBLANES, 0), SUBLANES), SUBLANES), sl]
            halo = jnp.where(ci == 0, halo, prev)
        xh = jnp.concatenate([halo, x], axis=0)
        w = w_ref[:, sl]
        y = w[C_CONV - 1:C_CONV] * x
        for s in range(1, C_CONV):
            y = y + w[C_CONV - 1 - s:C_CONV - s] * pltpu.roll(xh, s, 0)[SUBLANES:]
        return _pad_rows(_silu(y), c)

    def l2n(x):
        return x * lax.rsqrt(jnp.sum(x * x, axis=-1, keepdims=True) + EPS)

    def chunk(ci, t0):
        rows = pl.ds(t0, tc)
        zg = _pad_rows(zg_ref[0, rows, :], c)
        beta_all = _sigmoid(zg)
        la_all = -jnp.exp(al_ref[...]) * _softplus(zg + dl_ref[...])
        nv = nqk * rep
        hs = range(nv)
        qsls = [slice(qh * HD_C, (qh + 1) * HD_C) for qh in range(nqk)]
        vsls = [slice(h * HD_C, (h + 1) * HD_C) for h in hs]
        qs = [l2n(conv(q_ref, wq_ref, bq_ref, hq_ref, sl, ci, t0)) * HD_C ** -0.5 for sl in qsls]
        ks = [l2n(conv(k_ref, wk_ref, bk_ref, hk_ref, sl, ci, t0)) for sl in qsls]
        if tc < c:
            ks = [jnp.where(valid, k, 0.0) for k in ks]
        qbs = [q.astype(BF16) for q in qs]
        kbs = [k.astype(BF16) for k in ks]
        kks = [_dg(kb, kb, _NT) for kb in kbs]
        qks = [_dg(qb, kb, _NT) for qb, kb in zip(qbs, kbs)]
        vs = [conv(v_ref, wv_ref, bv_ref, hv_ref, sl, ci, t0) for sl in vsls]
        betas = [jnp.sum(jnp.where(lane == g * nv + h, beta_all, 0.0), axis=-1, keepdims=True) for h in hs]
        las = [jnp.sum(jnp.where(lane == H_C_V + g * nv + h, la_all, 0.0), axis=-1, keepdims=True) for h in hs]
        if tc < c:
            betas = [jnp.where(valid, x, 0.0) for x in betas]
            las = [jnp.where(valid, x, 0.0) for x in las]
        bc_all = _sel_dot(lower_incl, jnp.concatenate([jnp.broadcast_to(la, (c, LANES)) for la in las], axis=1))
        bcols = [bc_all[:, h * LANES:h * LANES + c] for h in hs]
        decs = [jnp.exp(jnp.where(row >= col, bc - bc.T, NEG)) for bc in bcols]
        b1s = [bc[:, 0:1] for bc in bcols]
        ebs = [jnp.exp(b1) for b1 in b1s]
        bends = [b1[c - 1:c] for b1 in b1s]
        ns = [jnp.where(row > col, betas[h] * (kks[h // rep] * decs[h]), 0.0) for h in hs]
        nds = [jnp.where(same_sub, n, 0.0) for n in ns]
        xs = [eye - nd for nd in nds]
        ps = nds
        for _ in range(int(math.log2(sb)) - 1):
            ps = [_dot1(p, p) for p in ps]
            xs = [x + _dot1(x, p) for x, p in zip(xs, ps)]
        nblk = c // sb
        if nblk > 1:
            mms = [_dot1(x, n - nd) for x, n, nd in zip(xs, ns, nds)]
            ys = [eye - mm for mm in mms]
            ps = mms
            for _ in range(int(math.log2(nblk)) - 1):
                ps = [_dot1(p, p) for p in ps]
                ys = [y + _dot1(y, p) for y, p in zip(ys, ps)]
            xs = [_dot1(y, x) for y, x in zip(ys, xs)]
        sts = [st_ref[h] for h in hs]
        stbs = [st.astype(BF16) for st in sts]
        ksts = [_dg(kbs[h // rep], stbs[h], _NT) for h in hs]
        qsts = [_dg(qbs[h // rep], stbs[h], _NT) for h in hs]
        rhss = [betas[h] * (vs[h] - ebs[h] * ksts[h]) for h in hs]
        ubs = [_dot1(x, rhs).astype(BF16) for x, rhs in zip(xs, rhss)]
        o_intra = [_dg((qks[h // rep] * decs[h]).astype(BF16), ubs[h]) for h in hs]
        upd = [_dg(ubs[h], (ks[h // rep] * jnp.exp(bends[h] - b1s[h])).astype(BF16), _TN) for h in hs]
        for h in hs:
            st_ref[h] = jnp.exp(bends[h]) * sts[h] + upd[h]
            o = ebs[h] * qsts[h] + o_intra[h]
            on = o * lax.rsqrt(jnp.mean(o * o, axis=-1, keepdims=True) + EPS) * nrm
            out = on * _silu(_pad_rows(z_ref[0, rows, vsls[h]], c))
            o_ref[0, rows, vsls[h]] = out[0:tc].astype(o_ref.dtype)

    _run_chunks(nchunks, c, chunk)

    @pl.when(pl.program_id(2) == pl.num_programs(2) - 1)
    def _():
        for h in range(nqk * rep):
            sfin_ref[0, h] = st_ref[h].T


def _delta(zc, col0, zg, conv_w, buf, a_log, dt_bias, nrm, s0, chunk):
    b, t, _ = zc.shape
    tb = min(t, MIXER_ROW_BLOCK)
    assert t % tb == 0 and tb % SUBLANES == 0
    c, tc, nchunks = _chunk_plan(tb, chunk)
    rep = H_C_V // H_C_QK
    nqk = DELTA_QK_HEADS_PER_STEP
    nv = nqk * rep
    ng = H_C_QK // nqk
    qw = nqk * HD_C
    vw = nv * HD_C
    assert W_CQK % qw == 0 and (2 * W_CQK) % vw == 0
    kq0 = W_CQK // qw
    v0 = 2 * W_CQK // vw
    z0 = v0 + W_CV // vw
    assert col0 % qw == 0 and col0 % vw == 0
    qb = col0 // qw
    vb = col0 // vw
    buf8 = jnp.pad(buf, ((0, 0), (SUBLANES - (C_CONV - 1), 0), (0, 0)))
    pad_l = jnp.zeros((H_C_V,), F32)
    a_lane = jnp.pad(jnp.concatenate([pad_l, a_log]), (0, LANES - 2 * H_C_V)).reshape(1, LANES)
    d_lane = jnp.pad(jnp.concatenate([pad_l, dt_bias]), (0, LANES - 2 * H_C_V)).reshape(1, LANES)
    hb = tb // SUBLANES

    def rows(col0):
        return lambda bi, g, ti: (bi, ti, col0 + g)

    def first(col0):
        return lambda bi, g, ti: (bi, 0, col0 + g)

    def halo(col0):
        return lambda bi, g, ti: (bi, jnp.maximum(ti * hb - 1, 0), col0 + g)

    full = lambda bi, g, ti: (0, 0)
    sblk = (1, nv, HD_C, HD_C)
    smap = lambda bi, g, ti: (bi, g, 0, 0)
    return pl.pallas_call(
        functools.partial(_delta_kernel, c=c, tc=tc, nchunks=nchunks),
        grid=(b, ng, t // tb),
        in_specs=[pl.BlockSpec((1, tb, qw), rows(qb)), pl.BlockSpec((1, tb, qw), rows(qb + kq0)),
                  pl.BlockSpec((1, tb, vw), rows(vb + v0)), pl.BlockSpec((1, tb, vw), rows(vb + z0)),
                  pl.BlockSpec((1, tb, W_CG), lambda bi, g, ti: (bi, ti, 0)),
                  pl.BlockSpec((C_CONV, qw), lambda bi, g, ti: (0, g)),
                  pl.BlockSpec((C_CONV, qw), lambda bi, g, ti: (0, kq0 + g)),
                  pl.BlockSpec((C_CONV, vw), lambda bi, g, ti: (0, v0 + g)),
                  pl.BlockSpec((1, SUBLANES, qw), first(0)), pl.BlockSpec((1, SUBLANES, qw), first(kq0)),
                  pl.BlockSpec((1, SUBLANES, vw), first(v0)),
                  pl.BlockSpec((1, SUBLANES, qw), halo(qb)), pl.BlockSpec((1, SUBLANES, qw), halo(qb + kq0)),
                  pl.BlockSpec((1, SUBLANES, vw), halo(vb + v0)),
                  pl.BlockSpec((1, LANES), full), pl.BlockSpec((1, LANES), full),
                  pl.BlockSpec((1, HD_C), full),
                  pl.BlockSpec(sblk, smap)],
        out_specs=[pl.BlockSpec((1, tb, vw), rows(0)), pl.BlockSpec(sblk, smap)],
        out_shape=[jax.ShapeDtypeStruct((b, t, W_CV), BF16), jax.ShapeDtypeStruct((b, H_C_V, HD_C, HD_C), F32)],
        scratch_shapes=[pltpu.VMEM((nv, HD_C, HD_C), F32)],
        compiler_params=_cparams("parallel", "parallel", "arbitrary"),
        name="gated_deltanet",
    )(zc, zc, zc, zc, zg, conv_w, conv_w, conv_w, buf8, buf8, buf8, zc, zc, zc,
      a_lane, d_lane, nrm.reshape(1, HD_C), s0)


def _ret_kernel(q_ref, k_ref, v_ref, g_ref, nrm_ref, s0_ref, o_ref, sfin_ref, st_ref, *, c, tc, nchunks):
    nh = st_ref.shape[0]
    row = lax.broadcasted_iota(jnp.int32, (c, c), 0)
    col = lax.broadcasted_iota(jnp.int32, (c, c), 1)
    steps = (jnp.minimum(row + 1, tc) - jnp.minimum(col + 1, tc)).astype(F32)
    r1 = lax.broadcasted_iota(jnp.int32, (c, 1), 0)
    nsteps = jnp.minimum(r1 + 1, tc).astype(F32)
    valid = r1 < tc
    nrm = nrm_ref[...]
    lgs, decs = [], []
    for h in range(nh):
        hf = jnp.full((1, 1), pl.program_id(1) * nh + h, jnp.int32).astype(F32)
        lg = jnp.log1p(-jnp.exp2(-5.0 - hf))
        lgs.append(lg)
        decs.append(jnp.exp(jnp.where(row >= col, steps * lg, NEG)))
        st_ref[h] = s0_ref[0, h].T

    def chunk(ci, t0):
        rows = pl.ds(t0, tc)
        hs = range(nh)
        sls = [slice(h * HD_D, (h + 1) * HD_D) for h in hs]
        b1s = [nsteps * lgs[h] for h in hs]
        bends = [float(tc) * lgs[h] for h in hs]
        qs = [_pad_rows(q_ref[0, rows, sl], c).astype(BF16) for sl in sls]
        ks = [_pad_rows(k_ref[0, rows, sl], c) * HD_D ** -0.5 for sl in sls]
        if tc < c:
            ks = [jnp.where(valid, k, 0.0) for k in ks]
        vs = [_pad_rows(v_ref[0, rows, sl], c).astype(BF16) for sl in sls]
        sts = [st_ref[h] for h in hs]
        atts = [_dg(qs[h], ks[h].astype(BF16), _NT) * decs[h] for h in hs]
        o_inter = [_dg(qs[h], sts[h].astype(BF16), _NT) for h in hs]
        o_intra = [_dg(atts[h].astype(BF16), vs[h]) for h in hs]
        upd = [_dg(vs[h], (ks[h] * jnp.exp(bends[h] - b1s[h])).astype(BF16), _TN) for h in hs]
        for h in hs:
            sl = sls[h]
            st_ref[h] = jnp.exp(bends[h]) * sts[h] + upd[h]
            o = jnp.exp(b1s[h]) * o_inter[h] + o_intra[h]
            mu = jnp.mean(o, axis=-1, keepdims=True)
            oc = o - mu
            var = jnp.mean(oc * oc, axis=-1, keepdims=True)
            on = oc * lax.rsqrt(var + EPS) * nrm
            out = on * _silu(_pad_rows(g_ref[0, rows, sl], c))
            o_ref[0, rows, sl] = out[0:tc].astype(o_ref.dtype)

    _run_chunks(nchunks, c, chunk)
    for h in range(nh):
        sfin_ref[0, h] = st_ref[h].T


def _retention(zd, nrm, s0, chunk):
    b, t, _ = zd.shape
    c, tc, nchunks = _chunk_plan(t, chunk)
    nh = RET_HEADS_PER_STEP
    ng = H_D // nh
    blk = (1, t, nh * HD_D)
    sblk = (1, nh, HD_D, HD_D)
    return pl.pallas_call(
        functools.partial(_ret_kernel, c=c, tc=tc, nchunks=nchunks),
        grid=(b, ng),
        in_specs=[pl.BlockSpec(blk, lambda bi, g: (bi, 0, g)),
                  pl.BlockSpec(blk, lambda bi, g: (bi, 0, ng + g)),
                  pl.BlockSpec(blk, lambda bi, g: (bi, 0, 2 * ng + g)),
                  pl.BlockSpec(blk, lambda bi, g: (bi, 0, 3 * ng + g)),
                  pl.BlockSpec((1, HD_D), lambda bi, g: (0, 0)),
                  pl.BlockSpec(sblk, lambda bi, g: (bi, g, 0, 0))],
        out_specs=[pl.BlockSpec(blk, lambda bi, g: (bi, 0, g)),
                   pl.BlockSpec(sblk, lambda bi, g: (bi, g, 0, 0))],
        out_shape=[jax.ShapeDtypeStruct((b, t, W_D), BF16), jax.ShapeDtypeStruct((b, H_D, HD_D, HD_D), F32)],
        scratch_shapes=[pltpu.VMEM((nh, HD_D, HD_D), F32)],
        compiler_params=_cparams("parallel", "arbitrary"),
        name="retention",
    )(zd, zd, zd, zd, nrm.reshape(1, HD_D), s0)


def _mixers(z, cache, s_hgrn, s_delta, buf_delta, s_ret, p, layer):
    zabc, zg, zd = z
    bsz, t, _ = zabc.shape
    col_b = 3 * W_A
    col_c = col_b + 4 * W_B
    if cache is None:
        o_a = _attention_prompt(zabc)
        kv_new = zabc
    else:
        o_a = _attention_sample(zabc, cache, layer)
        kv_new = zabc[:, :, W_A:3 * W_A].reshape(bsz, t, 2, H_A, HD_A)
    o_b, s_hgrn_new = _hgrn(zabc, col_b, p['hgrn_lb'], p['hgrn_norm'][layer], s_hgrn, layer, GLA_CHUNK)
    o_c, s_delta_new = _delta(zabc, col_c, zg, p['delta_conv'][layer], buf_delta, p['delta_A_log'][layer],
                              p['delta_dt_bias'][layer], p['delta_norm'][layer], s_delta, DELTA_CHUNK)
    pre = jnp.concatenate([buf_delta, zabc[:, :, col_c:col_c + 2 * W_CQK + W_CV]], axis=1)
    buf_delta_new = pre[:, pre.shape[1] - (C_CONV - 1):]
    o_d, s_ret_new = _retention(zd, p['ret_norm'][layer], s_ret, RET_CHUNK)
    parts = [o.reshape(bsz * t, -1) for o in (o_a, o_b, o_c, o_d)]
    return parts, (kv_new, s_hgrn_new, s_delta_new, buf_delta_new, s_ret_new)


def _layer(xs, pes, cache, states, p, layer):
    shapes = [x.shape for x in xs]
    d = shapes[0][2]
    x2 = [x.reshape(-1, d) for x in xs]
    hn = [_rmsnorm(x, p['attn_norm'][layer], BF16) for x in x2]
    z = [_matmul(hn[0], hn[1], p[name], layer) for name in ('w_in_abc', 'w_in_g', 'w_in_d')]
    parts, mixed = [], []
    for g in range(2):
        bsz, t, _ = shapes[g]
        zg = [zz[g].reshape(bsz, t, -1) for zz in z]
        s_hgrn, s_delta, buf_delta, s_ret, _ = states[g]
        pg, mg = _mixers(zg, cache if g == 1 else None, s_hgrn, s_delta, buf_delta, s_ret, p, layer)
        parts.append(pg)
        mixed.append(mg)
    x2 = _matmul_residual(parts[0], parts[1], p['w_out'], layer, x2[0], x2[1],
                          (1024, 512, 256, 64), (512, 256, 128))

    hf = [_rmsnorm(x, p['ffn_norm'][layer], BF16) for x in x2]
    act_p, tail_p, act_s, tail_s, w_down_bf16 = _ffn_gate_up(
        hf[0], hf[1], p['w_gate'], p['w_up'], p['ffn_conv'], p['w_down'], layer,
        shapes[0][1], states[1][4], shapes[1][1])
    x2 = _matmul_residual([act_p], [act_s], w_down_bf16, 0, x2[0], x2[1], (512, 256, 64), (256, 128))

    hp = [_rmsnorm(x, p['ple_norm'][layer], BF16) for x in x2]
    pe2 = [pe.reshape(-1, pe.shape[-1]).astype(BF16) for pe in pes]
    x2 = _ple(hp[0], hp[1], p['ple_gate'], pe2[0], pe2[1], p['ple_proj'], layer, x2[0], x2[1])
    new_states = [mixed[0] + (tail_p,), mixed[1] + (tail_s,)]
    return [x.reshape(s) for x, s in zip(x2, shapes)], new_states


def kernel(x_prompt, x_sample, cache_attn_kv, state_hgrn, state_delta, state_delta_conv, state_ret,
           state_ffn_conv, p_prompt, p_sample, attn_norm, w_in, hgrn_lb, hgrn_norm, delta_conv,
           delta_A_log, delta_dt_bias, delta_norm, ret_norm, w_out, ffn_norm, w_gate, w_up, ffn_conv,
           w_down, ple_norm, ple_gate, ple_proj, final_norm):
    depth = w_in.shape[0]
    bp = x_prompt.shape[0]
    o_b = 3 * W_A
    o_c = o_b + 4 * W_B
    o_g = o_c + 2 * W_CQK + 2 * W_CV
    o_d = o_g + 2 * H_C_V
    w_in_t = jnp.swapaxes(w_in, 1, 2)
    w_in_d, w_in_g = _cast_tail_t(w_in_t, o_g, o_d - o_g)
    p = {'attn_norm': attn_norm,
         'w_in_abc': _cast_bf16_t(w_in_t, 0, o_g), 'w_in_g': w_in_g, 'w_in_d': w_in_d,
         'hgrn_lb': hgrn_lb, 'hgrn_norm': hgrn_norm, 'delta_conv': delta_conv, 'delta_A_log': delta_A_log,
         'delta_dt_bias': delta_dt_bias, 'delta_norm': delta_norm, 'ret_norm': ret_norm,
         'w_out': w_out, 'ffn_norm': ffn_norm, 'w_gate': w_gate, 'w_up': w_up, 'ffn_conv': ffn_conv,
         'w_down': w_down, 'ple_norm': ple_norm, 'ple_gate': ple_gate, 'ple_proj': ple_proj}
    xs = [x_prompt, x_sample]
    st_p, st_s = [], []
    for l in range(depth):
        zero = lambda *s: jnp.zeros((bp,) + s, F32)
        states = [(zero(H_B, HD_B, HD_B), zero(H_C_V, HD_C, HD_C), zero(C_CONV - 1, 2 * W_CQK + W_CV),
                   zero(H_D, HD_D, HD_D), None),
                  (state_hgrn[l], state_delta[l], state_delta_conv[l], state_ret[l], state_ffn_conv[l])]
        xs, (sp, ss) = _layer(xs, [p_prompt[l], p_sample[l]], cache_attn_kv, states, p, l)
        st_p.append(sp)
        st_s.append(ss)
    xp, xs = xs

    def stack(sts, i):
        return jnp.stack([s[i] for s in sts])

    def final(x):
        return _rmsnorm(x.reshape(-1, x.shape[-1]), final_norm, F32).reshape(x.shape)

    kv_sample = _kv_shift(cache_attn_kv, stack(st_s, 0))
    s_p = x_prompt.shape[1]
    kv_prompt = _kv_pack([s[0] for s in st_p], min(A_BRANCHES[-1][0], s_p))
    return (final(xp), final(xs),
            kv_prompt, kv_sample, stack(st_p, 1), stack(st_s, 1), stack(st_p, 2), stack(st_s, 2),
            stack(st_p, 3), stack(st_s, 3), stack(st_p, 4), stack(st_s, 4), stack(st_p, 5), stack(st_s, 5))
```
